```python
import jax, jax.numpy as jnp
from jax import lax
import numpy as np


D_MODEL = 1024
BATCH = 8
SEQ = 2048
DEPTH = 1

D_FF = 2816
M_HEADS = 8
M_HEAD_DIM = 128
M_WIDTH = M_HEADS * M_HEAD_DIM
G_HEADS = 8
G_HEAD_DIM = 128
G_WIDTH = G_HEADS * G_HEAD_DIM
CONV_K = 5
CHUNK = 64
N_ADA = 3
EPS = 1e-6
FFN_RES = 0.5
PROJ_SIZES = (M_WIDTH, M_WIDTH, M_WIDTH, M_WIDTH, 4 * M_HEADS,
              G_WIDTH, G_WIDTH, G_WIDTH, G_WIDTH, 4 * G_HEADS,
              D_MODEL, D_MODEL)
PROJ_DIM = sum(PROJ_SIZES)

kernel_name = 'hybrid_mlstm_gdn_macaron_block'


def rms_norm(x, w):
    xf = x.astype(jnp.float32)
    y = xf * lax.rsqrt(jnp.mean(xf * xf, axis=-1, keepdims=True) + EPS)
    return y.astype(x.dtype) * w


def l2_norm(x):
    return x * lax.rsqrt(jnp.sum(x * x, axis=-1, keepdims=True) + EPS)


def modulate(h, shift, scale):
    return h * (1 + scale[:, None, :]) + shift[:, None, :]


def swiglu(h, w_in, w_out):
    gate, up = jnp.split(h @ w_in, 2, axis=-1)
    return (jax.nn.silu(gate) * up) @ w_out


def _chunk(t):
    b, s = t.shape[:2]
    t = t.reshape((b, s // CHUNK, CHUNK) + t.shape[2:])
    if t.ndim == 5:
        return t.transpose(1, 0, 3, 2, 4)
    return t.transpose(1, 0, 3, 2)


def _unchunk(t):
    nc, b, h, l, d = t.shape
    return t.transpose(1, 0, 3, 2, 4).reshape(b, nc * l, h, d)


def centred_dwconv(x, w):
    return lax.conv_general_dilated(
        x, w[:, None, :], window_strides=(1,),
        padding=[(CONV_K // 2, CONV_K // 2)],
        dimension_numbers=('NWC', 'WIO', 'NWC'),
        feature_group_count=x.shape[-1])


def mlstm_dir(q, k, v, i_pre, f_pre):
    b, s, h, dk = q.shape
    dv = v.shape[-1]
    qc, kc, vc = _chunk(q), _chunk(k), _chunk(v)
    ic = _chunk(i_pre)
    bc = jnp.cumsum(_chunk(jax.nn.log_sigmoid(f_pre)), axis=-1)
    causal = jnp.tril(jnp.ones((CHUNK, CHUNK), bool))

    def step(carry, xs):
        cmat, nvec, m = carry
        qj, kj, vj, ij, bj = xs
        dmat = jnp.where(causal, bj[..., :, None] - bj[..., None, :] + ij[..., None, :], -jnp.inf)
        inter = bj + m[..., None]
        m_t = jnp.maximum(inter, dmat.max(-1))
        p = jnp.exp(dmat - m_t[..., None])
        sc = jnp.einsum('bhtd,bhsd->bhts', qj, kj) * p
        w_inter = jnp.exp(inter - m_t)
        num = (jnp.einsum('bhts,bhsv->bhtv', sc, vj)
               + w_inter[..., None] * jnp.einsum('bhtk,bhkv->bhtv', qj, cmat))
        den = sc.sum(-1) + w_inter * jnp.einsum('bhtk,bhk->bht', qj, nvec)
        hj = num / jnp.maximum(jnp.abs(den), jnp.exp(-m_t))[..., None]
        g = bj[..., -1]
        a = g[..., None] - bj + ij
        m_new = jnp.maximum(g + m, a.max(-1))
        kw = kj * jnp.exp(a - m_new[..., None])[..., None]
        decay = jnp.exp(g + m - m_new)
        cmat = decay[..., None, None] * cmat + jnp.einsum('bhsk,bhsv->bhkv', kw, vj)
        nvec = decay[..., None] * nvec + kw.sum(axis=-2)
        return (cmat, nvec, m_new), hj

    init = (jnp.zeros((b, h, dk, dv), jnp.float32),
            jnp.zeros((b, h, dk), jnp.float32),
            jnp.zeros((b, h), jnp.float32))
    _, hs = lax.scan(step, init, (qc, kc, vc, ic, bc))
    return _unchunk(hs)


def gdn_dir(q, k, v, g, beta):
    b, s, h, dk = q.shape
    dv = v.shape[-1]
    qc, kc, vc = _chunk(q), _chunk(k), _chunk(v)
    bc = _chunk(beta)
    gam = jnp.cumsum(_chunk(g), axis=-1)
    diff = gam[..., :, None] - gam[..., None, :]
    row = jnp.arange(CHUNK)
    strict = row[:, None] > row[None, :]
    incl = row[:, None] >= row[None, :]
    kk = jnp.einsum('nbhtd,nbhsd->nbhts', kc, kc)
    a_mat = bc[..., :, None] * kk * jnp.exp(jnp.where(strict, diff, -jnp.inf))
    m_mat = a_mat + jnp.eye(CHUNK, dtype=a_mat.dtype)
    rhs = jnp.concatenate([bc[..., None] * vc, (bc * jnp.exp(gam))[..., None] * kc], axis=-1)
    sol = lax.linalg.triangular_solve(m_mat, rhs, left_side=True, lower=True, unit_diagonal=True)
    u, w = sol[..., :dv], sol[..., dv:]
    attn = jnp.einsum('nbhtd,nbhsd->nbhts', qc, kc) * jnp.exp(jnp.where(incl, diff, -jnp.inf))
    q_dec = qc * jnp.exp(gam)[..., None]
    g_last = gam[..., -1]
    k_dec = kc * jnp.exp(g_last[..., None] - gam)[..., None]

    def step(state, xs):
        uj, wj, aj, qj, kj, gj = xs
        v_new = uj - jnp.einsum('bhtk,bhkv->bhtv', wj, state)
        o = jnp.einsum('bhtk,bhkv->bhtv', qj, state) + jnp.einsum('bhts,bhsv->bhtv', aj, v_new)
        state = jnp.exp(gj)[..., None, None] * state + jnp.einsum('bhsk,bhsv->bhkv', kj, v_new)
        return state, o

    s0 = jnp.zeros((b, h, dk, dv), jnp.float32)
    _, os = lax.scan(step, s0, (u, w, attn, q_dec, k_dec, g_last))
    return _unchunk(os)


def token_mix(h, w_in, mlstm_gate_bias, gdn_a_log, gdn_dt_bias, gdn_conv_w,
              mlstm_out_norm, gdn_out_norm, w_branch_mlstm, w_branch_gdn, w_out):
    b, s, _ = h.shape
    f32 = jnp.float32
    split_idx = [int(i) for i in np.cumsum(PROJ_SIZES)[:-1]]
    (mq, mk, mv, mo, mg, gq, gk, gv, gz, gg, merge_m, merge_g) = jnp.split(h @ w_in, split_idx, axis=-1)

    def heads(t, nh):
        return t.reshape(b, s, nh, -1)

    def flip(t):
        return jnp.flip(t, axis=1)

    q = heads(mq, M_HEADS).astype(f32) * (M_HEAD_DIM ** -0.5)
    k = heads(mk, M_HEADS).astype(f32)
    v = heads(mv, M_HEADS).astype(f32)
    mgates = mg.astype(f32).reshape(b, s, 4, M_HEADS) + mlstm_gate_bias.astype(f32)
    h_m = (mlstm_dir(q, k, v, mgates[:, :, 0], mgates[:, :, 1])
           + flip(mlstm_dir(flip(q), flip(k), flip(v), flip(mgates[:, :, 2]), flip(mgates[:, :, 3]))))
    h_m = rms_norm(h_m, mlstm_out_norm.astype(f32)) * jax.nn.sigmoid(heads(mo, M_HEADS).astype(f32))
    h_m = h_m.reshape(b, s, M_WIDTH).astype(h.dtype)

    qkv = jax.nn.silu(centred_dwconv(jnp.concatenate([gq, gk, gv], axis=-1), gdn_conv_w))
    cq, ck, cv = jnp.split(qkv, 3, axis=-1)
    q = l2_norm(heads(cq, G_HEADS).astype(f32)) * (G_HEAD_DIM ** -0.5)
    k = l2_norm(heads(ck, G_HEADS).astype(f32))
    v = heads(cv, G_HEADS).astype(f32)
    ggates = gg.astype(f32).reshape(b, s, 4, G_HEADS)
    a_log = gdn_a_log.astype(f32)
    dt_b = gdn_dt_bias.astype(f32)
    g_f = -jnp.exp(a_log[0]) * jax.nn.softplus(ggates[:, :, 0] + dt_b[0])
    beta_f = jax.nn.sigmoid(ggates[:, :, 1])
    g_b = -jnp.exp(a_log[1]) * jax.nn.softplus(ggates[:, :, 2] + dt_b[1])
    beta_b = jax.nn.sigmoid(ggates[:, :, 3])
    o = (gdn_dir(q, k, v, g_f, beta_f)
         + flip(gdn_dir(flip(q), flip(k), flip(v), flip(g_b), flip(beta_b))))
    h_g = rms_norm(o, gdn_out_norm.astype(f32)) * jax.nn.silu(heads(gz, G_HEADS).astype(f32))
    h_g = h_g.reshape(b, s, G_WIDTH).astype(h.dtype)

    y = (jax.nn.sigmoid(merge_m) * (h_m @ w_branch_mlstm)
         + jax.nn.sigmoid(merge_g) * (h_g @ w_branch_gdn))
    return y @ w_out


def setup_inputs(seed: int = 0) -> dict:
    key = jax.random.key(seed)
    ks = jax.random.split(key, 24)
    f32 = jnp.float32

    def nrm(k, shape, scale):
        return jax.random.normal(k, shape, f32) * scale

    def gain(k, shape):
        return 1.0 + 0.02 * jax.random.normal(k, shape, f32)

    gate_offsets = jnp.array([0.0, 3.0, 0.0, 3.0], f32)[None, :, None]
    dt = jnp.exp(jax.random.uniform(ks[11], (DEPTH, 2, G_HEADS), f32,
                                    float(np.log(1e-3)), float(np.log(1e-1))))
    return {
        'x': nrm(ks[0], (BATCH, SEQ, D_MODEL), 1.0),
        'c': nrm(ks[1], (BATCH, D_MODEL), 1.0),
        'w_ada': nrm(ks[2], (DEPTH, D_MODEL, N_ADA * 3 * D_MODEL), 0.5 * D_MODEL ** -0.5),
        'b_ada': nrm(ks[3], (DEPTH, N_ADA * 3 * D_MODEL), 0.02),
        'norm_ffn1': gain(ks[4], (DEPTH, D_MODEL)),
        'w_ffn1_in': nrm(ks[5], (DEPTH, D_MODEL, 2 * D_FF), D_MODEL ** -0.5),
        'w_ffn1_out': nrm(ks[6], (DEPTH, D_FF, D_MODEL), D_FF ** -0.5),
        'norm_mix': gain(ks[7], (DEPTH, D_MODEL)),
        'w_in': nrm(ks[8], (DEPTH, D_MODEL, PROJ_DIM), D_MODEL ** -0.5),
        'mlstm_gate_bias': gate_offsets + nrm(ks[9], (DEPTH, 4, M_HEADS), 0.1),
        'gdn_a_log': jnp.log(jax.random.uniform(ks[10], (DEPTH, 2, G_HEADS), f32, 1.0, 16.0)),
        'gdn_dt_bias': dt + jnp.log(-jnp.expm1(-dt)),
        'gdn_conv_w': nrm(ks[12], (DEPTH, CONV_K, 3 * G_WIDTH), CONV_K ** -0.5),
        'mlstm_out_norm': gain(ks[13], (DEPTH, M_HEAD_DIM)),
        'gdn_out_norm': gain(ks[14], (DEPTH, G_HEAD_DIM)),
        'w_branch_mlstm': nrm(ks[15], (DEPTH, M_WIDTH, D_MODEL), M_WIDTH ** -0.5),
        'w_branch_gdn': nrm(ks[16], (DEPTH, G_WIDTH, D_MODEL), G_WIDTH ** -0.5),
        'w_out': nrm(ks[17], (DEPTH, D_MODEL, D_MODEL), D_MODEL ** -0.5),
        'norm_ffn2': gain(ks[18], (DEPTH, D_MODEL)),
        'w_ffn2_in': nrm(ks[19], (DEPTH, D_MODEL, 2 * D_FF), D_MODEL ** -0.5),
        'w_ffn2_out': nrm(ks[20], (DEPTH, D_FF, D_MODEL), D_FF ** -0.5),
        'norm_final': gain(ks[21], (D_MODEL,)),
    }


def reference(x, c, w_ada, b_ada, norm_ffn1, w_ffn1_in, w_ffn1_out, norm_mix, w_in,
              mlstm_gate_bias, gdn_a_log, gdn_dt_bias, gdn_conv_w, mlstm_out_norm,
              gdn_out_norm, w_branch_mlstm, w_branch_gdn, w_out, norm_ffn2, w_ffn2_in,
              w_ffn2_out, norm_final):
    b = x.shape[0]
    cs = jax.nn.silu(c)
    for l in range(DEPTH):
        mod = (cs @ w_ada[l] + b_ada[l]).reshape(b, N_ADA, 3, D_MODEL)
        h = modulate(rms_norm(x, norm_ffn1[l]), mod[:, 0, 0], mod[:, 0, 1])
        x = x + FFN_RES * mod[:, 0, 2][:, None, :] * swiglu(h, w_ffn1_in[l], w_ffn1_out[l])
        h = modulate(rms_norm(x, norm_mix[l]), mod[:, 1, 0], mod[:, 1, 1])
        x = x + mod[:, 1, 2][:, None, :] * token_mix(
            h, w_in[l], mlstm_gate_bias[l], gdn_a_log[l], gdn_dt_bias[l], gdn_conv_w[l],
            mlstm_out_norm[l], gdn_out_norm[l], w_branch_mlstm[l], w_branch_gdn[l], w_out[l])
        h = modulate(rms_norm(x, norm_ffn2[l]), mod[:, 2, 0], mod[:, 2, 1])
        x = x + FFN_RES * mod[:, 2, 2][:, None, :] * swiglu(h, w_ffn2_in[l], w_ffn2_out[l])
    return rms_norm(x, norm_final)
```

```python
import functools

import jax
import jax.numpy as jnp
from jax import lax
from jax.experimental import pallas as pl
from jax.experimental.pallas import tpu as pltpu

F32 = jnp.float32
BF16 = jnp.bfloat16

EPS = 1e-6
FFN_RES = 0.5
HEADS = 8
HEAD_DIM = 128
CONV_K = 5
LANES = 128
M_CHUNK = 128
G_CHUNK = 64
VMEM_LIMIT = 56 * 1024 * 1024

M_X_F, M_C_F, M_X_B, M_C_B = 0, 1, 2, 3
G_C_F, G_X_F, G_C_B, G_X_B = 0, 1, 2, 3


def _dot(a, b):
    return jnp.dot(a, b, preferred_element_type=F32)


def _dot_nt(a, b):
    return lax.dot_general(a, b, (((1,), (1,)), ((), ())), preferred_element_type=F32)


def _dot_tn(a, b):
    return lax.dot_general(a, b, (((0,), (0,)), ((), ())), preferred_element_type=F32)


def _rms(x, w):
    return (x * lax.rsqrt(jnp.mean(x * x, axis=-1, keepdims=True) + EPS)) * w


def _sigmoid(x):
    return 1.0 / (1.0 + jnp.exp(-x))


def _params(*sem):
    return pltpu.CompilerParams(dimension_semantics=sem, vmem_limit_bytes=VMEM_LIMIT)


def _ada_kernel(c_ref, w_ref, b_ref, o_ref):
    c = c_ref[...]
    cs = (c * _sigmoid(c)).astype(BF16)
    o_ref[...] = _dot(cs, w_ref[...].astype(BF16)) + b_ref[...]


def _ada(c, w_ada, b_ada):
    b, d = c.shape
    n = w_ada.shape[1]
    tn = 1024
    return pl.pallas_call(
        _ada_kernel,
        grid=(n // tn,),
        in_specs=[pl.BlockSpec((b, d), lambda j: (0, 0)),
                  pl.BlockSpec((d, tn), lambda j: (0, j)),
                  pl.BlockSpec((1, tn), lambda j: (0, j))],
        out_specs=pl.BlockSpec((b, tn), lambda j: (0, j)),
        out_shape=jax.ShapeDtypeStruct((b, n), F32),
        compiler_params=_params("parallel"),
        name="ada",
    )(c, w_ada, b_ada.reshape(1, n))


def _ffn_kernel(x_ref, mod_ref, nw_ref, wg_ref, wu_ref, wo_ref, nf_ref, o_ref, *, sub, final):
    x = x_ref[0]
    shift = mod_ref[0, 3 * sub + 0:3 * sub + 1, :]
    scale = mod_ref[0, 3 * sub + 1:3 * sub + 2, :]
    gate = mod_ref[0, 3 * sub + 2:3 * sub + 3, :]
    h = (_rms(x, nw_ref[...]) * (1.0 + scale) + shift).astype(BF16)
    nf = wg_ref.shape[0]

    def body(f, acc):
        g = _dot(h, wg_ref[f])
        u = _dot(h, wu_ref[f])
        a = (g * _sigmoid(g) * u).astype(BF16)
        return acc + _dot(a, wo_ref[f])

    acc = lax.fori_loop(0, nf, body, jnp.zeros(x.shape, F32))
    y = x + FFN_RES * gate * acc
    if final:
        y = _rms(y, nf_ref[...])
    o_ref[0] = y


def _ffn(x, mod, norm_w, w_in, w_out, norm_final, *, sub, final):
    b, s, d = x.shape
    f = w_out.shape[0]
    fc = 256
    nfc = f // fc
    tm = min(512, s)
    wg = w_in[:, :f].astype(BF16).reshape(d, nfc, fc).transpose(1, 0, 2)
    wu = w_in[:, f:].astype(BF16).reshape(d, nfc, fc).transpose(1, 0, 2)
    wo = w_out.astype(BF16).reshape(nfc, fc, d)
    const3 = lambda i, j: (0, 0, 0)
    const2 = lambda i, j: (0, 0)
    single = pl.Buffered(1)
    return pl.pallas_call(
        functools.partial(_ffn_kernel, sub=sub, final=final),
        grid=(b, s // tm),
        in_specs=[pl.BlockSpec((1, tm, d), lambda i, j: (i, j, 0)),
                  pl.BlockSpec((1, 9, d), lambda i, j: (i, 0, 0)),
                  pl.BlockSpec((1, d), const2),
                  pl.BlockSpec((nfc, d, fc), const3, pipeline_mode=single),
                  pl.BlockSpec((nfc, d, fc), const3, pipeline_mode=single),
                  pl.BlockSpec((nfc, fc, d), const3, pipeline_mode=single),
                  pl.BlockSpec((1, d), const2)],
        out_specs=pl.BlockSpec((1, tm, d), lambda i, j: (i, j, 0)),
        out_shape=jax.ShapeDtypeStruct((b, s, d), F32),
        compiler_params=_params("parallel", "parallel"),
        name="ffn_final" if final else "ffn",
    )(x, mod, norm_w.reshape(1, d), wg, wu, wo, norm_final.reshape(1, d))


def _proj_kernel(x_ref, mod_ref, nw_ref, w_ref, wgate_ref, o_ref, gate_ref, h_scr):
    @pl.when(pl.program_id(2) == 0)
    def _():
        shift = mod_ref[0, 3:4, :]
        scale = mod_ref[0, 4:5, :]
        h = (_rms(x_ref[0], nw_ref[...]) * (1.0 + scale) + shift).astype(BF16)
        h_scr[...] = h
        gate_ref[0] = _dot(h, wgate_ref[...])

    o_ref[0] = _dot(h_scr[...], w_ref[...]).astype(BF16)


def _proj(x, mod, norm_w, w_big, w_gates):
    b, s, d = x.shape
    n = w_big.shape[1]
    tm, tn = min(1024, s), 2048
    return pl.pallas_call(
        _proj_kernel,
        grid=(b, s // tm, n // tn),
        in_specs=[pl.BlockSpec((1, tm, d), lambda i, j, k: (i, j, 0)),
                  pl.BlockSpec((1, 9, d), lambda i, j, k: (i, 0, 0)),
                  pl.BlockSpec((1, d), lambda i, j, k: (0, 0)),
                  pl.BlockSpec((d, tn), lambda i, j, k: (0, k)),
                  pl.BlockSpec((d, LANES), lambda i, j, k: (0, 0))],
        out_specs=[pl.BlockSpec((1, tm, tn), lambda i, j, k: (i, j, k)),
                   pl.BlockSpec((1, tm, LANES), lambda i, j, k: (i, j, 0))],
        out_shape=[jax.ShapeDtypeStruct((b, s, n), BF16),
                   jax.ShapeDtypeStruct((b, s, LANES), F32)],
        scratch_shapes=[pltpu.VMEM((tm, d), BF16)],
        compiler_params=_params("parallel", "parallel", "arbitrary"),
        name="proj",
    )(x, mod, norm_w.reshape(1, d), w_big, w_gates)


def _split_dot(tri, y):
    hi = y.astype(BF16)
    r1 = y - hi.astype(F32)
    mid = r1.astype(BF16)
    lo = (r1 - mid.astype(F32)).astype(BF16)
    return _dot(tri, hi) + _dot(tri, mid) + _dot(tri, lo)


def _gate_prep_kernel(g_ref, p_ref, o_ref, y_scr, *, lm, lg):
    s = g_ref.shape[1]
    lane = lax.broadcasted_iota(jnp.int32, (1, LANES), 1)
    group = lane // HEADS
    pre = g_ref[0] + p_ref[0:1, :]
    a_coef = -jnp.exp(p_ref[1:2, :])
    tail = jnp.log(1.0 + jnp.exp(-jnp.abs(pre)))
    sp = jnp.maximum(pre, 0.0) + tail
    log_sig = jnp.minimum(pre, 0.0) - tail
    is_logf = (group == 1) | (group == 3)
    is_a = (group == 4) | (group == 6)
    is_beta = (group == 5) | (group == 7)
    y = jnp.where(is_logf, log_sig, pre)
    y = jnp.where(is_a, a_coef * sp, y)
    y = jnp.where(is_beta, _sigmoid(pre), y)
    y_scr[...] = y
    is_prefix = (group == 1) | (group == 4)
    is_suffix = (group == 3) | (group == 6)

    def cumsum_pass(l, lanes_sel):
        t = lax.broadcasted_iota(jnp.int32, (l, l), 0)
        u = lax.broadcasted_iota(jnp.int32, (l, l), 1)
        tri = (t >= u).astype(BF16)

        def body(c, carry):
            r0 = pl.multiple_of(c * l, l)
            yc = y_scr[pl.ds(r0, l), :]
            pre_c = _split_dot(tri, yc)
            tot = pre_c[l - 1:l, :]
            suf_c = tot - pre_c + yc
            out = jnp.where(is_prefix, pre_c, jnp.where(is_suffix, suf_c, yc))
            cur = o_ref[0, pl.ds(r0, l), :]
            o_ref[0, pl.ds(r0, l), :] = jnp.where(lanes_sel, out, cur)
            return carry

        lax.fori_loop(0, s // l, body, 0)

    o_ref[0] = y
    cumsum_pass(lm, group < 4)
    cumsum_pass(lg, group >= 4)


def _gate_prep(gates_raw, params):
    b, s, _ = gates_raw.shape
    return pl.pallas_call(
        functools.partial(_gate_prep_kernel, lm=M_CHUNK, lg=G_CHUNK),
        grid=(b,),
        in_specs=[pl.BlockSpec((1, s, LANES), lambda i: (i, 0, 0)),
                  pl.BlockSpec((8, LANES), lambda i: (0, 0))],
        out_specs=pl.BlockSpec((1, s, LANES), lambda i: (i, 0, 0)),
        out_shape=jax.ShapeDtypeStruct((b, s, LANES), F32),
        scratch_shapes=[pltpu.VMEM((s, LANES), F32)],
        compiler_params=_params("parallel"),
        name="gate_prep",
    )(gates_raw, params)


def _per_head_gates(g, l):
    b, s = g.shape[:2]
    cols = jnp.pad(g.transpose(0, 3, 1, 2), ((0, 0), (0, 0), (0, 0), (0, 4)))
    rows = g.transpose(0, 3, 2, 1).reshape(b, HEADS, 4, s // l, l).transpose(0, 1, 3, 2, 4)
    rows = jnp.pad(rows, ((0, 0), (0, 0), (0, 0), (0, 4), (0, 0)))
    return cols, rows


def _mlstm_kernel(q_ref, k_ref, v_ref, og_ref, gcol_ref, grow_ref, nw_ref, o_ref,
                  hf_scr, hb_scr, cf_scr, cb_scr, *, l, nc):
    scale = HEAD_DIM ** -0.5
    cf_scr[...] = jnp.zeros_like(cf_scr)
    cb_scr[...] = jnp.zeros_like(cb_scr)
    t_idx = lax.broadcasted_iota(jnp.int32, (l, l), 0)
    s_idx = lax.broadcasted_iota(jnp.int32, (l, l), 1)
    ones_col = (lax.broadcasted_iota(jnp.int32, (l, LANES), 1) == 0).astype(BF16)

    def one_dir(c, m, fwd, c_scr, h_scr):
        r0 = pl.multiple_of(c * l, l)
        q = q_ref[0, pl.ds(r0, l), :]
        k = k_ref[0, pl.ds(r0, l), :]
        v_aug = jnp.concatenate([v_ref[0, pl.ds(r0, l), :], ones_col], axis=-1)
        cols = gcol_ref[0, 0, pl.ds(r0, l), :]
        rows = grow_ref[0, 0, c]
        xi, ci = (M_X_F, M_C_F) if fwd else (M_X_B, M_C_B)
        i_col, b_col = cols[:, xi:xi + 1], cols[:, ci:ci + 1]
        i_row, b_row = rows[xi:xi + 1, :], rows[ci:ci + 1, :]
        mask = (t_idx >= s_idx) if fwd else (t_idx <= s_idx)
        dmat = jnp.where(mask, b_col - b_row + i_row, -jnp.inf)
        inter = b_col + m
        m_t = jnp.maximum(inter, jnp.max(dmat, axis=-1, keepdims=True))
        p = jnp.exp(dmat - m_t)
        sc = (_dot_nt(q, k) * scale * p).astype(BF16)
        w_inter = jnp.exp(inter - m_t) * scale
        cmat = c_scr[...]
        num_aug = _dot(sc, v_aug) + w_inter * _dot(q, cmat.astype(BF16))
        den = num_aug[:, HEAD_DIM:HEAD_DIM + 1]
        h_scr[pl.ds(r0, l), :] = num_aug[:, :HEAD_DIM] / jnp.maximum(jnp.abs(den), jnp.exp(-m_t))
        g = b_row[:, l - 1:l] if fwd else b_row[:, 0:1]
        m_new = jnp.maximum(g + m, jnp.max(g - b_row + i_row, axis=-1, keepdims=True))
        kw = (k.astype(F32) * jnp.exp(g - b_col + i_col - m_new)).astype(BF16)
        c_scr[...] = jnp.exp(g + m - m_new) * cmat + _dot_tn(kw, v_aug)
        return m_new

    def body(j, carry):
        m_f, m_b = carry
        m_f = one_dir(j, m_f, True, cf_scr, hf_scr)
        m_b = one_dir(nc - 1 - j, m_b, False, cb_scr, hb_scr)
        return m_f, m_b

    zero = jnp.zeros((1, 1), F32)
    lax.fori_loop(0, nc, body, (zero, zero))
    h = _rms(hf_scr[...] + hb_scr[...], nw_ref[...])
    o_ref[0] = (h * _sigmoid(og_ref[0].astype(F32))).astype(BF16)


def _mlstm(proj, gcols, grows, norm_w):
    b, s, _ = proj.shape
    l = M_CHUNK
    nc = s // l
    blk = lambda off: pl.BlockSpec((1, s, HEAD_DIM), lambda i, h: (i, 0, off + h))
    return pl.pallas_call(
        functools.partial(_mlstm_kernel, l=l, nc=nc),
        grid=(b, HEADS),
        in_specs=[blk(0), blk(HEADS), blk(2 * HEADS), blk(3 * HEADS),
                  pl.BlockSpec((1, 1, s, 8), lambda i, h: (i, h, 0, 0)),
                  pl.BlockSpec((1, 1, nc, 8, l), lambda i, h: (i, h, 0, 0, 0)),
                  pl.BlockSpec((1, HEAD_DIM), lambda i, h: (0, 0))],
        out_specs=pl.BlockSpec((1, s, HEAD_DIM), lambda i, h: (i, 0, h)),
        out_shape=jax.ShapeDtypeStruct((b, s, HEADS * HEAD_DIM), BF16),
        scratch_shapes=[pltpu.VMEM((s, HEAD_DIM), F32), pltpu.VMEM((s, HEAD_DIM), F32),
                        pltpu.VMEM((HEAD_DIM, 2 * HEAD_DIM), F32),
                        pltpu.VMEM((HEAD_DIM, 2 * HEAD_DIM), F32)],
        compiler_params=_params("parallel", "parallel"),
        name="mlstm",
    )(proj, proj, proj, proj, gcols, grows, norm_w.reshape(1, HEAD_DIM))


def _unit_tri_inverse(a, eye):
    l = a.shape[0]
    p = eye - a
    apow = a
    steps = l.bit_length() - 2
    for _ in range(steps):
        ab = apow.astype(BF16)
        apow = _dot(ab, ab)
        p = p + _dot(p.astype(BF16), apow.astype(BF16))
    return p


def _gdn_kernel(q_ref, k_ref, v_ref, z_ref, cw_ref, gcol_ref, grow_ref, nw_ref, o_ref,
                pad_scr, q_scr, k_scr, v_scr, of_scr, ob_scr, sf_scr, sb_scr, *, l, nc):
    s = q_ref.shape[1]
    scale = HEAD_DIM ** -0.5
    half = CONV_K // 2

    pad_scr[0:8, :] = jnp.zeros((8, HEAD_DIM), F32)
    pad_scr[8 + s:16 + s, :] = jnp.zeros((8, HEAD_DIM), F32)

    def conv_silu(x_ref, which):
        pad_scr[8:8 + s, :] = x_ref[0].astype(F32)
        acc = jnp.zeros((s, HEAD_DIM), F32)
        for j in range(CONV_K):
            acc = acc + pad_scr[8 - half + j:8 - half + j + s, :] * cw_ref[which, 0, j:j + 1, :]
        return acc * _sigmoid(acc)

    def l2n(x):
        return x * lax.rsqrt(jnp.sum(x * x, axis=-1, keepdims=True) + EPS)

    q_scr[...] = l2n(conv_silu(q_ref, 0))
    k_scr[...] = l2n(conv_silu(k_ref, 1))
    v_scr[...] = conv_silu(v_ref, 2)

    sf_scr[...] = jnp.zeros_like(sf_scr)
    sb_scr[...] = jnp.zeros_like(sb_scr)
    t_idx = lax.broadcasted_iota(jnp.int32, (l, l), 0)
    s_idx = lax.broadcasted_iota(jnp.int32, (l, l), 1)
    eye = (t_idx == s_idx).astype(F32)

    def one_dir(c, fwd, s_scr, o_scr):
        r0 = pl.multiple_of(c * l, l)
        q = q_scr[pl.ds(r0, l), :]
        k = k_scr[pl.ds(r0, l), :]
        v = v_scr[pl.ds(r0, l), :]
        cols = gcol_ref[0, 0, pl.ds(r0, l), :]
        rows = grow_ref[0, 0, c]
        xi, ci = (G_X_F, G_C_F) if fwd else (G_X_B, G_C_B)
        beta, gam_col = cols[:, xi:xi + 1], cols[:, ci:ci + 1]
        gam_row = rows[ci:ci + 1, :]
        incl = (t_idx >= s_idx) if fwd else (t_idx <= s_idx)
        e_incl = jnp.exp(jnp.where(incl, gam_col - gam_row, -jnp.inf))
        e_strict = jnp.where(t_idx == s_idx, 0.0, e_incl)
        kb = k.astype(BF16)
        qk_kk = _dot_nt(jnp.concatenate([q.astype(BF16), kb], axis=0), kb)
        attn = qk_kk[:l] * scale * e_incl
        a_mat = beta * qk_kk[l:] * e_strict
        t_inv = _unit_tri_inverse(a_mat, eye)
        eg = jnp.exp(gam_col)
        rhs = jnp.concatenate([beta * v, (beta * eg) * k], axis=-1).astype(BF16)
        uw = _dot(t_inv.astype(BF16), rhs)
        u, w = uw[:, :HEAD_DIM], uw[:, HEAD_DIM:]
        g_last = gam_row[:, l - 1:l] if fwd else gam_row[:, 0:1]
        state = s_scr[...]
        sb = state.astype(BF16)
        v_new = u - _dot(w.astype(BF16), sb)
        vnb = v_new.astype(BF16)
        o_scr[pl.ds(r0, l), :] = (_dot((q * (eg * scale)).astype(BF16), sb)
                                  + _dot(attn.astype(BF16), vnb))
        k_dec = (k * jnp.exp(g_last - gam_col)).astype(BF16)
        s_scr[...] = jnp.exp(g_last) * state + _dot_tn(k_dec, vnb)

    def body(j, carry):
        one_dir(j, True, sf_scr, of_scr)
        one_dir(nc - 1 - j, False, sb_scr, ob_scr)
        return carry

    lax.fori_loop(0, nc, body, 0)
    z = z_ref[0].astype(F32)
    o_ref[0] = (_rms(of_scr[...] + ob_scr[...], nw_ref[...]) * (z * _sigmoid(z))).astype(BF16)


def _gdn(proj, conv_w, gcols, grows, norm_w):
    b, s, _ = proj.shape
    l = G_CHUNK
    nc = s // l
    blk = lambda off: pl.BlockSpec((1, s, HEAD_DIM), lambda i, h: (i, 0, off + h))
    seq = pltpu.VMEM((s, HEAD_DIM), F32)
    st = pltpu.VMEM((HEAD_DIM, HEAD_DIM), F32)
    return pl.pallas_call(
        functools.partial(_gdn_kernel, l=l, nc=nc),
        grid=(b, HEADS),
        in_specs=[blk(4 * HEADS), blk(5 * HEADS), blk(6 * HEADS), blk(7 * HEADS),
                  pl.BlockSpec((3, 1, 8, HEAD_DIM), lambda i, h: (0, h, 0, 0)),
                  pl.BlockSpec((1, 1, s, 8), lambda i, h: (i, h, 0, 0)),
                  pl.BlockSpec((1, 1, nc, 8, l), lambda i, h: (i, h, 0, 0, 0)),
                  pl.BlockSpec((1, HEAD_DIM), lambda i, h: (0, 0))],
        out_specs=pl.BlockSpec((1, s, HEAD_DIM), lambda i, h: (i, 0, h)),
        out_shape=jax.ShapeDtypeStruct((b, s, HEADS * HEAD_DIM), BF16),
        scratch_shapes=[pltpu.VMEM((s + 16, HEAD_DIM), F32), seq, seq, seq, seq, seq, st, st],
        compiler_params=_params("parallel", "parallel"),
        name="gdn",
    )(proj, proj, proj, proj, conv_w, gcols, grows, norm_w.reshape(1, HEAD_DIM))


def _merge_kernel(x_ref, mod_ref, hm_ref, hg_ref, mm_ref, mg_ref, wm_ref, wg_ref, wo_ref, o_ref):
    gate = mod_ref[0, 5:6, :]
    y = (_sigmoid(mm_ref[0].astype(F32)) * _dot(hm_ref[0], wm_ref[...])
         + _sigmoid(mg_ref[0].astype(F32)) * _dot(hg_ref[0], wg_ref[...]))
    o_ref[0] = x_ref[0] + gate * _dot(y.astype(BF16), wo_ref[...])


def _merge(x, mod, h_m, h_g, proj, w_m, w_g, w_o):
    b, s, d = x.shape
    tm = min(512, s)
    width = HEADS * HEAD_DIM
    tok = lambda w, off: pl.BlockSpec((1, tm, w), lambda i, j: (i, j, off))
    wspec = lambda r: pl.BlockSpec((r, d), lambda i, j: (0, 0))
    merge_off = 8 * width // d
    return pl.pallas_call(
        _merge_kernel,
        grid=(b, s // tm),
        in_specs=[tok(d, 0),
                  pl.BlockSpec((1, 9, d), lambda i, j: (i, 0, 0)),
                  tok(width, 0), tok(width, 0),
                  tok(d, merge_off), tok(d, merge_off + 1),
                  wspec(width), wspec(width), wspec(d)],
        out_specs=tok(d, 0),
        out_shape=jax.ShapeDtypeStruct((b, s, d), F32),
        compiler_params=_params("parallel", "parallel"),
        name="merge",
    )(x, mod, h_m, h_g, proj, proj, w_m, w_g, w_o)


def kernel(x, c, w_ada, b_ada, norm_ffn1, w_ffn1_in, w_ffn1_out, norm_mix, w_in, mlstm_gate_bias,
           gdn_a_log, gdn_dt_bias, gdn_conv_w, mlstm_out_norm, gdn_out_norm, w_branch_mlstm,
           w_branch_gdn, w_out, norm_ffn2, w_ffn2_in, w_ffn2_out, norm_final):
    b, s, d = x.shape
    depth = w_ada.shape[0]
    width = HEADS * HEAD_DIM
    ng = 4 * HEADS
    sizes = (width,) * 4 + (ng,) + (width,) * 4 + (ng,) + (d, d)
    offs = [0]
    for sz in sizes:
        offs.append(offs[-1] + sz)
    for layer in range(depth):
        mod = _ada(c, w_ada[layer], b_ada[layer]).reshape(b, 9, d)
        x = _ffn(x, mod, norm_ffn1[layer], w_ffn1_in[layer], w_ffn1_out[layer], norm_final,
                 sub=0, final=False)

        wl = w_in[layer]
        col = lambda i: wl[:, offs[i]:offs[i + 1]]
        w_big = jnp.concatenate([col(i) for i in (0, 1, 2, 3, 5, 6, 7, 8, 10, 11)], axis=1).astype(BF16)
        w_gates = jnp.concatenate([col(4), col(9), jnp.zeros((d, LANES - 2 * ng), F32)],
                                  axis=1).astype(BF16)
        proj, gates_raw = _proj(x, mod, norm_mix[layer], w_big, w_gates)

        zeros_h = jnp.zeros((HEADS,), F32)
        bias_row = jnp.concatenate([mlstm_gate_bias[layer].reshape(ng),
                                    gdn_dt_bias[layer][0], zeros_h, gdn_dt_bias[layer][1], zeros_h,
                                    jnp.zeros((LANES - 2 * ng,), F32)])
        alog_row = jnp.concatenate([jnp.zeros((ng,), F32),
                                    gdn_a_log[layer][0], zeros_h, gdn_a_log[layer][1], zeros_h,
                                    jnp.zeros((LANES - 2 * ng,), F32)])
        gparams = jnp.zeros((8, LANES), F32).at[0].set(bias_row).at[1].set(alog_row)
        gates = _gate_prep(gates_raw, gparams)
        m_cols, m_rows = _per_head_gates(gates[:, :, :ng].reshape(b, s, 4, HEADS), M_CHUNK)
        g_cols, g_rows = _per_head_gates(gates[:, :, ng:2 * ng].reshape(b, s, 4, HEADS), G_CHUNK)

        h_m = _mlstm(proj, m_cols, m_rows, mlstm_out_norm[layer])
        conv_w = gdn_conv_w[layer].reshape(CONV_K, 3, HEADS, HEAD_DIM).transpose(1, 2, 0, 3)
        conv_w = jnp.pad(conv_w, ((0, 0), (0, 0), (0, 8 - CONV_K), (0, 0)))
        h_g = _gdn(proj, conv_w, g_cols, g_rows, gdn_out_norm[layer])

        x = _merge(x, mod, h_m, h_g, proj, w_branch_mlstm[layer].astype(BF16),
                   w_branch_gdn[layer].astype(BF16), w_out[layer].astype(BF16))
        last = layer == depth - 1
        x = _ffn(x, mod, norm_ffn2[layer], w_ffn2_in[layer], w_ffn2_out[layer], norm_final,
                 sub=2, final=last)
    return x
```

```python
import functools

import jax
import jax.numpy as jnp
from jax import lax
from jax.experimental import pallas as pl
from jax.experimental.pallas import tpu as pltpu

F32 = jnp.float32
BF16 = jnp.bfloat16

EPS = 1e-6
FFN_RES = 0.5
HEADS = 8
HEAD_DIM = 128
CONV_K = 5
LANES = 128
M_CHUNK = 128
M_HEADS_PER_STEP = 2
M_CHUNKS_PER_ITER = 2
G_CHUNK = 64
G_HEADS_PER_STEP = 2
G_CHUNKS_PER_ITER = 4
VMEM_LIMIT = 56 * 1024 * 1024

M_X_F, M_C_F, M_X_B, M_C_B = 0, 1, 2, 3
G_C_F, G_X_F, G_C_B, G_X_B = 0, 1, 2, 3


def _dot(a, b):
    return jnp.dot(a, b, preferred_element_type=F32)


def _dot_nt(a, b):
    return lax.dot_general(a, b, (((1,), (1,)), ((), ())), preferred_element_type=F32)


def _dot_tn(a, b):
    return lax.dot_general(a, b, (((0,), (0,)), ((), ())), preferred_element_type=F32)


def _rms(x, w):
    return (x * lax.rsqrt(jnp.mean(x * x, axis=-1, keepdims=True) + EPS)) * w


def _sigmoid(x):
    return 1.0 / (1.0 + jnp.exp(-x))


def _params(*sem):
    return pltpu.CompilerParams(dimension_semantics=sem, vmem_limit_bytes=VMEM_LIMIT)


def _ada_kernel(c_ref, w_ref, b_ref, o_ref):
    c = c_ref[...]
    cs = (c * _sigmoid(c)).astype(BF16)
    o_ref[...] = _dot(cs, w_ref[...].astype(BF16)) + b_ref[...]


def _ada(c, w_ada, b_ada):
    b, d = c.shape
    n = w_ada.shape[1]
    tn = 1024
    return pl.pallas_call(
        _ada_kernel,
        grid=(n // tn,),
        in_specs=[pl.BlockSpec((b, d), lambda j: (0, 0)),
                  pl.BlockSpec((d, tn), lambda j: (0, j)),
                  pl.BlockSpec((1, tn), lambda j: (0, j))],
        out_specs=pl.BlockSpec((b, tn), lambda j: (0, j)),
        out_shape=jax.ShapeDtypeStruct((b, n), F32),
        compiler_params=_params("parallel"),
        name="ada",
    )(c, w_ada, b_ada.reshape(1, n))


def _ffn_kernel(x_ref, mod_ref, nw_ref, wg_ref, wu_ref, wo_ref, nf_ref, o_ref, *, sub, final):
    x = x_ref[0]
    shift = mod_ref[0, 3 * sub + 0:3 * sub + 1, :]
    scale = mod_ref[0, 3 * sub + 1:3 * sub + 2, :]
    gate = mod_ref[0, 3 * sub + 2:3 * sub + 3, :]
    h = (_rms(x, nw_ref[...]) * (1.0 + scale) + shift).astype(BF16)
    nf = wg_ref.shape[0]

    def body(f, acc):
        g = _dot(h, wg_ref[f])
        u = _dot(h, wu_ref[f])
        a = (g * _sigmoid(g) * u).astype(BF16)
        return acc + _dot(a, wo_ref[f])

    acc = lax.fori_loop(0, nf, body, jnp.zeros(x.shape, F32))
    y = x + FFN_RES * gate * acc
    if final:
        y = _rms(y, nf_ref[...])
    o_ref[0] = y


def _ffn(x, mod, norm_w, w_in, w_out, norm_final, *, sub, final):
    b, s, d = x.shape
    f = w_out.shape[0]
    fc = 256
    nfc = f // fc
    tm = min(512, s)
    wg = w_in[:, :f].astype(BF16).reshape(d, nfc, fc).transpose(1, 0, 2)
    wu = w_in[:, f:].astype(BF16).reshape(d, nfc, fc).transpose(1, 0, 2)
    wo = w_out.astype(BF16).reshape(nfc, fc, d)
    const3 = lambda i, j: (0, 0, 0)
    const2 = lambda i, j: (0, 0)
    single = pl.Buffered(1)
    return pl.pallas_call(
        functools.partial(_ffn_kernel, sub=sub, final=final),
        grid=(b, s // tm),
        in_specs=[pl.BlockSpec((1, tm, d), lambda i, j: (i, j, 0)),
                  pl.BlockSpec((1, 9, d), lambda i, j: (i, 0, 0)),
                  pl.BlockSpec((1, d), const2),
                  pl.BlockSpec((nfc, d, fc), const3, pipeline_mode=single),
                  pl.BlockSpec((nfc, d, fc), const3, pipeline_mode=single),
                  pl.BlockSpec((nfc, fc, d), const3, pipeline_mode=single),
                  pl.BlockSpec((1, d), const2)],
        out_specs=pl.BlockSpec((1, tm, d), lambda i, j: (i, j, 0)),
        out_shape=jax.ShapeDtypeStruct((b, s, d), F32),
        compiler_params=_params("parallel", "parallel"),
        name="ffn_final" if final else "ffn",
    )(x, mod, norm_w.reshape(1, d), wg, wu, wo, norm_final.reshape(1, d))


def _proj_kernel(x_ref, mod_ref, nw_ref, w_ref, wgate_ref, o_ref, gate_ref, h_scr):
    @pl.when(pl.program_id(2) == 0)
    def _():
        shift = mod_ref[0, 3:4, :]
        scale = mod_ref[0, 4:5, :]
        h = (_rms(x_ref[0], nw_ref[...]) * (1.0 + scale) + shift).astype(BF16)
        h_scr[...] = h
        gate_ref[0] = _dot(h, wgate_ref[...])

    o_ref[0] = _dot(h_scr[...], w_ref[...]).astype(BF16)


def _proj(x, mod, norm_w, w_big, w_gates):
    b, s, d = x.shape
    n = w_big.shape[1]
    tm, tn = min(1024, s), 2048
    return pl.pallas_call(
        _proj_kernel,
        grid=(b, s // tm, n // tn),
        in_specs=[pl.BlockSpec((1, tm, d), lambda i, j, k: (i, j, 0)),
                  pl.BlockSpec((1, 9, d), lambda i, j, k: (i, 0, 0)),
                  pl.BlockSpec((1, d), lambda i, j, k: (0, 0)),
                  pl.BlockSpec((d, tn), lambda i, j, k: (0, k)),
                  pl.BlockSpec((d, LANES), lambda i, j, k: (0, 0))],
        out_specs=[pl.BlockSpec((1, tm, tn), lambda i, j, k: (i, j, k)),
                   pl.BlockSpec((1, tm, LANES), lambda i, j, k: (i, j, 0))],
        out_shape=[jax.ShapeDtypeStruct((b, s, n), BF16),
                   jax.ShapeDtypeStruct((b, s, LANES), F32)],
        scratch_shapes=[pltpu.VMEM((tm, d), BF16)],
        compiler_params=_params("parallel", "parallel", "arbitrary"),
        name="proj",
    )(x, mod, norm_w.reshape(1, d), w_big, w_gates)


def _split_dot(tri, y):
    hi = y.astype(BF16)
    r1 = y - hi.astype(F32)
    mid = r1.astype(BF16)
    lo = (r1 - mid.astype(F32)).astype(BF16)
    return _dot(tri, hi) + _dot(tri, mid) + _dot(tri, lo)


def _gate_prep_kernel(g_ref, p_ref, o_ref, y_scr, *, lm, lg):
    s = g_ref.shape[1]
    lane = lax.broadcasted_iota(jnp.int32, (1, LANES), 1)
    group = lane // HEADS
    pre = g_ref[0] + p_ref[0:1, :]
    a_coef = -jnp.exp(p_ref[1:2, :])
    tail = jnp.log(1.0 + jnp.exp(-jnp.abs(pre)))
    sp = jnp.maximum(pre, 0.0) + tail
    log_sig = jnp.minimum(pre, 0.0) - tail
    is_logf = (group == 1) | (group == 3)
    is_a = (group == 4) | (group == 6)
    is_beta = (group == 5) | (group == 7)
    y = jnp.where(is_logf, log_sig, pre)
    y = jnp.where(is_a, a_coef * sp, y)
    y = jnp.where(is_beta, _sigmoid(pre), y)
    y_scr[...] = y
    is_prefix = (group == 1) | (group == 4)
    is_suffix = (group == 3) | (group == 6)

    def cumsum_pass(l, lanes_sel):
        t = lax.broadcasted_iota(jnp.int32, (l, l), 0)
        u = lax.broadcasted_iota(jnp.int32, (l, l), 1)
        tri = (t >= u).astype(BF16)

        def body(c, carry):
            r0 = pl.multiple_of(c * l, l)
            yc = y_scr[pl.ds(r0, l), :]
            pre_c = _split_dot(tri, yc)
            tot = pre_c[l - 1:l, :]
            suf_c = tot - pre_c + yc
            out = jnp.where(is_prefix, pre_c, jnp.where(is_suffix, suf_c, yc))
            cur = o_ref[0, pl.ds(r0, l), :]
            o_ref[0, pl.ds(r0, l), :] = jnp.where(lanes_sel, out, cur)
            return carry

        lax.fori_loop(0, s // l, body, 0)

    o_ref[0] = y
    cumsum_pass(lm, group < 4)
    cumsum_pass(lg, group >= 4)


def _gate_prep(gates_raw, params):
    b, s, _ = gates_raw.shape
    return pl.pallas_call(
        functools.partial(_gate_prep_kernel, lm=M_CHUNK, lg=G_CHUNK),
        grid=(b,),
        in_specs=[pl.BlockSpec((1, s, LANES), lambda i: (i, 0, 0)),
                  pl.BlockSpec((8, LANES), lambda i: (0, 0))],
        out_specs=pl.BlockSpec((1, s, LANES), lambda i: (i, 0, 0)),
        out_shape=jax.ShapeDtypeStruct((b, s, LANES), F32),
        scratch_shapes=[pltpu.VMEM((s, LANES), F32)],
        compiler_params=_params("parallel"),
        name="gate_prep",
    )(gates_raw, params)


def _per_head_gates(g, l):
    b, s = g.shape[:2]
    cols = jnp.pad(g.transpose(0, 3, 1, 2), ((0, 0), (0, 0), (0, 0), (0, 4)))
    rows = g.transpose(0, 3, 2, 1).reshape(b, HEADS, 4, s // l, l).transpose(0, 1, 3, 2, 4)
    rows = jnp.pad(rows, ((0, 0), (0, 0), (0, 0), (0, 4), (0, 0)))
    return cols, rows


def _mlstm_kernel(q_ref, k_ref, v_ref, og_ref, gcol_ref, grow_ref, nw_ref, o_ref,
                  h_scr, c_scr, m_scr, *, l, nc, hb, cu):
    d = HEAD_DIM
    scale = d ** -0.5
    t_idx = lax.broadcasted_iota(jnp.int32, (l, l), 0)
    s_idx = lax.broadcasted_iota(jnp.int32, (l, l), 1)
    ones_col = (lax.broadcasted_iota(jnp.int32, (l, LANES), 1) == 0).astype(BF16)

    def gate_idx(dr):
        return (M_X_F, M_C_F) if dr == 0 else (M_X_B, M_C_B)

    def chunk_of(dr, step):
        return step if dr == 0 else nc - 1 - step

    def total_logf(b_row, dr):
        return b_row[:, l - 1:l] if dr == 0 else b_row[:, 0:1]

    for hd in range(hb):
        for dr in range(2):
            xi, ci = gate_idx(dr)
            m = jnp.zeros((1, 1), F32)
            for step in range(nc):
                c = chunk_of(dr, step)
                rows = grow_ref[0, hd, c]
                i_row, b_row = rows[xi:xi + 1, :], rows[ci:ci + 1, :]
                g = total_logf(b_row, dr)
                m_scr[2 * hd + dr, c] = jnp.broadcast_to(m, (8, LANES))
                m = jnp.maximum(g + m, jnp.max(g - b_row + i_row, axis=-1, keepdims=True))

    c_scr[...] = jnp.zeros_like(c_scr)

    def body(it, carry):
        probs = []
        for k_step in range(cu):
            for hd in range(hb):
                for dr in range(2):
                    probs.append(dict(hd=hd, dr=dr, c=chunk_of(dr, it * cu + k_step)))
        for pr in probs:
            hd, c = pr["hd"], pr["c"]
            r0 = pl.multiple_of(c * l, l)
            pr["q"] = q_ref[0, pl.ds(r0, l), hd * d:(hd + 1) * d]
            pr["k"] = k_ref[0, pl.ds(r0, l), hd * d:(hd + 1) * d]
            pr["qk"] = _dot_nt(pr["q"], pr["k"])
        for pr in probs:
            hd, dr, c = pr["hd"], pr["dr"], pr["c"]
            r0 = pl.multiple_of(c * l, l)
            xi, ci = gate_idx(dr)
            cols = gcol_ref[0, hd, pl.ds(r0, l), :]
            rows = grow_ref[0, hd, c]
            i_col, b_col = cols[:, xi:xi + 1], cols[:, ci:ci + 1]
            i_row, b_row = rows[xi:xi + 1, :], rows[ci:ci + 1, :]
            m_prev = m_scr[2 * hd + dr, c][0:1, 0:1]
            mask = (t_idx >= s_idx) if dr == 0 else (t_idx <= s_idx)
            dmat = jnp.where(mask, b_col - b_row + i_row, -jnp.inf)
            inter = b_col + m_prev
            m_t = jnp.maximum(inter, jnp.max(dmat, axis=-1, keepdims=True))
            sc = (pr["qk"] * scale * jnp.exp(dmat - m_t)).astype(BF16)
            v_aug = jnp.concatenate([v_ref[0, pl.ds(r0, l), hd * d:(hd + 1) * d], ones_col], axis=-1)
            pr["num"] = _dot(sc, v_aug)
            pr["w_inter"] = jnp.exp(inter - m_t) * scale
            pr["floor"] = jnp.exp(-m_t)
            g = total_logf(b_row, dr)
            m_new = jnp.maximum(g + m_prev, jnp.max(g - b_row + i_row, axis=-1, keepdims=True))
            pr["decay"] = jnp.exp(g + m_prev - m_new)
            kw = (pr["k"].astype(F32) * jnp.exp(g - b_col + i_col - m_new)).astype(BF16)
            pr["x"] = _dot_tn(kw, v_aug)
        cmats = {}
        for pr in probs:
            key = 2 * pr["hd"] + pr["dr"]
            cmat = cmats[key] if key in cmats else c_scr[key]
            pr["qc"] = _dot(pr["q"], cmat.astype(BF16))
            cmats[key] = pr["decay"] * cmat + pr["x"]
        for key, cmat in cmats.items():
            c_scr[key] = cmat
        for pr in probs:
            hd, dr, c = pr["hd"], pr["dr"], pr["c"]
            num_aug = pr["num"] + pr["w_inter"] * pr["qc"]
            den = num_aug[:, d:d + 1]
            h_scr[2 * hd + dr, pl.ds(pl.multiple_of(c * l, l), l), :] = (
                num_aug[:, :d] / jnp.maximum(jnp.abs(den), pr["floor"]))
        return carry

    lax.fori_loop(0, nc // cu, body, 0)
    for hd in range(hb):
        h = _rms(h_scr[2 * hd] + h_scr[2 * hd + 1], nw_ref[...])
        og = og_ref[0, :, hd * d:(hd + 1) * d].astype(F32)
        o_ref[0, :, hd * d:(hd + 1) * d] = (h * _sigmoid(og)).astype(BF16)


def _mlstm(proj, gcols, grows, norm_w):
    b, s, _ = proj.shape
    l = M_CHUNK
    nc = s // l
    hb = M_HEADS_PER_STEP
    cu = min(M_CHUNKS_PER_ITER, nc)
    d = HEAD_DIM
    blk = lambda off: pl.BlockSpec((1, s, hb * d), lambda i, h: (i, 0, off * HEADS // hb + h))
    return pl.pallas_call(
        functools.partial(_mlstm_kernel, l=l, nc=nc, hb=hb, cu=cu),
        grid=(b, HEADS // hb),
        in_specs=[blk(0), blk(1), blk(2), blk(3),
                  pl.BlockSpec((1, hb, s, 8), lambda i, h: (i, h, 0, 0)),
                  pl.BlockSpec((1, hb, nc, 8, l), lambda i, h: (i, h, 0, 0, 0)),
                  pl.BlockSpec((1, d), lambda i, h: (0, 0))],
        out_specs=pl.BlockSpec((1, s, hb * d), lambda i, h: (i, 0, h)),
        out_shape=jax.ShapeDtypeStruct((b, s, HEADS * d), BF16),
        scratch_shapes=[pltpu.VMEM((2 * hb, s, d), F32),
                        pltpu.VMEM((2 * hb, d, 2 * d), F32),
                        pltpu.VMEM((2 * hb, nc, 8, LANES), F32)],
        compiler_params=_params("parallel", "parallel"),
        name="mlstm",
    )(proj, proj, proj, proj, gcols, grows, norm_w.reshape(1, d))


def _pair_rows(g, l):
    b, s = g.shape[:2]
    rows = g.transpose(0, 3, 2, 1).reshape(b, HEADS, 2, 2, s // l, l)
    rows = rows.transpose(0, 1, 4, 3, 2, 5).reshape(b, HEADS, s // l, 2, 2 * l)
    return jnp.pad(rows, ((0, 0), (0, 0), (0, 0), (0, 6), (0, 0)))


def _gdn_kernel(q_ref, k_ref, v_ref, z_ref, cw_ref, gcol_ref, grow_ref, nw_ref, o_ref,
                pad_scr, q_scr, k_scr, v_scr, qm_scr, n_scr, oa_scr, od_scr, st_scr,
                *, l, nc, hb, cu):
    s = q_ref.shape[1]
    d = HEAD_DIM
    scale = d ** -0.5
    half = CONV_K // 2

    pad_scr[0:8, :] = jnp.zeros((8, d), F32)
    pad_scr[8 + s:16 + s, :] = jnp.zeros((8, d), F32)

    rb = min(s, 256)

    def conv_silu(x_ref, dst_scr, which, hd, norm):
        pad_scr[8:8 + s, :] = x_ref[0, :, hd * d:(hd + 1) * d].astype(F32)
        for r0 in range(0, s, rb):
            acc = jnp.zeros((rb, d), F32)
            for j in range(CONV_K):
                lo = 8 - half + j + r0
                acc = acc + pad_scr[lo:lo + rb, :] * cw_ref[which, hd, j:j + 1, :]
            y = acc * _sigmoid(acc)
            if norm:
                y = y * lax.rsqrt(jnp.sum(y * y, axis=-1, keepdims=True) + EPS)
            dst_scr[hd, r0:r0 + rb, :] = y.astype(dst_scr.dtype)

    for hd in range(hb):
        conv_silu(q_ref, q_scr, 0, hd, True)
        conv_silu(k_ref, k_scr, 1, hd, True)
        conv_silu(v_ref, v_scr, 2, hd, False)

    t_idx = lax.broadcasted_iota(jnp.int32, (l, 2 * l), 0)
    lane = lax.broadcasted_iota(jnp.int32, (l, 2 * l), 1)
    is_f = lane < l
    s_idx = jnp.where(is_f, lane, lane - l)
    incl = jnp.where(is_f, t_idx - s_idx, s_idx - t_idx) >= 0
    diag = t_idx == s_idx
    eye2 = diag.astype(F32)
    left4 = lax.broadcasted_iota(jnp.int32, (l, 4 * d), 1) < 2 * d
    zeros2 = jnp.zeros((l, 2 * d), F32)

    def blockdiag(x2):
        return jnp.concatenate([jnp.where(is_f, x2, 0.0), jnp.where(is_f, 0.0, x2)], axis=0).astype(BF16)

    base = 8
    shift = base.bit_length() - 1
    same_base = (t_idx >> shift) == (s_idx >> shift)
    merge_masks = []
    size = base
    while size < l:
        sh = size.bit_length() - 1
        merge_masks.append(((t_idx >> (sh + 1)) == (s_idx >> (sh + 1))) & ((t_idx >> sh) != (s_idx >> sh)))
        size *= 2

    def body_a(it, carry):
        probs = [dict(c=it * cu + cc, hd=hd) for cc in range(cu) for hd in range(hb)]
        for pr in probs:
            c, hd = pr["c"], pr["hd"]
            r0 = pl.multiple_of(c * l, l)
            qb = q_scr[hd, pl.ds(r0, l), :]
            kb = k_scr[hd, pl.ds(r0, l), :]
            pr["kq"] = _dot_nt(jnp.concatenate([qb, kb], axis=0),
                               jnp.concatenate([kb, kb], axis=0))
        for pr in probs:
            c, hd = pr["c"], pr["hd"]
            cols = gcol_ref[0, hd, pl.ds(pl.multiple_of(c * l, l), l), :]
            gam_f, beta_f = cols[:, G_C_F:G_C_F + 1], cols[:, G_X_F:G_X_F + 1]
            gam_b, beta_b = cols[:, G_C_B:G_C_B + 1], cols[:, G_X_B:G_X_B + 1]
            gam_row = grow_ref[0, hd, c][0:1, :]
            e_incl = jnp.exp(jnp.where(incl, jnp.where(is_f, gam_f, gam_b) - gam_row, -jnp.inf))
            pr["attn2"] = pr["kq"][:l] * scale * e_incl
            a2 = jnp.where(is_f, beta_f, beta_b) * pr["kq"][l:] * jnp.where(diag, 0.0, e_incl)
            pr["a2"] = a2
            dblk = jnp.where(same_base, a2, 0.0)
            pr["p"] = eye2 - dblk
            pr["d2"] = _dot(dblk.astype(BF16), blockdiag(dblk))
        for pr in probs:
            res = _dot(jnp.concatenate([pr["p"], pr["d2"]], axis=0).astype(BF16), blockdiag(pr["d2"]))
            pr["p"] = pr["p"] + res[:l]
            pr["d4"] = res[l:]
        for pr in probs:
            pr["p"] = pr["p"] + _dot(pr["p"].astype(BF16), blockdiag(pr["d4"]))
        for mask in merge_masks:
            for pr in probs:
                pr["x"] = _dot(jnp.where(mask, pr["a2"], 0.0).astype(BF16), blockdiag(pr["p"]))
            for pr in probs:
                pr["p"] = pr["p"] - _dot(pr["p"].astype(BF16), blockdiag(pr["x"]))
        for pr in probs:
            c, hd = pr["c"], pr["hd"]
            r0 = pl.multiple_of(c * l, l)
            k = k_scr[hd, pl.ds(r0, l), :].astype(F32)
            v = v_scr[hd, pl.ds(r0, l), :].astype(F32)
            cols = gcol_ref[0, hd, pl.ds(r0, l), :]
            gam_f, beta_f = cols[:, G_C_F:G_C_F + 1], cols[:, G_X_F:G_X_F + 1]
            gam_b, beta_b = cols[:, G_C_B:G_C_B + 1], cols[:, G_X_B:G_X_B + 1]
            rhs_f = jnp.concatenate([-(beta_f * jnp.exp(gam_f)) * k, beta_f * v], axis=-1)
            rhs_b = jnp.concatenate([-(beta_b * jnp.exp(gam_b)) * k, beta_b * v], axis=-1)
            rhs_bd = jnp.concatenate([jnp.concatenate([rhs_f, zeros2], axis=-1),
                                      jnp.concatenate([zeros2, rhs_b], axis=-1)], axis=0).astype(BF16)
            pr["wu"] = _dot(pr["p"].astype(BF16), rhs_bd)
        for pr in probs:
            c, hd = pr["c"], pr["hd"]
            r0 = pl.multiple_of(c * l, l)
            k = k_scr[hd, pl.ds(r0, l), :].astype(F32)
            cols = gcol_ref[0, hd, pl.ds(r0, l), :]
            gam_f, gam_b = cols[:, G_C_F:G_C_F + 1], cols[:, G_C_B:G_C_B + 1]
            gam_row = grow_ref[0, hd, c][0:1, :]
            g_last_f, g_last_b = gam_row[:, l - 1:l], gam_row[:, l:l + 1]
            wu = pr["wu"]
            wu_bd = jnp.concatenate([jnp.where(left4, wu, 0.0), jnp.where(left4, 0.0, wu)],
                                    axis=0).astype(BF16)
            k_dec = jnp.concatenate([k * jnp.exp(g_last_f - gam_f), k * jnp.exp(g_last_b - gam_b)],
                                    axis=0)
            pr["res"] = _dot(jnp.concatenate([pr["attn2"], k_dec.T], axis=0).astype(BF16), wu_bd)
        for pr in probs:
            c, hd = pr["c"], pr["hd"]
            r0 = pl.multiple_of(c * l, l)
            q = q_scr[hd, pl.ds(r0, l), :].astype(F32)
            cols = gcol_ref[0, hd, pl.ds(r0, l), :]
            gam_f, gam_b = cols[:, G_C_F:G_C_F + 1], cols[:, G_C_B:G_C_B + 1]
            res = pr["res"]
            qd_f = q * (jnp.exp(gam_f) * scale) + res[:l, 0:d]
            qd_b = q * (jnp.exp(gam_b) * scale) + res[:l, 2 * d:3 * d]
            qm_scr[2 * hd, c] = jnp.concatenate([qd_f, res[l:, 0:d]], axis=0).astype(BF16)
            qm_scr[2 * hd + 1, c] = jnp.concatenate([qd_b, res[l:, 2 * d:3 * d]], axis=0).astype(BF16)
            n_scr[2 * hd, c] = res[l:, d:2 * d].astype(BF16)
            n_scr[2 * hd + 1, c] = res[l:, 3 * d:].astype(BF16)
            oa_scr[hd, pl.ds(r0, l), :] = res[:l, d:2 * d] + res[:l, 3 * d:]
        return carry

    lax.fori_loop(0, nc // cu, body_a, 0)

    st_scr[...] = jnp.zeros_like(st_scr)

    def body_b(j, carry):
        chains = [(hd, dr, j if dr == 0 else nc - 1 - j) for hd in range(hb) for dr in range(2)]
        states = [st_scr[2 * hd + dr] for hd, dr, _ in chains]
        rs = [_dot(qm_scr[2 * hd + dr, c], st.astype(BF16)) for (hd, dr, c), st in zip(chains, states)]
        for (hd, dr, c), state, r in zip(chains, states, rs):
            idx = 2 * hd + dr
            gam_row = grow_ref[0, hd, c][0:1, :]
            g_last = gam_row[:, l - 1:l] if dr == 0 else gam_row[:, l:l + 1]
            st_scr[idx] = jnp.exp(g_last) * state + r[l:] + n_scr[idx, c]
            od_scr[idx, pl.ds(pl.multiple_of(c * l, l), l), :] = r[:l].astype(BF16)
        return carry

    lax.fori_loop(0, nc, body_b, 0)

    for hd in range(hb):
        z = z_ref[0, :, hd * d:(hd + 1) * d].astype(F32)
        o = oa_scr[hd] + od_scr[2 * hd] + od_scr[2 * hd + 1]
        o_ref[0, :, hd * d:(hd + 1) * d] = (_rms(o, nw_ref[...]) * (z * _sigmoid(z))).astype(BF16)


def _gdn(proj, conv_w, gcols, grows, norm_w):
    b, s, _ = proj.shape
    l = G_CHUNK
    assert 2 * l == LANES
    nc = s // l
    hb = G_HEADS_PER_STEP
    cu = min(G_CHUNKS_PER_ITER, nc)
    d = HEAD_DIM
    blk = lambda off: pl.BlockSpec((1, s, hb * d), lambda i, h: (i, 0, off * HEADS // hb + h))
    return pl.pallas_call(
        functools.partial(_gdn_kernel, l=l, nc=nc, hb=hb, cu=cu),
        grid=(b, HEADS // hb),
        in_specs=[blk(4), blk(5), blk(6), blk(7),
                  pl.BlockSpec((3, hb, 8, d), lambda i, h: (0, h, 0, 0)),
                  pl.BlockSpec((1, hb, s, 8), lambda i, h: (i, h, 0, 0)),
                  pl.BlockSpec((1, hb, nc, 8, 2 * l), lambda i, h: (i, h, 0, 0, 0)),
                  pl.BlockSpec((1, d), lambda i, h: (0, 0))],
        out_specs=pl.BlockSpec((1, s, hb * d), lambda i, h: (i, 0, h)),
        out_shape=jax.ShapeDtypeStruct((b, s, HEADS * d), BF16),
        scratch_shapes=[pltpu.VMEM((s + 16, d), F32),
                        pltpu.VMEM((hb, s, d), BF16), pltpu.VMEM((hb, s, d), BF16),
                        pltpu.VMEM((hb, s, d), BF16),
                        pltpu.VMEM((2 * hb, nc, l + d, d), BF16),
                        pltpu.VMEM((2 * hb, nc, d, d), BF16),
                        pltpu.VMEM((hb, s, d), F32),
                        pltpu.VMEM((2 * hb, s, d), BF16),
                        pltpu.VMEM((2 * hb, d, d), F32)],
        compiler_params=_params("parallel", "parallel"),
        name="gdn",
    )(proj, proj, proj, proj, conv_w, gcols, grows, norm_w.reshape(1, d))


def _merge_kernel(x_ref, mod_ref, hm_ref, hg_ref, mm_ref, mg_ref, wm_ref, wg_ref, wo_ref, o_ref):
    gate = mod_ref[0, 5:6, :]
    y = (_sigmoid(mm_ref[0].astype(F32)) * _dot(hm_ref[0], wm_ref[...])
         + _sigmoid(mg_ref[0].astype(F32)) * _dot(hg_ref[0], wg_ref[...]))
    o_ref[0] = x_ref[0] + gate * _dot(y.astype(BF16), wo_ref[...])


def _merge(x, mod, h_m, h_g, proj, w_m, w_g, w_o):
    b, s, d = x.shape
    tm = min(512, s)
    width = HEADS * HEAD_DIM
    tok = lambda w, off: pl.BlockSpec((1, tm, w), lambda i, j: (i, j, off))
    wspec = lambda r: pl.BlockSpec((r, d), lambda i, j: (0, 0))
    merge_off = 8 * width // d
    return pl.pallas_call(
        _merge_kernel,
        grid=(b, s // tm),
        in_specs=[tok(d, 0),
                  pl.BlockSpec((1, 9, d), lambda i, j: (i, 0, 0)),
                  tok(width, 0), tok(width, 0),
                  tok(d, merge_off), tok(d, merge_off + 1),
                  wspec(width), wspec(width), wspec(d)],
        out_specs=tok(d, 0),
        out_shape=jax.ShapeDtypeStruct((b, s, d), F32),
        compiler_params=_params("parallel", "parallel"),
        name="merge",
    )(x, mod, h_m, h_g, proj, proj, w_m, w_g, w_o)


def kernel(x, c, w_ada, b_ada, norm_ffn1, w_ffn1_in, w_ffn1_out, norm_mix, w_in, mlstm_gate_bias,
           gdn_a_log, gdn_dt_bias, gdn_conv_w, mlstm_out_norm, gdn_out_norm, w_branch_mlstm,
           w_branch_gdn, w_out, norm_ffn2, w_ffn2_in, w_ffn2_out, norm_final):
    b, s, d = x.shape
    depth = w_ada.shape[0]
    width = HEADS * HEAD_DIM
    ng = 4 * HEADS
    sizes = (width,) * 4 + (ng,) + (width,) * 4 + (ng,) + (d, d)
    offs = [0]
    for sz in sizes:
        offs.append(offs[-1] + sz)
    for layer in range(depth):
        mod = _ada(c, w_ada[layer], b_ada[layer]).reshape(b, 9, d)
        x = _ffn(x, mod, norm_ffn1[layer], w_ffn1_in[layer], w_ffn1_out[layer], norm_final,
                 sub=0, final=False)

        wl = w_in[layer]
        col = lambda i: wl[:, offs[i]:offs[i + 1]]
        w_big = jnp.concatenate([col(i) for i in (0, 1, 2, 3, 5, 6, 7, 8, 10, 11)], axis=1).astype(BF16)
        w_gates = jnp.concatenate([col(4), col(9), jnp.zeros((d, LANES - 2 * ng), F32)],
                                  axis=1).astype(BF16)
        proj, gates_raw = _proj(x, mod, norm_mix[layer], w_big, w_gates)

        zeros_h = jnp.zeros((HEADS,), F32)
        bias_row = jnp.concatenate([mlstm_gate_bias[layer].reshape(ng),
                                    gdn_dt_bias[layer][0], zeros_h, gdn_dt_bias[layer][1], zeros_h,
                                    jnp.zeros((LANES - 2 * ng,), F32)])
        alog_row = jnp.concatenate([jnp.zeros((ng,), F32),
                                    gdn_a_log[layer][0], zeros_h, gdn_a_log[layer][1], zeros_h,
                                    jnp.zeros((LANES - 2 * ng,), F32)])
        gparams = jnp.zeros((8, LANES), F32).at[0].set(bias_row).at[1].set(alog_row)
        gates = _gate_prep(gates_raw, gparams)
        m_cols, m_rows = _per_head_gates(gates[:, :, :ng].reshape(b, s, 4, HEADS), M_CHUNK)
        g_gates = gates[:, :, ng:2 * ng].reshape(b, s, 4, HEADS)
        g_cols = _per_head_gates(g_gates, G_CHUNK)[0]
        g_rows = _pair_rows(g_gates, G_CHUNK)

        h_m = _mlstm(proj, m_cols, m_rows, mlstm_out_norm[layer])
        conv_w = gdn_conv_w[layer].reshape(CONV_K, 3, HEADS, HEAD_DIM).transpose(1, 2, 0, 3)
        conv_w = jnp.pad(conv_w, ((0, 0), (0, 0), (0, 8 - CONV_K), (0, 0)))
        h_g = _gdn(proj, conv_w, g_cols, g_rows, gdn_out_norm[layer])

        x = _merge(x, mod, h_m, h_g, proj, w_branch_mlstm[layer].astype(BF16),
                   w_branch_gdn[layer].astype(BF16), w_out[layer].astype(BF16))
        last = layer == depth - 1
        x = _ffn(x, mod, norm_ffn2[layer], w_ffn2_in[layer], w_ffn2_out[layer], norm_final,
                 sub=2, final=last)
    return x
```

```python
import functools

import jax
import jax.numpy as jnp
from jax import lax
from jax.experimental import pallas as pl
from jax.experimental.pallas import tpu as pltpu

F32 = jnp.float32
BF16 = jnp.bfloat16

EPS = 1e-6
FFN_RES = 0.5
HEADS = 8
HEAD_DIM = 128
CONV_K = 5
LANES = 128
BF16_SUBLANES = 16
M_CHUNK = 128
M_HEADS_PER_STEP = 2
M_CHUNKS_PER_ITER = 2
G_CHUNK = 64
G_HEADS_PER_STEP = 2
G_CHUNKS_PER_ITER = 4
VMEM_LIMIT = 56 * 1024 * 1024

M_X_F, M_C_F, M_X_B, M_C_B = 0, 1, 2, 3
G_C_F, G_X_F, G_C_B, G_X_B = 0, 1, 2, 3


def _dot(a, b):
    return jnp.dot(a, b, preferred_element_type=F32)


def _dot_nt(a, b):
    return lax.dot_general(a, b, (((1,), (1,)), ((), ())), preferred_element_type=F32)


def _dot_tn(a, b):
    return lax.dot_general(a, b, (((0,), (0,)), ((), ())), preferred_element_type=F32)


def _rms(x, w):
    return (x * lax.rsqrt(jnp.mean(x * x, axis=-1, keepdims=True) + EPS)) * w


def _sigmoid(x):
    return 1.0 / (1.0 + jnp.exp(-x))


def _params(*sem):
    return pltpu.CompilerParams(dimension_semantics=sem, vmem_limit_bytes=VMEM_LIMIT)


def _ada_kernel(c_ref, w_ref, b_ref, o_ref):
    c = c_ref[...]
    cs = (c * _sigmoid(c)).astype(BF16)
    o_ref[...] = _dot(cs, w_ref[...].astype(BF16)) + b_ref[...]


def _ada(c, w_ada, b_ada):
    b, d = c.shape
    n = w_ada.shape[1]
    tn = 1024
    return pl.pallas_call(
        _ada_kernel,
        grid=(n // tn,),
        in_specs=[pl.BlockSpec((b, d), lambda j: (0, 0)),
                  pl.BlockSpec((d, tn), lambda j: (0, j)),
                  pl.BlockSpec((1, tn), lambda j: (0, j))],
        out_specs=pl.BlockSpec((b, tn), lambda j: (0, j)),
        out_shape=jax.ShapeDtypeStruct((b, n), F32),
        compiler_params=_params("parallel"),
        name="ada",
    )(c, w_ada, b_ada.reshape(1, n))


def _ffn_kernel(x_ref, mod_ref, nw_ref, wg_ref, wu_ref, wo_ref, nf_ref, o_ref, a_scr,
                *, sub, final, fc):
    x = x_ref[0]
    shift = mod_ref[0, 3 * sub + 0:3 * sub + 1, :]
    scale = mod_ref[0, 3 * sub + 1:3 * sub + 2, :]
    gate = mod_ref[0, 3 * sub + 2:3 * sub + 3, :]
    h = (_rms(x, nw_ref[...]) * (1.0 + scale) + shift).astype(BF16)
    f = wg_ref.shape[1]
    for c0 in range(0, f, fc):
        c1 = min(c0 + fc, f)
        g = _dot(h, wg_ref[:, c0:c1])
        u = _dot(h, wu_ref[:, c0:c1])
        a_scr[:, c0:c1] = (g * _sigmoid(g) * u).astype(BF16)
    y = x + FFN_RES * gate * _dot(a_scr[...], wo_ref[...])
    if final:
        y = _rms(y, nf_ref[...])
    o_ref[0] = y


def _ffn(x, mod, norm_w, w_in, w_out, norm_final, *, sub, final):
    b, s, d = x.shape
    f = w_out.shape[0]
    tm = min(512, s)
    wg = w_in[:, :f].astype(BF16)
    wu = w_in[:, f:].astype(BF16)
    wo = w_out.astype(BF16)
    const2 = lambda i, j: (0, 0)
    single = pl.Buffered(1)
    return pl.pallas_call(
        functools.partial(_ffn_kernel, sub=sub, final=final, fc=512),
        grid=(b, s // tm),
        in_specs=[pl.BlockSpec((1, tm, d), lambda i, j: (i, j, 0)),
                  pl.BlockSpec((1, 9, d), lambda i, j: (i, 0, 0)),
                  pl.BlockSpec((1, d), const2),
                  pl.BlockSpec((d, f), const2, pipeline_mode=single),
                  pl.BlockSpec((d, f), const2, pipeline_mode=single),
                  pl.BlockSpec((f, d), const2, pipeline_mode=single),
                  pl.BlockSpec((1, d), const2)],
        out_specs=pl.BlockSpec((1, tm, d), lambda i, j: (i, j, 0)),
        out_shape=jax.ShapeDtypeStruct((b, s, d), F32),
        scratch_shapes=[pltpu.VMEM((tm, f), BF16)],
        compiler_params=_params("parallel", "parallel"),
        name="ffn_final" if final else "ffn",
    )(x, mod, norm_w.reshape(1, d), wg, wu, wo, norm_final.reshape(1, d))


def _proj_kernel(x_ref, mod_ref, nw_ref, w_ref, wgate_ref, o_ref, gate_ref, h_scr):
    @pl.when(pl.program_id(2) == 0)
    def _():
        shift = mod_ref[0, 3:4, :]
        scale = mod_ref[0, 4:5, :]
        h = (_rms(x_ref[0], nw_ref[...]) * (1.0 + scale) + shift).astype(BF16)
        h_scr[...] = h
        gate_ref[0] = _dot(h, wgate_ref[...])

    o_ref[0] = _dot(h_scr[...], w_ref[...]).astype(BF16)


def _proj(x, mod, norm_w, w_big, w_gates):
    b, s, d = x.shape
    n = w_big.shape[1]
    tm, tn = min(1024, s), 2048
    return pl.pallas_call(
        _proj_kernel,
        grid=(b, s // tm, n // tn),
        in_specs=[pl.BlockSpec((1, tm, d), lambda i, j, k: (i, j, 0)),
                  pl.BlockSpec((1, 9, d), lambda i, j, k: (i, 0, 0)),
                  pl.BlockSpec((1, d), lambda i, j, k: (0, 0)),
                  pl.BlockSpec((d, tn), lambda i, j, k: (0, k)),
                  pl.BlockSpec((d, LANES), lambda i, j, k: (0, 0))],
        out_specs=[pl.BlockSpec((1, tm, tn), lambda i, j, k: (i, j, k)),
                   pl.BlockSpec((1, tm, LANES), lambda i, j, k: (i, j, 0))],
        out_shape=[jax.ShapeDtypeStruct((b, s, n), BF16),
                   jax.ShapeDtypeStruct((b, s, LANES), F32)],
        scratch_shapes=[pltpu.VMEM((tm, d), BF16)],
        compiler_params=_params("parallel", "parallel", "arbitrary"),
        name="proj",
    )(x, mod, norm_w.reshape(1, d), w_big, w_gates)


def _split_dot(tri, y):
    hi = y.astype(BF16)
    r1 = y - hi.astype(F32)
    mid = r1.astype(BF16)
    lo = (r1 - mid.astype(F32)).astype(BF16)
    return _dot(tri, hi) + _dot(tri, mid) + _dot(tri, lo)


def _gate_prep_kernel(g_ref, p_ref, o_ref, y_scr, *, lm, lg):
    s = g_ref.shape[1]
    lane = lax.broadcasted_iota(jnp.int32, (1, LANES), 1)
    group = lane // HEADS
    pre = g_ref[0] + p_ref[0:1, :]
    a_coef = -jnp.exp(p_ref[1:2, :])
    tail = jnp.log(1.0 + jnp.exp(-jnp.abs(pre)))
    sp = jnp.maximum(pre, 0.0) + tail
    log_sig = jnp.minimum(pre, 0.0) - tail
    is_logf = (group == 1) | (group == 3)
    is_a = (group == 4) | (group == 6)
    is_beta = (group == 5) | (group == 7)
    y = jnp.where(is_logf, log_sig, pre)
    y = jnp.where(is_a, a_coef * sp, y)
    y = jnp.where(is_beta, _sigmoid(pre), y)
    y_scr[...] = y
    is_prefix = (group == 1) | (group == 4)
    is_suffix = (group == 3) | (group == 6)

    def cumsum_pass(l, lanes_sel):
        t = lax.broadcasted_iota(jnp.int32, (l, l), 0)
        u = lax.broadcasted_iota(jnp.int32, (l, l), 1)
        tri = (t >= u).astype(BF16)

        def body(c, carry):
            r0 = pl.multiple_of(c * l, l)
            yc = y_scr[pl.ds(r0, l), :]
            pre_c = _split_dot(tri, yc)
            tot = pre_c[l - 1:l, :]
            suf_c = tot - pre_c + yc
            out = jnp.where(is_prefix, pre_c, jnp.where(is_suffix, suf_c, yc))
            cur = o_ref[0, pl.ds(r0, l), :]
            o_ref[0, pl.ds(r0, l), :] = jnp.where(lanes_sel, out, cur)
            return carry

        lax.fori_loop(0, s // l, body, 0)

    o_ref[0] = y
    cumsum_pass(lm, group < 4)
    cumsum_pass(lg, group >= 4)


def _gate_prep(gates_raw, params):
    b, s, _ = gates_raw.shape
    return pl.pallas_call(
        functools.partial(_gate_prep_kernel, lm=M_CHUNK, lg=G_CHUNK),
        grid=(b,),
        in_specs=[pl.BlockSpec((1, s, LANES), lambda i: (i, 0, 0)),
                  pl.BlockSpec((8, LANES), lambda i: (0, 0))],
        out_specs=pl.BlockSpec((1, s, LANES), lambda i: (i, 0, 0)),
        out_shape=jax.ShapeDtypeStruct((b, s, LANES), F32),
        scratch_shapes=[pltpu.VMEM((s, LANES), F32)],
        compiler_params=_params("parallel"),
        name="gate_prep",
    )(gates_raw, params)


def _per_head_gates(g, l):
    b, s = g.shape[:2]
    cols = jnp.pad(g.transpose(0, 3, 1, 2), ((0, 0), (0, 0), (0, 0), (0, 4)))
    rows = g.transpose(0, 3, 2, 1).reshape(b, HEADS, 4, s // l, l).transpose(0, 1, 3, 2, 4)
    rows = jnp.pad(rows, ((0, 0), (0, 0), (0, 0), (0, 4), (0, 0)))
    return cols, rows


def _mlstm_kernel(q_ref, k_ref, v_ref, og_ref, gcol_ref, grow_ref, nw_ref, o_ref,
                  qt_scr, vt_scr, h_scr, c_scr, m_scr, *, l, nc, hb, cu):
    d = HEAD_DIM
    da = vt_scr.shape[2]
    scale = d ** -0.5
    s_idx = lax.broadcasted_iota(jnp.int32, (l, l), 0)
    t_idx = lax.broadcasted_iota(jnp.int32, (l, l), 1)
    ones_rows = (lax.broadcasted_iota(jnp.int32, (da - d, l), 0) == 0).astype(BF16)

    for hd in range(hb):
        for c in range(nc):
            qt_scr[hd, c] = q_ref[0, c * l:(c + 1) * l, hd * d:(hd + 1) * d].astype(F32).T.astype(BF16)
            vt = v_ref[0, c * l:(c + 1) * l, hd * d:(hd + 1) * d].astype(F32).T.astype(BF16)
            vt_scr[hd, c] = jnp.concatenate([vt, ones_rows], axis=0)

    def gate_idx(dr):
        return (M_X_F, M_C_F) if dr == 0 else (M_X_B, M_C_B)

    def chunk_of(dr, step):
        return step if dr == 0 else nc - 1 - step

    def total_logf(b_row, dr):
        return b_row[:, l - 1:l] if dr == 0 else b_row[:, 0:1]

    row_id = lax.broadcasted_iota(jnp.int32, (8 * nc, 1), 0) & 7
    for hd in range(hb):
        rows_all = grow_ref[0, hd].reshape(8 * nc, l)
        b_up = pltpu.roll(rows_all, 8 * nc - 1, axis=0)
        g_col = jnp.where(row_id == M_X_F, b_up[:, l - 1:l], b_up[:, 0:1])
        a_max = jnp.max(g_col - b_up + rows_all, axis=-1, keepdims=True)
        g_rep = jnp.broadcast_to(g_col, (8 * nc, l))
        a_rep = jnp.broadcast_to(a_max, (8 * nc, l))
        for dr in range(2):
            xi, _ = gate_idx(dr)
            m = jnp.zeros((1, l), F32)
            for step in range(nc):
                c = chunk_of(dr, step)
                r = 8 * c + xi
                m_scr[2 * hd + dr, c] = jnp.broadcast_to(m, (8, l))
                m = jnp.maximum(g_rep[r:r + 1, :] + m, a_rep[r:r + 1, :])

    c_scr[...] = jnp.zeros_like(c_scr)

    def body(it, carry):
        probs = []
        for k_step in range(cu):
            for hd in range(hb):
                for dr in range(2):
                    probs.append(dict(hd=hd, dr=dr, c=chunk_of(dr, it * cu + k_step)))
        for pr in probs:
            hd, c = pr["hd"], pr["c"]
            pr["k"] = k_ref[0, pl.ds(pl.multiple_of(c * l, l), l), hd * d:(hd + 1) * d]
            pr["qt"] = qt_scr[hd, c]
            pr["st"] = _dot(pr["k"], pr["qt"])
        for pr in probs:
            hd, dr, c = pr["hd"], pr["dr"], pr["c"]
            xi, ci = gate_idx(dr)
            cols = gcol_ref[0, hd, pl.ds(pl.multiple_of(c * l, l), l), :]
            rows = grow_ref[0, hd, c]
            c_col = cols[:, ci:ci + 1] - cols[:, xi:xi + 1]
            i_row, b_row = rows[xi:xi + 1, :], rows[ci:ci + 1, :]
            m_prev = m_scr[2 * hd + dr, c][0:1, :]
            mask = (s_idx <= t_idx) if dr == 0 else (s_idx >= t_idx)
            dmat = jnp.where(mask, b_row - c_col, -jnp.inf)
            inter = b_row + m_prev
            m_t = jnp.maximum(inter, jnp.max(dmat, axis=0, keepdims=True))
            pst = (pr["st"] * scale * jnp.exp(dmat - m_t)).astype(BF16)
            vt = vt_scr[hd, c]
            pr["num"] = _dot(vt, pst)
            pr["w_inter"] = jnp.exp(inter - m_t) * scale
            pr["floor"] = jnp.exp(-m_t)
            g = total_logf(b_row, dr)
            m_new = jnp.maximum(g + m_prev, jnp.max(g - b_row + i_row, axis=-1, keepdims=True))
            pr["decay"] = jnp.exp(g + m_prev - m_new)
            vw = (vt.astype(F32) * jnp.exp(g - b_row + i_row - m_new)).astype(BF16)
            pr["x"] = _dot(vw, pr["k"])
        cmats = {}
        for pr in probs:
            key = 2 * pr["hd"] + pr["dr"]
            cmat = cmats[key] if key in cmats else c_scr[key]
            pr["qc"] = _dot(cmat.astype(BF16), pr["qt"])
            cmats[key] = pr["decay"] * cmat + pr["x"]
        for key, cmat in cmats.items():
            c_scr[key] = cmat
        for pr in probs:
            hd, dr, c = pr["hd"], pr["dr"], pr["c"]
            num_aug = pr["num"] + pr["w_inter"] * pr["qc"]
            den = num_aug[d:d + 1, :]
            h_scr[2 * hd + dr, c] = num_aug[:d, :] / jnp.maximum(jnp.abs(den), pr["floor"])
        return carry

    lax.fori_loop(0, nc // cu, body, 0)
    for hd in range(hb):
        for c in range(nc):
            ht = h_scr[2 * hd, c] + h_scr[2 * hd + 1, c]
            ht = ht * lax.rsqrt(jnp.mean(ht * ht, axis=0, keepdims=True) + EPS)
            og = og_ref[0, c * l:(c + 1) * l, hd * d:(hd + 1) * d].astype(F32)
            o_ref[0, c * l:(c + 1) * l, hd * d:(hd + 1) * d] = (
                ht.T * nw_ref[...] * _sigmoid(og)).astype(BF16)


def _mlstm(proj, gcols, grows, norm_w):
    b, s, _ = proj.shape
    l = M_CHUNK
    nc = s // l
    hb = M_HEADS_PER_STEP
    cu = min(M_CHUNKS_PER_ITER, nc)
    d = HEAD_DIM
    blk = lambda off: pl.BlockSpec((1, s, hb * d), lambda i, h: (i, 0, off * HEADS // hb + h))
    return pl.pallas_call(
        functools.partial(_mlstm_kernel, l=l, nc=nc, hb=hb, cu=cu),
        grid=(b, HEADS // hb),
        in_specs=[blk(0), blk(1), blk(2), blk(3),
                  pl.BlockSpec((1, hb, s, 8), lambda i, h: (i, h, 0, 0)),
                  pl.BlockSpec((1, hb, nc, 8, l), lambda i, h: (i, h, 0, 0, 0)),
                  pl.BlockSpec((1, d), lambda i, h: (0, 0))],
        out_specs=pl.BlockSpec((1, s, hb * d), lambda i, h: (i, 0, h)),
        out_shape=jax.ShapeDtypeStruct((b, s, HEADS * d), BF16),
        scratch_shapes=[pltpu.VMEM((hb, nc, d, l), BF16),
                        pltpu.VMEM((hb, nc, d + BF16_SUBLANES, l), BF16),
                        pltpu.VMEM((2 * hb, nc, d, l), F32),
                        pltpu.VMEM((2 * hb, d + BF16_SUBLANES, d), F32),
                        pltpu.VMEM((2 * hb, nc, 8, LANES), F32)],
        compiler_params=_params("parallel", "parallel"),
        name="mlstm",
    )(proj, proj, proj, proj, gcols, grows, norm_w.reshape(1, d))


def _pair_rows(g, l):
    b, s = g.shape[:2]
    rows = g.transpose(0, 3, 2, 1).reshape(b, HEADS, 2, 2, s // l, l)
    rows = rows.transpose(0, 1, 4, 3, 2, 5).reshape(b, HEADS, s // l, 2, 2 * l)
    return jnp.pad(rows, ((0, 0), (0, 0), (0, 0), (0, 6), (0, 0)))


def _gdn_kernel(q_ref, k_ref, v_ref, z_ref, cw_ref, gcol_ref, grow_ref, nw_ref, o_ref,
                pad_scr, q_scr, k_scr, v_scr, qm_scr, n_scr, oa_scr, od_scr, st_scr,
                *, l, nc, hb, cu):
    s = q_ref.shape[1]
    d = HEAD_DIM
    scale = d ** -0.5
    half = CONV_K // 2

    pad_scr[0:8, :] = jnp.zeros((8, d), F32)
    pad_scr[8 + s:16 + s, :] = jnp.zeros((8, d), F32)

    rb = min(s, 256)

    def conv_silu(x_ref, dst_scr, which, hd, norm):
        pad_scr[8:8 + s, :] = x_ref[0, :, hd * d:(hd + 1) * d].astype(F32)
        for r0 in range(0, s, rb):
            acc = jnp.zeros((rb, d), F32)
            for j in range(CONV_K):
                lo = 8 - half + j + r0
                acc = acc + pad_scr[lo:lo + rb, :] * cw_ref[which, hd, j:j + 1, :]
            y = acc * _sigmoid(acc)
            if norm:
                y = y * lax.rsqrt(jnp.sum(y * y, axis=-1, keepdims=True) + EPS)
            dst_scr[hd, r0:r0 + rb, :] = y.astype(dst_scr.dtype)

    for hd in range(hb):
        conv_silu(q_ref, q_scr, 0, hd, True)
        conv_silu(k_ref, k_scr, 1, hd, True)
        conv_silu(v_ref, v_scr, 2, hd, False)

    t_idx = lax.broadcasted_iota(jnp.int32, (l, 2 * l), 0)
    lane = lax.broadcasted_iota(jnp.int32, (l, 2 * l), 1)
    is_f = lane < l
    s_idx = jnp.where(is_f, lane, lane - l)
    incl = jnp.where(is_f, t_idx - s_idx, s_idx - t_idx) >= 0
    diag = t_idx == s_idx
    eye2 = diag.astype(F32)
    left4 = lax.broadcasted_iota(jnp.int32, (l, 4 * d), 1) < 2 * d
    zeros2 = jnp.zeros((l, 2 * d), F32)

    def blockdiag(x2):
        return jnp.concatenate([jnp.where(is_f, x2, 0.0), jnp.where(is_f, 0.0, x2)], axis=0).astype(BF16)

    base = 8
    shift = base.bit_length() - 1
    same_base = (t_idx >> shift) == (s_idx >> shift)
    merge_masks = []
    size = base
    while size < l:
        sh = size.bit_length() - 1
        merge_masks.append(((t_idx >> (sh + 1)) == (s_idx >> (sh + 1))) & ((t_idx >> sh) != (s_idx >> sh)))
        size *= 2

    def body_a(it, carry):
        probs = [dict(c=it * cu + cc, hd=hd) for cc in range(cu) for hd in range(hb)]
        for pr in probs:
            c, hd = pr["c"], pr["hd"]
            r0 = pl.multiple_of(c * l, l)
            qb = q_scr[hd, pl.ds(r0, l), :]
            kb = k_scr[hd, pl.ds(r0, l), :]
            pr["kq"] = _dot_nt(jnp.concatenate([qb, kb], axis=0),
                               jnp.concatenate([kb, kb], axis=0))
        for pr in probs:
            c, hd = pr["c"], pr["hd"]
            cols = gcol_ref[0, hd, pl.ds(pl.multiple_of(c * l, l), l), :]
            gam_f, beta_f = cols[:, G_C_F:G_C_F + 1], cols[:, G_X_F:G_X_F + 1]
            gam_b, beta_b = cols[:, G_C_B:G_C_B + 1], cols[:, G_X_B:G_X_B + 1]
            gam_row = grow_ref[0, hd, c][0:1, :]
            e_incl = jnp.exp(jnp.where(incl, jnp.where(is_f, gam_f, gam_b) - gam_row, -jnp.inf))
            pr["attn2"] = pr["kq"][:l] * scale * e_incl
            a2 = jnp.where(is_f, beta_f, beta_b) * pr["kq"][l:] * jnp.where(diag, 0.0, e_incl)
            pr["a2"] = a2
            dblk = jnp.where(same_base, a2, 0.0)
            pr["p"] = eye2 - dblk
            pr["d2"] = _dot(dblk.astype(BF16), blockdiag(dblk))
        for pr in probs:
            res = _dot(jnp.concatenate([pr["p"], pr["d2"]], axis=0).astype(BF16), blockdiag(pr["d2"]))
            pr["p"] = pr["p"] + res[:l]
            pr["d4"] = res[l:]
        for pr in probs:
            pr["p"] = pr["p"] + _dot(pr["p"].astype(BF16), blockdiag(pr["d4"]))
        for mask in merge_masks:
            for pr in probs:
                pr["x"] = _dot(jnp.where(mask, pr["a2"], 0.0).astype(BF16), blockdiag(pr["p"]))
            for pr in probs:
                pr["p"] = pr["p"] - _dot(pr["p"].astype(BF16), blockdiag(pr["x"]))
        for pr in probs:
            c, hd = pr["c"], pr["hd"]
            r0 = pl.multiple_of(c * l, l)
            k = k_scr[hd, pl.ds(r0, l), :].astype(F32)
            v = v_scr[hd, pl.ds(r0, l), :].astype(F32)
            cols = gcol_ref[0, hd, pl.ds(r0, l), :]
            gam_f, beta_f = cols[:, G_C_F:G_C_F + 1], cols[:, G_X_F:G_X_F + 1]
            gam_b, beta_b = cols[:, G_C_B:G_C_B + 1], cols[:, G_X_B:G_X_B + 1]
            rhs_f = jnp.concatenate([-(beta_f * jnp.exp(gam_f)) * k, beta_f * v], axis=-1)
            rhs_b = jnp.concatenate([-(beta_b * jnp.exp(gam_b)) * k, beta_b * v], axis=-1)
            rhs_bd = jnp.concatenate([jnp.concatenate([rhs_f, zeros2], axis=-1),
                                      jnp.concatenate([zeros2, rhs_b], axis=-1)], axis=0).astype(BF16)
            pr["wu"] = _dot(pr["p"].astype(BF16), rhs_bd)
        for pr in probs:
            c, hd = pr["c"], pr["hd"]
            r0 = pl.multiple_of(c * l, l)
            k = k_scr[hd, pl.ds(r0, l), :].astype(F32)
            cols = gcol_ref[0, hd, pl.ds(r0, l), :]
            gam_f, gam_b = cols[:, G_C_F:G_C_F + 1], cols[:, G_C_B:G_C_B + 1]
            gam_row = grow_ref[0, hd, c][0:1, :]
            g_last_f, g_last_b = gam_row[:, l - 1:l], gam_row[:, l:l + 1]
            wu = pr["wu"]
            wu_bd = jnp.concatenate([jnp.where(left4, wu, 0.0), jnp.where(left4, 0.0, wu)],
                                    axis=0).astype(BF16)
            k_dec = jnp.concatenate([k * jnp.exp(g_last_f - gam_f), k * jnp.exp(g_last_b - gam_b)],
                                    axis=0)
            pr["res"] = _dot(jnp.concatenate([pr["attn2"], k_dec.T], axis=0).astype(BF16), wu_bd)
        for pr in probs:
            c, hd = pr["c"], pr["hd"]
            r0 = pl.multiple_of(c * l, l)
            q = q_scr[hd, pl.ds(r0, l), :].astype(F32)
            cols = gcol_ref[0, hd, pl.ds(r0, l), :]
            gam_f, gam_b = cols[:, G_C_F:G_C_F + 1], cols[:, G_C_B:G_C_B + 1]
            res = pr["res"]
            qd_f = q * (jnp.exp(gam_f) * scale) + res[:l, 0:d]
            qd_b = q * (jnp.exp(gam_b) * scale) + res[:l, 2 * d:3 * d]
            qm_scr[2 * hd, c] = jnp.concatenate([qd_f, res[l:, 0:d]], axis=0).astype(BF16)
            qm_scr[2 * hd + 1, c] = jnp.concatenate([qd_b, res[l:, 2 * d:3 * d]], axis=0).astype(BF16)
            n_scr[2 * hd, c] = res[l:, d:2 * d].astype(BF16)
            n_scr[2 * hd + 1, c] = res[l:, 3 * d:].astype(BF16)
            oa_scr[hd, pl.ds(r0, l), :] = res[:l, d:2 * d] + res[:l, 3 * d:]
        return carry

    lax.fori_loop(0, nc // cu, body_a, 0)

    st_scr[...] = jnp.zeros_like(st_scr)

    def body_b(j, carry):
        chains = [(hd, dr, j if dr == 0 else nc - 1 - j) for hd in range(hb) for dr in range(2)]
        states = [st_scr[2 * hd + dr] for hd, dr, _ in chains]
        rs = [_dot(qm_scr[2 * hd + dr, c], st.astype(BF16)) for (hd, dr, c), st in zip(chains, states)]
        for (hd, dr, c), state, r in zip(chains, states, rs):
            idx = 2 * hd + dr
            gam_row = grow_ref[0, hd, c][0:1, :]
            g_last = gam_row[:, l - 1:l] if dr == 0 else gam_row[:, l:l + 1]
            st_scr[idx] = jnp.exp(g_last) * state + r[l:] + n_scr[idx, c]
            od_scr[idx, pl.ds(pl.multiple_of(c * l, l), l), :] = r[:l].astype(BF16)
        return carry

    lax.fori_loop(0, nc, body_b, 0)

    for hd in range(hb):
        z = z_ref[0, :, hd * d:(hd + 1) * d].astype(F32)
        o = oa_scr[hd] + od_scr[2 * hd] + od_scr[2 * hd + 1]
        o_ref[0, :, hd * d:(hd + 1) * d] = (_rms(o, nw_ref[...]) * (z * _sigmoid(z))).astype(BF16)


def _gdn(proj, conv_w, gcols, grows, norm_w):
    b, s, _ = proj.shape
    l = G_CHUNK
    assert 2 * l == LANES
    nc = s // l
    hb = G_HEADS_PER_STEP
    cu = min(G_CHUNKS_PER_ITER, nc)
    d = HEAD_DIM
    blk = lambda off: pl.BlockSpec((1, s, hb * d), lambda i, h: (i, 0, off * HEADS // hb + h))
    return pl.pallas_call(
        functools.partial(_gdn_kernel, l=l, nc=nc, hb=hb, cu=cu),
        grid=(b, HEADS // hb),
        in_specs=[blk(4), blk(5), blk(6), blk(7),
                  pl.BlockSpec((3, hb, 8, d), lambda i, h: (0, h, 0, 0)),
                  pl.BlockSpec((1, hb, s, 8), lambda i, h: (i, h, 0, 0)),
                  pl.BlockSpec((1, hb, nc, 8, 2 * l), lambda i, h: (i, h, 0, 0, 0)),
                  pl.BlockSpec((1, d), lambda i, h: (0, 0))],
        out_specs=pl.BlockSpec((1, s, hb * d), lambda i, h: (i, 0, h)),
        out_shape=jax.ShapeDtypeStruct((b, s, HEADS * d), BF16),
        scratch_shapes=[pltpu.VMEM((s + 16, d), F32),
                        pltpu.VMEM((hb, s, d), BF16), pltpu.VMEM((hb, s, d), BF16),
                        pltpu.VMEM((hb, s, d), BF16),
                        pltpu.VMEM((2 * hb, nc, l + d, d), BF16),
                        pltpu.VMEM((2 * hb, nc, d, d), BF16),
                        pltpu.VMEM((hb, s, d), F32),
                        pltpu.VMEM((2 * hb, s, d), BF16),
                        pltpu.VMEM((2 * hb, d, d), F32)],
        compiler_params=_params("parallel", "parallel"),
        name="gdn",
    )(proj, proj, proj, proj, conv_w, gcols, grows, norm_w.reshape(1, d))


def _merge_kernel(x_ref, mod_ref, hm_ref, hg_ref, mm_ref, mg_ref, wm_ref, wg_ref, wo_ref, o_ref):
    gate = mod_ref[0, 5:6, :]
    y = (_sigmoid(mm_ref[0].astype(F32)) * _dot(hm_ref[0], wm_ref[...])
         + _sigmoid(mg_ref[0].astype(F32)) * _dot(hg_ref[0], wg_ref[...]))
    o_ref[0] = x_ref[0] + gate * _dot(y.astype(BF16), wo_ref[...])


def _merge(x, mod, h_m, h_g, proj, w_m, w_g, w_o):
    b, s, d = x.shape
    tm = min(512, s)
    width = HEADS * HEAD_DIM
    tok = lambda w, off: pl.BlockSpec((1, tm, w), lambda i, j: (i, j, off))
    wspec = lambda r: pl.BlockSpec((r, d), lambda i, j: (0, 0))
    merge_off = 8 * width // d
    return pl.pallas_call(
        _merge_kernel,
        grid=(b, s // tm),
        in_specs=[tok(d, 0),
                  pl.BlockSpec((1, 9, d), lambda i, j: (i, 0, 0)),
                  tok(width, 0), tok(width, 0),
                  tok(d, merge_off), tok(d, merge_off + 1),
                  wspec(width), wspec(width), wspec(d)],
        out_specs=tok(d, 0),
        out_shape=jax.ShapeDtypeStruct((b, s, d), F32),
        compiler_params=_params("parallel", "parallel"),
        name="merge",
    )(x, mod, h_m, h_g, proj, proj, w_m, w_g, w_o)


def kernel(x, c, w_ada, b_ada, norm_ffn1, w_ffn1_in, w_ffn1_out, norm_mix, w_in, mlstm_gate_bias,
           gdn_a_log, gdn_dt_bias, gdn_conv_w, mlstm_out_norm, gdn_out_norm, w_branch_mlstm,
           w_branch_gdn, w_out, norm_ffn2, w_ffn2_in, w_ffn2_out, norm_final):
    b, s, d = x.shape
    depth = w_ada.shape[0]
    width = HEADS * HEAD_DIM
    ng = 4 * HEADS
    sizes = (width,) * 4 + (ng,) + (width,) * 4 + (ng,) + (d, d)
    offs = [0]
    for sz in sizes:
        offs.append(offs[-1] + sz)
    for layer in range(depth):
        mod = _ada(c, w_ada[layer], b_ada[layer]).reshape(b, 9, d)
        x = _ffn(x, mod, norm_ffn1[layer], w_ffn1_in[layer], w_ffn1_out[layer], norm_final,
                 sub=0, final=False)

        wl = w_in[layer]
        col = lambda i: wl[:, offs[i]:offs[i + 1]]
        w_big = jnp.concatenate([col(i) for i in (0, 1, 2, 3, 5, 6, 7, 8, 10, 11)], axis=1).astype(BF16)
        w_gates = jnp.concatenate([col(4), col(9), jnp.zeros((d, LANES - 2 * ng), F32)],
                                  axis=1).astype(BF16)
        proj, gates_raw = _proj(x, mod, norm_mix[layer], w_big, w_gates)

        zeros_h = jnp.zeros((HEADS,), F32)
        bias_row = jnp.concatenate([mlstm_gate_bias[layer].reshape(ng),
                                    gdn_dt_bias[layer][0], zeros_h, gdn_dt_bias[layer][1], zeros_h,
                                    jnp.zeros((LANES - 2 * ng,), F32)])
        alog_row = jnp.concatenate([jnp.zeros((ng,), F32),
                                    gdn_a_log[layer][0], zeros_h, gdn_a_log[layer][1], zeros_h,
                                    jnp.zeros((LANES - 2 * ng,), F32)])
        gparams = jnp.zeros((8, LANES), F32).at[0].set(bias_row).at[1].set(alog_row)
        gates = _gate_prep(gates_raw, gparams)
        m_cols, m_rows = _per_head_gates(gates[:, :, :ng].reshape(b, s, 4, HEADS), M_CHUNK)
        g_gates = gates[:, :, ng:2 * ng].reshape(b, s, 4, HEADS)
        g_cols = _per_head_gates(g_gates, G_CHUNK)[0]
        g_rows = _pair_rows(g_gates, G_CHUNK)

        h_m = _mlstm(proj, m_cols, m_rows, mlstm_out_norm[layer])
        conv_w = gdn_conv_w[layer].reshape(CONV_K, 3, HEADS, HEAD_DIM).transpose(1, 2, 0, 3)
        conv_w = jnp.pad(conv_w, ((0, 0), (0, 0), (0, 8 - CONV_K), (0, 0)))
        h_g = _gdn(proj, conv_w, g_cols, g_rows, gdn_out_norm[layer])

        x = _merge(x, mod, h_m, h_g, proj, w_branch_mlstm[layer].astype(BF16),
                   w_branch_gdn[layer].astype(BF16), w_out[layer].astype(BF16))
        last = layer == depth - 1
        x = _ffn(x, mod, norm_ffn2[layer], w_ffn2_in[layer], w_ffn2_out[layer], norm_final,
                 sub=2, final=last)
    return x
```

```python
import functools

import jax
import jax.numpy as jnp
from jax import lax
from jax.experimental import pallas as pl
from jax.experimental.pallas import tpu as pltpu

F32 = jnp.float32
BF16 = jnp.bfloat16

EPS = 1e-6
FFN_RES = 0.5
HEADS = 8
HEAD_DIM = 128
CONV_K = 5
LANES = 128
BF16_SUBLANES = 16
M_CHUNK = 128
M_HEADS_PER_STEP = 2
M_CHUNKS_PER_ITER = 4
G_CHUNK = 64
G_HEADS_PER_STEP = 2
G_CHUNKS_PER_ITER = 8
VMEM_LIMIT = 56 * 1024 * 1024

M_X_F, M_C_F, M_X_B, M_C_B = 0, 1, 2, 3
G_C_F, G_X_F, G_C_B, G_X_B = 0, 1, 2, 3


def _dot(a, b):
    return jnp.dot(a, b, preferred_element_type=F32)


def _dot_nt(a, b):
    return lax.dot_general(a, b, (((1,), (1,)), ((), ())), preferred_element_type=F32)


def _dot_tn(a, b):
    return lax.dot_general(a, b, (((0,), (0,)), ((), ())), preferred_element_type=F32)


def _rms(x, w):
    return (x * lax.rsqrt(jnp.mean(x * x, axis=-1, keepdims=True) + EPS)) * w


def _sigmoid(x):
    return 1.0 / (1.0 + jnp.exp(-x))


def _params(*sem):
    return pltpu.CompilerParams(dimension_semantics=sem, vmem_limit_bytes=VMEM_LIMIT)


def _ada_kernel(c_ref, w_ref, b_ref, o_ref):
    c = c_ref[...]
    cs = (c * _sigmoid(c)).astype(BF16)
    o_ref[...] = _dot(cs, w_ref[...].astype(BF16)) + b_ref[...]


def _ada(c, w_ada, b_ada):
    b, d = c.shape
    n = w_ada.shape[1]
    tn = 1024
    return pl.pallas_call(
        _ada_kernel,
        grid=(n // tn,),
        in_specs=[pl.BlockSpec((b, d), lambda j: (0, 0)),
                  pl.BlockSpec((d, tn), lambda j: (0, j)),
                  pl.BlockSpec((1, tn), lambda j: (0, j))],
        out_specs=pl.BlockSpec((b, tn), lambda j: (0, j)),
        out_shape=jax.ShapeDtypeStruct((b, n), F32),
        compiler_params=_params("parallel"),
        name="ada",
    )(c, w_ada, b_ada.reshape(1, n))


def _ffn_kernel(x_ref, mod_ref, nw_ref, wg_ref, wu_ref, wo_ref, nf_ref, o_ref, a_scr,
                *, sub, final, fc):
    x = x_ref[0]
    shift = mod_ref[0, 3 * sub + 0:3 * sub + 1, :]
    scale = mod_ref[0, 3 * sub + 1:3 * sub + 2, :]
    gate = mod_ref[0, 3 * sub + 2:3 * sub + 3, :]
    h = (_rms(x, nw_ref[...]) * (1.0 + scale) + shift).astype(BF16)
    f = wg_ref.shape[1]
    for c0 in range(0, f, fc):
        c1 = min(c0 + fc, f)
        g = _dot(h, wg_ref[:, c0:c1])
        u = _dot(h, wu_ref[:, c0:c1])
        a_scr[:, c0:c1] = (g * _sigmoid(g) * u).astype(BF16)
    y = x + FFN_RES * gate * _dot(a_scr[...], wo_ref[...])
    if final:
        y = _rms(y, nf_ref[...])
    o_ref[0] = y


def _ffn(x, mod, norm_w, w_in, w_out, norm_final, *, sub, final):
    b, s, d = x.shape
    f = w_out.shape[0]
    tm = min(512, s)
    wg = w_in[:, :f].astype(BF16)
    wu = w_in[:, f:].astype(BF16)
    wo = w_out.astype(BF16)
    const2 = lambda i, j: (0, 0)
    single = pl.Buffered(1)
    return pl.pallas_call(
        functools.partial(_ffn_kernel, sub=sub, final=final, fc=512),
        grid=(b, s // tm),
        in_specs=[pl.BlockSpec((1, tm, d), lambda i, j: (i, j, 0)),
                  pl.BlockSpec((1, 9, d), lambda i, j: (i, 0, 0)),
                  pl.BlockSpec((1, d), const2),
                  pl.BlockSpec((d, f), const2, pipeline_mode=single),
                  pl.BlockSpec((d, f), const2, pipeline_mode=single),
                  pl.BlockSpec((f, d), const2, pipeline_mode=single),
                  pl.BlockSpec((1, d), const2)],
        out_specs=pl.BlockSpec((1, tm, d), lambda i, j: (i, j, 0)),
        out_shape=jax.ShapeDtypeStruct((b, s, d), F32),
        scratch_shapes=[pltpu.VMEM((tm, f), BF16)],
        compiler_params=_params("parallel", "parallel"),
        name="ffn_final" if final else "ffn",
    )(x, mod, norm_w.reshape(1, d), wg, wu, wo, norm_final.reshape(1, d))


def _proj_kernel(x_ref, mod_ref, nw_ref, w_ref, wgate_ref, o_ref, gate_ref, h_scr):
    @pl.when(pl.program_id(2) == 0)
    def _():
        shift = mod_ref[0, 3:4, :]
        scale = mod_ref[0, 4:5, :]
        h = (_rms(x_ref[0], nw_ref[...]) * (1.0 + scale) + shift).astype(BF16)
        h_scr[...] = h
        gate_ref[0] = _dot(h, wgate_ref[...])

    o_ref[0] = _dot(h_scr[...], w_ref[...]).astype(BF16)


def _proj(x, mod, norm_w, w_big, w_gates):
    b, s, d = x.shape
    n = w_big.shape[1]
    tm, tn = min(1024, s), 2048
    return pl.pallas_call(
        _proj_kernel,
        grid=(b, s // tm, n // tn),
        in_specs=[pl.BlockSpec((1, tm, d), lambda i, j, k: (i, j, 0)),
                  pl.BlockSpec((1, 9, d), lambda i, j, k: (i, 0, 0)),
                  pl.BlockSpec((1, d), lambda i, j, k: (0, 0)),
                  pl.BlockSpec((d, tn), lambda i, j, k: (0, k)),
                  pl.BlockSpec((d, LANES), lambda i, j, k: (0, 0))],
        out_specs=[pl.BlockSpec((1, tm, tn), lambda i, j, k: (i, j, k)),
                   pl.BlockSpec((1, tm, LANES), lambda i, j, k: (i, j, 0))],
        out_shape=[jax.ShapeDtypeStruct((b, s, n), BF16),
                   jax.ShapeDtypeStruct((b, s, LANES), F32)],
        scratch_shapes=[pltpu.VMEM((tm, d), BF16)],
        compiler_params=_params("parallel", "parallel", "arbitrary"),
        name="proj",
    )(x, mod, norm_w.reshape(1, d), w_big, w_gates)


def _split_dot(tri, y):
    hi = y.astype(BF16)
    r1 = y - hi.astype(F32)
    mid = r1.astype(BF16)
    lo = (r1 - mid.astype(F32)).astype(BF16)
    return _dot(tri, hi) + _dot(tri, mid) + _dot(tri, lo)


def _gate_prep_kernel(g_ref, p_ref, o_ref, y_scr, *, lm, lg):
    s = g_ref.shape[1]
    lane = lax.broadcasted_iota(jnp.int32, (1, LANES), 1)
    group = lane // HEADS
    pre = g_ref[0] + p_ref[0:1, :]
    a_coef = -jnp.exp(p_ref[1:2, :])
    tail = jnp.log(1.0 + jnp.exp(-jnp.abs(pre)))
    sp = jnp.maximum(pre, 0.0) + tail
    log_sig = jnp.minimum(pre, 0.0) - tail
    is_logf = (group == 1) | (group == 3)
    is_a = (group == 4) | (group == 6)
    is_beta = (group == 5) | (group == 7)
    y = jnp.where(is_logf, log_sig, pre)
    y = jnp.where(is_a, a_coef * sp, y)
    y = jnp.where(is_beta, _sigmoid(pre), y)
    y_scr[...] = y
    is_prefix = (group == 1) | (group == 4)
    is_suffix = (group == 3) | (group == 6)

    def cumsum_pass(l, lanes_sel):
        t = lax.broadcasted_iota(jnp.int32, (l, l), 0)
        u = lax.broadcasted_iota(jnp.int32, (l, l), 1)
        tri = (t >= u).astype(BF16)

        def body(c, carry):
            r0 = pl.multiple_of(c * l, l)
            yc = y_scr[pl.ds(r0, l), :]
            pre_c = _split_dot(tri, yc)
            tot = pre_c[l - 1:l, :]
            suf_c = tot - pre_c + yc
            out = jnp.where(is_prefix, pre_c, jnp.where(is_suffix, suf_c, yc))
            cur = o_ref[0, pl.ds(r0, l), :]
            o_ref[0, pl.ds(r0, l), :] = jnp.where(lanes_sel, out, cur)
            return carry

        lax.fori_loop(0, s // l, body, 0)

    o_ref[0] = y
    cumsum_pass(lm, group < 4)
    cumsum_pass(lg, group >= 4)


def _gate_prep(gates_raw, params):
    b, s, _ = gates_raw.shape
    return pl.pallas_call(
        functools.partial(_gate_prep_kernel, lm=M_CHUNK, lg=G_CHUNK),
        grid=(b,),
        in_specs=[pl.BlockSpec((1, s, LANES), lambda i: (i, 0, 0)),
                  pl.BlockSpec((8, LANES), lambda i: (0, 0))],
        out_specs=pl.BlockSpec((1, s, LANES), lambda i: (i, 0, 0)),
        out_shape=jax.ShapeDtypeStruct((b, s, LANES), F32),
        scratch_shapes=[pltpu.VMEM((s, LANES), F32)],
        compiler_params=_params("parallel"),
        name="gate_prep",
    )(gates_raw, params)


def _per_head_gates(g, l):
    b, s = g.shape[:2]
    cols = jnp.pad(g.transpose(0, 3, 1, 2), ((0, 0), (0, 0), (0, 0), (0, 4)))
    rows = g.transpose(0, 3, 2, 1).reshape(b, HEADS, 4, s // l, l).transpose(0, 1, 3, 2, 4)
    rows = jnp.pad(rows, ((0, 0), (0, 0), (0, 0), (0, 4), (0, 0)))
    return cols, rows


def _mlstm_kernel(q_ref, k_ref, v_ref, og_ref, gcol_ref, grow_ref, nw_ref, o_ref,
                  qt_scr, vt_scr, h_scr, c_scr, m_scr, *, l, nc, hb, cu):
    d = HEAD_DIM
    da = vt_scr.shape[2]
    scale = d ** -0.5
    s_idx = lax.broadcasted_iota(jnp.int32, (l, l), 0)
    t_idx = lax.broadcasted_iota(jnp.int32, (l, l), 1)
    ones_rows = (lax.broadcasted_iota(jnp.int32, (da - d, l), 0) == 0).astype(BF16)

    for hd in range(hb):
        for c in range(nc):
            qt_scr[hd, c] = q_ref[0, c * l:(c + 1) * l, hd * d:(hd + 1) * d].astype(F32).T.astype(BF16)
            vt = v_ref[0, c * l:(c + 1) * l, hd * d:(hd + 1) * d].astype(F32).T.astype(BF16)
            vt_scr[hd, c] = jnp.concatenate([vt, ones_rows], axis=0)

    def gate_idx(dr):
        return (M_X_F, M_C_F) if dr == 0 else (M_X_B, M_C_B)

    def chunk_of(dr, step):
        return step if dr == 0 else nc - 1 - step

    def total_logf(b_row, dr):
        return b_row[:, l - 1:l] if dr == 0 else b_row[:, 0:1]

    row_id = lax.broadcasted_iota(jnp.int32, (8 * nc, 1), 0) & 7
    for hd in range(hb):
        rows_all = grow_ref[0, hd].reshape(8 * nc, l)
        b_up = pltpu.roll(rows_all, 8 * nc - 1, axis=0)
        g_col = jnp.where(row_id == M_X_F, b_up[:, l - 1:l], b_up[:, 0:1])
        a_max = jnp.max(g_col - b_up + rows_all, axis=-1, keepdims=True)
        g_rep = jnp.broadcast_to(g_col, (8 * nc, l))
        a_rep = jnp.broadcast_to(a_max, (8 * nc, l))
        for dr in range(2):
            xi, _ = gate_idx(dr)
            m = jnp.zeros((1, l), F32)
            for step in range(nc):
                c = chunk_of(dr, step)
                r = 8 * c + xi
                m_scr[2 * hd + dr, c] = jnp.broadcast_to(m, (8, l))
                m = jnp.maximum(g_rep[r:r + 1, :] + m, a_rep[r:r + 1, :])

    c_scr[...] = jnp.zeros_like(c_scr)

    def body(it, carry):
        probs = []
        for k_step in range(cu):
            for hd in range(hb):
                for dr in range(2):
                    probs.append(dict(hd=hd, dr=dr, c=chunk_of(dr, it * cu + k_step)))
        for pr in probs:
            hd, c = pr["hd"], pr["c"]
            pr["k"] = k_ref[0, pl.ds(pl.multiple_of(c * l, l), l), hd * d:(hd + 1) * d]
            pr["qt"] = qt_scr[hd, c]
            pr["st"] = _dot(pr["k"], pr["qt"])
        for pr in probs:
            hd, dr, c = pr["hd"], pr["dr"], pr["c"]
            xi, ci = gate_idx(dr)
            cols = gcol_ref[0, hd, pl.ds(pl.multiple_of(c * l, l), l), :]
            rows = grow_ref[0, hd, c]
            c_col = cols[:, ci:ci + 1] - cols[:, xi:xi + 1]
            i_row, b_row = rows[xi:xi + 1, :], rows[ci:ci + 1, :]
            m_prev = m_scr[2 * hd + dr, c][0:1, :]
            mask = (s_idx <= t_idx) if dr == 0 else (s_idx >= t_idx)
            dmat = jnp.where(mask, b_row - c_col, -jnp.inf)
            inter = b_row + m_prev
            m_t = jnp.maximum(inter, jnp.max(dmat, axis=0, keepdims=True))
            pst = (pr["st"] * scale * jnp.exp(dmat - m_t)).astype(BF16)
            vt = vt_scr[hd, c]
            pr["num"] = _dot(vt, pst)
            pr["w_inter"] = jnp.exp(inter - m_t) * scale
            pr["floor"] = jnp.exp(-m_t)
            g = total_logf(b_row, dr)
            m_new = jnp.maximum(g + m_prev, jnp.max(g - b_row + i_row, axis=-1, keepdims=True))
            pr["decay"] = jnp.exp(g + m_prev - m_new)
            vw = (vt.astype(F32) * jnp.exp(g - b_row + i_row - m_new)).astype(BF16)
            pr["x"] = _dot(vw, pr["k"])
        cmats = {}
        for pr in probs:
            key = 2 * pr["hd"] + pr["dr"]
            cmat = cmats[key] if key in cmats else c_scr[key]
            pr["qc"] = _dot(cmat.astype(BF16), pr["qt"])
            cmats[key] = pr["decay"] * cmat + pr["x"]
        for key, cmat in cmats.items():
            c_scr[key] = cmat
        for pr in probs:
            hd, dr, c = pr["hd"], pr["dr"], pr["c"]
            num_aug = pr["num"] + pr["w_inter"] * pr["qc"]
            den = num_aug[d:d + 1, :]
            h_scr[2 * hd + dr, c] = num_aug[:d, :] / jnp.maximum(jnp.abs(den), pr["floor"])
        return carry

    lax.fori_loop(0, nc // cu, body, 0)
    for hd in range(hb):
        for c in range(nc):
            ht = h_scr[2 * hd, c] + h_scr[2 * hd + 1, c]
            ht = ht * lax.rsqrt(jnp.mean(ht * ht, axis=0, keepdims=True) + EPS)
            og = og_ref[0, c * l:(c + 1) * l, hd * d:(hd + 1) * d].astype(F32)
            o_ref[0, c * l:(c + 1) * l, hd * d:(hd + 1) * d] = (
                ht.T * nw_ref[...] * _sigmoid(og)).astype(BF16)


def _mlstm(proj, gcols, grows, norm_w):
    b, s, _ = proj.shape
    l = M_CHUNK
    nc = s // l
    hb = M_HEADS_PER_STEP
    cu = min(M_CHUNKS_PER_ITER, nc)
    d = HEAD_DIM
    blk = lambda off: pl.BlockSpec((1, s, hb * d), lambda i, h: (i, 0, off * HEADS // hb + h))
    return pl.pallas_call(
        functools.partial(_mlstm_kernel, l=l, nc=nc, hb=hb, cu=cu),
        grid=(b, HEADS // hb),
        in_specs=[blk(0), blk(1), blk(2), blk(3),
                  pl.BlockSpec((1, hb, s, 8), lambda i, h: (i, h, 0, 0)),
                  pl.BlockSpec((1, hb, nc, 8, l), lambda i, h: (i, h, 0, 0, 0)),
                  pl.BlockSpec((1, d), lambda i, h: (0, 0))],
        out_specs=pl.BlockSpec((1, s, hb * d), lambda i, h: (i, 0, h)),
        out_shape=jax.ShapeDtypeStruct((b, s, HEADS * d), BF16),
        scratch_shapes=[pltpu.VMEM((hb, nc, d, l), BF16),
                        pltpu.VMEM((hb, nc, d + BF16_SUBLANES, l), BF16),
                        pltpu.VMEM((2 * hb, nc, d, l), F32),
                        pltpu.VMEM((2 * hb, d + BF16_SUBLANES, d), F32),
                        pltpu.VMEM((2 * hb, nc, 8, LANES), F32)],
        compiler_params=_params("parallel", "parallel"),
        name="mlstm",
    )(proj, proj, proj, proj, gcols, grows, norm_w.reshape(1, d))


def _pair_rows(g, l):
    b, s = g.shape[:2]
    rows = g.transpose(0, 3, 2, 1).reshape(b, HEADS, 2, 2, s // l, l)
    rows = rows.transpose(0, 1, 4, 3, 2, 5).reshape(b, HEADS, s // l, 2, 2 * l)
    return jnp.pad(rows, ((0, 0), (0, 0), (0, 0), (0, 6), (0, 0)))


def _gdn_kernel(q_ref, k_ref, v_ref, z_ref, cw_ref, gcol_ref, grow_ref, nw_ref, o_ref,
                pad_scr, q_scr, k_scr, v_scr, kt_scr, qm_scr, n_scr, oa_scr, od_scr, st_scr,
                *, l, nc, hb, cu):
    s = q_ref.shape[1]
    d = HEAD_DIM
    scale = d ** -0.5
    half = CONV_K // 2

    for which in range(3):
        pad_scr[which, 0:8, :] = jnp.zeros((8, d), F32)
        pad_scr[which, 8 + s:16 + s, :] = jnp.zeros((8, d), F32)

    rb = min(s, 256)
    srcs = (q_ref, k_ref, v_ref)
    dsts = (q_scr, k_scr, v_scr)

    for hd in range(hb):
        def fill(i, carry, hd=hd):
            r0 = pl.multiple_of(i * rb, rb)
            for which in range(3):
                pad_scr[which, pl.ds(8 + r0, rb), :] = (
                    srcs[which][0, pl.ds(r0, rb), hd * d:(hd + 1) * d].astype(F32))
            return carry

        lax.fori_loop(0, s // rb, fill, 0)

        def block(i, carry, hd=hd):
            r0 = pl.multiple_of(i * rb, rb)
            for which in range(3):
                acc = pad_scr[which, pl.ds(r0 + 8 - half, rb), :] * cw_ref[which, hd, 0:1, :]
                for j in range(1, CONV_K):
                    acc = acc + (pad_scr[which, pl.ds(r0 + 8 - half + j, rb), :]
                                 * cw_ref[which, hd, j:j + 1, :])
                y = acc * _sigmoid(acc)
                if which < 2:
                    y = y * lax.rsqrt(jnp.sum(y * y, axis=-1, keepdims=True) + EPS)
                dsts[which][hd, pl.ds(r0, rb), :] = y.astype(BF16)
                if which == 1:
                    for cc in range(rb // l):
                        yc = y[cc * l:(cc + 1) * l, :]
                        kt_scr[hd, i * (rb // l) + cc] = jnp.concatenate([yc, yc], axis=0).T.astype(BF16)
            return carry

        lax.fori_loop(0, s // rb, block, 0)

    t_idx = lax.broadcasted_iota(jnp.int32, (l, 2 * l), 0)
    lane = lax.broadcasted_iota(jnp.int32, (l, 2 * l), 1)
    is_f = lane < l
    s_idx = jnp.where(is_f, lane, lane - l)
    incl = jnp.where(is_f, t_idx - s_idx, s_idx - t_idx) >= 0
    diag = t_idx == s_idx
    eye2 = diag.astype(F32)
    zeros_b = jnp.zeros((l, d), BF16)

    def blockdiag(x2):
        return jnp.concatenate([jnp.where(is_f, x2, 0.0), jnp.where(is_f, 0.0, x2)], axis=0).astype(BF16)

    base = 8
    shift = base.bit_length() - 1
    same_base = (t_idx >> shift) == (s_idx >> shift)
    merge_masks = []
    size = base
    while size < l:
        sh = size.bit_length() - 1
        merge_masks.append(((t_idx >> (sh + 1)) == (s_idx >> (sh + 1))) & ((t_idx >> sh) != (s_idx >> sh)))
        size *= 2

    def body_a(it, carry):
        probs = [dict(c=it * cu + cc, hd=hd) for cc in range(cu) for hd in range(hb)]
        for pr in probs:
            c, hd = pr["c"], pr["hd"]
            r0 = pl.multiple_of(c * l, l)
            qb = q_scr[hd, pl.ds(r0, l), :]
            kb = k_scr[hd, pl.ds(r0, l), :]
            pr["kq"] = _dot_nt(jnp.concatenate([qb, kb], axis=0),
                               jnp.concatenate([kb, kb], axis=0))
        for pr in probs:
            c, hd = pr["c"], pr["hd"]
            cols = gcol_ref[0, hd, pl.ds(pl.multiple_of(c * l, l), l), :]
            gam_f, beta_f = cols[:, G_C_F:G_C_F + 1], cols[:, G_X_F:G_X_F + 1]
            gam_b, beta_b = cols[:, G_C_B:G_C_B + 1], cols[:, G_X_B:G_X_B + 1]
            gam_row = grow_ref[0, hd, c][0:1, :]
            e_incl = jnp.exp(jnp.where(incl, jnp.where(is_f, gam_f, gam_b) - gam_row, -jnp.inf))
            pr["attn2"] = pr["kq"][:l] * scale * e_incl
            a2 = jnp.where(is_f, beta_f, beta_b) * pr["kq"][l:] * jnp.where(diag, 0.0, e_incl)
            pr["a2"] = a2
            dblk = jnp.where(same_base, a2, 0.0)
            pr["p"] = eye2 - dblk
            pr["d2"] = _dot(dblk.astype(BF16), blockdiag(dblk))
        for pr in probs:
            res = _dot(jnp.concatenate([pr["p"], pr["d2"]], axis=0).astype(BF16), blockdiag(pr["d2"]))
            pr["p"] = pr["p"] + res[:l]
            pr["d4"] = res[l:]
        for pr in probs:
            pr["p"] = pr["p"] + _dot(pr["p"].astype(BF16), blockdiag(pr["d4"]))
        for mask in merge_masks:
            for pr in probs:
                pr["x"] = _dot(jnp.where(mask, pr["a2"], 0.0).astype(BF16), blockdiag(pr["p"]))
            for pr in probs:
                pr["p"] = pr["p"] - _dot(pr["p"].astype(BF16), blockdiag(pr["x"]))
        for pr in probs:
            c, hd = pr["c"], pr["hd"]
            r0 = pl.multiple_of(c * l, l)
            kb = k_scr[hd, pl.ds(r0, l), :]
            vb = v_scr[hd, pl.ds(r0, l), :]
            rows = grow_ref[0, hd, c]
            gam_row, beta_row = rows[0:1, :], rows[1:2, :]
            tk = (pr["p"] * (-beta_row * jnp.exp(gam_row))).astype(BF16)
            tv = (pr["p"] * beta_row).astype(BF16)
            pr["wk"] = _dot(tk, jnp.concatenate([jnp.concatenate([kb, zeros_b], axis=-1),
                                                 jnp.concatenate([zeros_b, kb], axis=-1)], axis=0))
            pr["wv"] = _dot(tv, jnp.concatenate([jnp.concatenate([vb, zeros_b], axis=-1),
                                                 jnp.concatenate([zeros_b, vb], axis=-1)], axis=0))
        for pr in probs:
            c, hd = pr["c"], pr["hd"]
            gam_row = grow_ref[0, hd, c][0:1, :]
            g_last = jnp.where(is_f[0:1, :], gam_row[:, l - 1:l], gam_row[:, l:l + 1])
            wk, wv = pr["wk"].astype(BF16), pr["wv"].astype(BF16)
            wu_bd = jnp.concatenate(
                [jnp.concatenate([wk[:, :d], wv[:, :d], zeros_b, zeros_b], axis=-1),
                 jnp.concatenate([zeros_b, zeros_b, wk[:, d:], wv[:, d:]], axis=-1)], axis=0)
            k_dec_t = (kt_scr[hd, c].astype(F32) * jnp.exp(g_last - gam_row)).astype(BF16)
            pr["res"] = _dot(jnp.concatenate([pr["attn2"].astype(BF16), k_dec_t], axis=0), wu_bd)
        for pr in probs:
            c, hd = pr["c"], pr["hd"]
            r0 = pl.multiple_of(c * l, l)
            q = q_scr[hd, pl.ds(r0, l), :].astype(F32)
            cols = gcol_ref[0, hd, pl.ds(r0, l), :]
            gam_f, gam_b = cols[:, G_C_F:G_C_F + 1], cols[:, G_C_B:G_C_B + 1]
            res = pr["res"]
            qd_f = q * (jnp.exp(gam_f) * scale) + res[:l, 0:d]
            qd_b = q * (jnp.exp(gam_b) * scale) + res[:l, 2 * d:3 * d]
            qm_scr[2 * hd, c] = jnp.concatenate([qd_f, res[l:, 0:d]], axis=0).astype(BF16)
            qm_scr[2 * hd + 1, c] = jnp.concatenate([qd_b, res[l:, 2 * d:3 * d]], axis=0).astype(BF16)
            n_scr[2 * hd, c] = res[l:, d:2 * d].astype(BF16)
            n_scr[2 * hd + 1, c] = res[l:, 3 * d:].astype(BF16)
            oa_scr[hd, pl.ds(r0, l), :] = res[:l, d:2 * d] + res[:l, 3 * d:]
        return carry

    lax.fori_loop(0, nc // cu, body_a, 0)

    st_scr[...] = jnp.zeros_like(st_scr)

    def body_b(j, carry):
        chains = [(hd, dr, j if dr == 0 else nc - 1 - j) for hd in range(hb) for dr in range(2)]
        states = [st_scr[2 * hd + dr] for hd, dr, _ in chains]
        rs = [_dot(qm_scr[2 * hd + dr, c], st.astype(BF16)) for (hd, dr, c), st in zip(chains, states)]
        for (hd, dr, c), state, r in zip(chains, states, rs):
            idx = 2 * hd + dr
            gam_row = grow_ref[0, hd, c][0:1, :]
            g_last = gam_row[:, l - 1:l] if dr == 0 else gam_row[:, l:l + 1]
            st_scr[idx] = jnp.exp(g_last) * state + r[l:] + n_scr[idx, c]
            od_scr[idx, pl.ds(pl.multiple_of(c * l, l), l), :] = r[:l].astype(BF16)
        return carry

    lax.fori_loop(0, nc, body_b, 0)

    for hd in range(hb):
        z = z_ref[0, :, hd * d:(hd + 1) * d].astype(F32)
        o = oa_scr[hd] + od_scr[2 * hd] + od_scr[2 * hd + 1]
        o_ref[0, :, hd * d:(hd + 1) * d] = (_rms(o, nw_ref[...]) * (z * _sigmoid(z))).astype(BF16)


def _gdn(proj, conv_w, gcols, grows, norm_w):
    b, s, _ = proj.shape
    l = G_CHUNK
    assert 2 * l == LANES
    nc = s // l
    hb = G_HEADS_PER_STEP
    cu = min(G_CHUNKS_PER_ITER, nc)
    d = HEAD_DIM
    blk = lambda off: pl.BlockSpec((1, s, hb * d), lambda i, h: (i, 0, off * HEADS // hb + h))
    return pl.pallas_call(
        functools.partial(_gdn_kernel, l=l, nc=nc, hb=hb, cu=cu),
        grid=(b, HEADS // hb),
        in_specs=[blk(4), blk(5), blk(6), blk(7),
                  pl.BlockSpec((3, hb, 8, d), lambda i, h: (0, h, 0, 0)),
                  pl.BlockSpec((1, hb, s, 8), lambda i, h: (i, h, 0, 0)),
                  pl.BlockSpec((1, hb, nc, 8, 2 * l), lambda i, h: (i, h, 0, 0, 0)),
                  pl.BlockSpec((1, d), lambda i, h: (0, 0))],
        out_specs=pl.BlockSpec((1, s, hb * d), lambda i, h: (i, 0, h)),
        out_shape=jax.ShapeDtypeStruct((b, s, HEADS * d), BF16),
        scratch_shapes=[pltpu.VMEM((3, s + 16, d), F32),
                        pltpu.VMEM((hb, s, d), BF16), pltpu.VMEM((hb, s, d), BF16),
                        pltpu.VMEM((hb, s, d), BF16),
                        pltpu.VMEM((hb, nc, d, 2 * l), BF16),
                        pltpu.VMEM((2 * hb, nc, l + d, d), BF16),
                        pltpu.VMEM((2 * hb, nc, d, d), BF16),
                        pltpu.VMEM((hb, s, d), F32),
                        pltpu.VMEM((2 * hb, s, d), BF16),
                        pltpu.VMEM((2 * hb, d, d), F32)],
        compiler_params=_params("parallel", "parallel"),
        name="gdn",
    )(proj, proj, proj, proj, conv_w, gcols, grows, norm_w.reshape(1, d))


def _merge_kernel(x_ref, mod_ref, hm_ref, hg_ref, mm_ref, mg_ref, wm_ref, wg_ref, wo_ref, o_ref):
    gate = mod_ref[0, 5:6, :]
    y = (_sigmoid(mm_ref[0].astype(F32)) * _dot(hm_ref[0], wm_ref[...])
         + _sigmoid(mg_ref[0].astype(F32)) * _dot(hg_ref[0], wg_ref[...]))
    o_ref[0] = x_ref[0] + gate * _dot(y.astype(BF16), wo_ref[...])


def _merge(x, mod, h_m, h_g, proj, w_m, w_g, w_o):
    b, s, d = x.shape
    tm = min(512, s)
    width = HEADS * HEAD_DIM
    tok = lambda w, off: pl.BlockSpec((1, tm, w), lambda i, j: (i, j, off))
    wspec = lambda r: pl.BlockSpec((r, d), lambda i, j: (0, 0))
    merge_off = 8 * width // d
    return pl.pallas_call(
        _merge_kernel,
        grid=(b, s // tm),
        in_specs=[tok(d, 0),
                  pl.BlockSpec((1, 9, d), lambda i, j: (i, 0, 0)),
                  tok(width, 0), tok(width, 0),
                  tok(d, merge_off), tok(d, merge_off + 1),
                  wspec(width), wspec(width), wspec(d)],
        out_specs=tok(d, 0),
        out_shape=jax.ShapeDtypeStruct((b, s, d), F32),
        compiler_params=_params("parallel", "parallel"),
        name="merge",
    )(x, mod, h_m, h_g, proj, proj, w_m, w_g, w_o)


def kernel(x, c, w_ada, b_ada, norm_ffn1, w_ffn1_in, w_ffn1_out, norm_mix, w_in, mlstm_gate_bias,
           gdn_a_log, gdn_dt_bias, gdn_conv_w, mlstm_out_norm, gdn_out_norm, w_branch_mlstm,
           w_branch_gdn, w_out, norm_ffn2, w_ffn2_in, w_ffn2_out, norm_final):
    b, s, d = x.shape
    depth = w_ada.shape[0]
    width = HEADS * HEAD_DIM
    ng = 4 * HEADS
    sizes = (width,) * 4 + (ng,) + (width,) * 4 + (ng,) + (d, d)
    offs = [0]
    for sz in sizes:
        offs.append(offs[-1] + sz)
    for layer in range(depth):
        mod = _ada(c, w_ada[layer], b_ada[layer]).reshape(b, 9, d)
        x = _ffn(x, mod, norm_ffn1[layer], w_ffn1_in[layer], w_ffn1_out[layer], norm_final,
                 sub=0, final=False)

        wl = w_in[layer]
        col = lambda i: wl[:, offs[i]:offs[i + 1]]
        w_big = jnp.concatenate([col(i) for i in (0, 1, 2, 3, 5, 6, 7, 8, 10, 11)], axis=1).astype(BF16)
        w_gates = jnp.concatenate([col(4), col(9), jnp.zeros((d, LANES - 2 * ng), F32)],
                                  axis=1).astype(BF16)
        proj, gates_raw = _proj(x, mod, norm_mix[layer], w_big, w_gates)

        zeros_h = jnp.zeros((HEADS,), F32)
        bias_row = jnp.concatenate([mlstm_gate_bias[layer].reshape(ng),
                                    gdn_dt_bias[layer][0], zeros_h, gdn_dt_bias[layer][1], zeros_h,
                                    jnp.zeros((LANES - 2 * ng,), F32)])
        alog_row = jnp.concatenate([jnp.zeros((ng,), F32),
                                    gdn_a_log[layer][0], zeros_h, gdn_a_log[layer][1], zeros_h,
                                    jnp.zeros((LANES - 2 * ng,), F32)])
        gparams = jnp.zeros((8, LANES), F32).at[0].set(bias_row).at[1].set(alog_row)
        gates = _gate_prep(gates_raw, gparams)
        m_cols, m_rows = _per_head_gates(gates[:, :, :ng].reshape(b, s, 4, HEADS), M_CHUNK)
        g_gates = gates[:, :, ng:2 * ng].reshape(b, s, 4, HEADS)
        g_cols = _per_head_gates(g_gates, G_CHUNK)[0]
        g_rows = _pair_rows(g_gates, G_CHUNK)

        h_m = _mlstm(proj, m_cols, m_rows, mlstm_out_norm[layer])
        conv_w = gdn_conv_w[layer].reshape(CONV_K, 3, HEADS, HEAD_DIM).transpose(1, 2, 0, 3)
        conv_w = jnp.pad(conv_w, ((0, 0), (0, 0), (0, 8 - CONV_K), (0, 0)))
        h_g = _gdn(proj, conv_w, g_cols, g_rows, gdn_out_norm[layer])

        x = _merge(x, mod, h_m, h_g, proj, w_branch_mlstm[layer].astype(BF16),
                   w_branch_gdn[layer].astype(BF16), w_out[layer].astype(BF16))
        last = layer == depth - 1
        x = _ffn(x, mod, norm_ffn2[layer], w_ffn2_in[layer], w_ffn2_out[layer], norm_final,
                 sub=2, final=last)
    return x
```

```python
import functools

import jax
import jax.numpy as jnp
from jax import lax
from jax.experimental import pallas as pl
from jax.experimental.pallas import tpu as pltpu

F32 = jnp.float32
BF16 = jnp.bfloat16

EPS = 1e-6
FFN_RES = 0.5
HEADS = 8
HEAD_DIM = 128
CONV_K = 5
LANES = 128
BF16_SUBLANES = 16
M_CHUNK = 128
M_HEADS_PER_STEP = 2
M_CHUNKS_PER_ITER = 4
G_CHUNK = 64
G_HEADS_PER_STEP = 2
G_CHUNKS_PER_ITER = 8
VMEM_LIMIT = 56 * 1024 * 1024

M_X_F, M_C_F, M_X_B, M_C_B = 0, 1, 2, 3
G_C_F, G_X_F, G_C_B, G_X_B = 0, 1, 2, 3
M_GATE0, G_GATE0 = 0, 4


def _dot(a, b):
    return jnp.dot(a, b, preferred_element_type=F32)


def _dot_nt(a, b):
    return lax.dot_general(a, b, (((1,), (1,)), ((), ())), preferred_element_type=F32)


def _dot_tn(a, b):
    return lax.dot_general(a, b, (((0,), (0,)), ((), ())), preferred_element_type=F32)


def _rms(x, w):
    return (x * lax.rsqrt(jnp.mean(x * x, axis=-1, keepdims=True) + EPS)) * w


def _sigmoid(x):
    return 1.0 / (1.0 + jnp.exp(-x))


def _params(*sem):
    return pltpu.CompilerParams(dimension_semantics=sem, vmem_limit_bytes=VMEM_LIMIT)


def _ada_kernel(c_ref, w_ref, b_ref, o_ref):
    c = c_ref[...]
    cs = (c * _sigmoid(c)).astype(BF16)
    o_ref[...] = _dot(cs, w_ref[...].astype(BF16)) + b_ref[...]


def _ada(c, w_ada, b_ada):
    b, d = c.shape
    n = w_ada.shape[1]
    tn = 1024
    return pl.pallas_call(
        _ada_kernel,
        grid=(n // tn,),
        in_specs=[pl.BlockSpec((b, d), lambda j: (0, 0)),
                  pl.BlockSpec((d, tn), lambda j: (0, j)),
                  pl.BlockSpec((1, tn), lambda j: (0, j))],
        out_specs=pl.BlockSpec((b, tn), lambda j: (0, j)),
        out_shape=jax.ShapeDtypeStruct((b, n), F32),
        compiler_params=_params("parallel"),
        name="ada",
    )(c, w_ada, b_ada.reshape(1, n))


def _ffn_kernel(x_ref, mod_ref, nw_ref, wg_ref, wu_ref, wo_ref, nf_ref, o_ref, a_scr,
                *, sub, final, fc):
    x = x_ref[0]
    shift = mod_ref[0, 3 * sub + 0:3 * sub + 1, :]
    scale = mod_ref[0, 3 * sub + 1:3 * sub + 2, :]
    gate = mod_ref[0, 3 * sub + 2:3 * sub + 3, :]
    h = (_rms(x, nw_ref[...]) * (1.0 + scale) + shift).astype(BF16)
    f = wg_ref.shape[1]
    for c0 in range(0, f, fc):
        c1 = min(c0 + fc, f)
        g = _dot(h, wg_ref[:, c0:c1])
        u = _dot(h, wu_ref[:, c0:c1])
        a_scr[:, c0:c1] = (g * _sigmoid(g) * u).astype(BF16)
    y = x + FFN_RES * gate * _dot(a_scr[...], wo_ref[...])
    if final:
        y = _rms(y, nf_ref[...])
    o_ref[0] = y


def _ffn(x, mod, norm_w, w_in, w_out, norm_final, *, sub, final):
    b, s, d = x.shape
    f = w_out.shape[0]
    tm = min(512, s)
    wg = w_in[:, :f].astype(BF16)
    wu = w_in[:, f:].astype(BF16)
    wo = w_out.astype(BF16)
    const2 = lambda i, j: (0, 0)
    single = pl.Buffered(1)
    return pl.pallas_call(
        functools.partial(_ffn_kernel, sub=sub, final=final, fc=512),
        grid=(b, s // tm),
        in_specs=[pl.BlockSpec((1, tm, d), lambda i, j: (i, j, 0)),
                  pl.BlockSpec((1, 9, d), lambda i, j: (i, 0, 0)),
                  pl.BlockSpec((1, d), const2),
                  pl.BlockSpec((d, f), const2, pipeline_mode=single),
                  pl.BlockSpec((d, f), const2, pipeline_mode=single),
                  pl.BlockSpec((f, d), const2, pipeline_mode=single),
                  pl.BlockSpec((1, d), const2)],
        out_specs=pl.BlockSpec((1, tm, d), lambda i, j: (i, j, 0)),
        out_shape=jax.ShapeDtypeStruct((b, s, d), F32),
        scratch_shapes=[pltpu.VMEM((tm, f), BF16)],
        compiler_params=_params("parallel", "parallel"),
        name="ffn_final" if final else "ffn",
    )(x, mod, norm_w.reshape(1, d), wg, wu, wo, norm_final.reshape(1, d))


def _proj_kernel(x_ref, mod_ref, nw_ref, w_ref, wgate_ref, o_ref, gate_ref, h_scr):
    @pl.when(pl.program_id(2) == 0)
    def _():
        shift = mod_ref[0, 3:4, :]
        scale = mod_ref[0, 4:5, :]
        h = (_rms(x_ref[0], nw_ref[...]) * (1.0 + scale) + shift).astype(BF16)
        h_scr[...] = h
        gate_ref[0] = _dot(h, wgate_ref[...])

    o_ref[0] = _dot(h_scr[...], w_ref[...]).astype(BF16)


def _proj(x, mod, norm_w, w_big, w_gates):
    b, s, d = x.shape
    n = w_big.shape[1]
    tm, tn = min(1024, s), 2048
    return pl.pallas_call(
        _proj_kernel,
        grid=(b, s // tm, n // tn),
        in_specs=[pl.BlockSpec((1, tm, d), lambda i, j, k: (i, j, 0)),
                  pl.BlockSpec((1, 9, d), lambda i, j, k: (i, 0, 0)),
                  pl.BlockSpec((1, d), lambda i, j, k: (0, 0)),
                  pl.BlockSpec((d, tn), lambda i, j, k: (0, k)),
                  pl.BlockSpec((d, LANES), lambda i, j, k: (0, 0))],
        out_specs=[pl.BlockSpec((1, tm, tn), lambda i, j, k: (i, j, k)),
                   pl.BlockSpec((1, tm, LANES), lambda i, j, k: (i, j, 0))],
        out_shape=[jax.ShapeDtypeStruct((b, s, n), BF16),
                   jax.ShapeDtypeStruct((b, s, LANES), F32)],
        scratch_shapes=[pltpu.VMEM((tm, d), BF16)],
        compiler_params=_params("parallel", "parallel", "arbitrary"),
        name="proj",
    )(x, mod, norm_w.reshape(1, d), w_big, w_gates)


def _split_dot(tri, y):
    hi = y.astype(BF16)
    r1 = y - hi.astype(F32)
    mid = r1.astype(BF16)
    lo = (r1 - mid.astype(F32)).astype(BF16)
    return _dot(tri, hi) + _dot(tri, mid) + _dot(tri, lo)


def _gate_prep_kernel(g_ref, p_ref, cols_ref, rows_ref, y_scr, o_scr, *, lm, lg, lanes_per_group):
    s = g_ref.shape[1]
    lane = lax.broadcasted_iota(jnp.int32, (1, LANES), 1)
    j = lane & 7
    pre = g_ref[0] + p_ref[0:1, :]
    a_coef = -jnp.exp(p_ref[1:2, :])
    tail = jnp.log(1.0 + jnp.exp(-jnp.abs(pre)))
    sp = jnp.maximum(pre, 0.0) + tail
    log_sig = jnp.minimum(pre, 0.0) - tail
    is_logf = (j == 1) | (j == 3)
    is_a = (j == 4) | (j == 6)
    is_beta = (j == 5) | (j == 7)
    y = jnp.where(is_logf, log_sig, pre)
    y = jnp.where(is_a, a_coef * sp, y)
    y = jnp.where(is_beta, _sigmoid(pre), y)
    y_scr[...] = y
    is_prefix = (j == 1) | (j == 4)
    is_suffix = (j == 3) | (j == 6)

    def cumsum_pass(l, lanes_sel):
        t = lax.broadcasted_iota(jnp.int32, (l, l), 0)
        u = lax.broadcasted_iota(jnp.int32, (l, l), 1)
        tri = (t >= u).astype(BF16)

        per_iter = min(4, s // l)

        def body(it, carry):
            starts = [pl.multiple_of((it * per_iter + u) * l, l) for u in range(per_iter)]
            ycs = [y_scr[pl.ds(r0, l), :] for r0 in starts]
            pres = [_split_dot(tri, yc) for yc in ycs]
            for r0, yc, pre_c in zip(starts, ycs, pres):
                tot = pre_c[l - 1:l, :]
                suf_c = tot - pre_c + yc
                out = jnp.where(is_prefix, pre_c, jnp.where(is_suffix, suf_c, yc))
                cur = o_scr[pl.ds(r0, l), :]
                o_scr[pl.ds(r0, l), :] = jnp.where(lanes_sel, out, cur)
            return carry

        lax.fori_loop(0, s // (l * per_iter), body, 0)

    o_scr[...] = y
    cumsum_pass(lm, j < 4)
    cumsum_pass(lg, j >= 4)

    n_rows = rows_ref.shape[1]
    blk = min(s, 256)

    def emit(i, carry):
        r0 = pl.multiple_of(i * blk, blk)
        o = o_scr[pl.ds(r0, blk), :]
        for grp in range(cols_ref.shape[1]):
            shift = (LANES - lanes_per_group * grp) % LANES
            cols_ref[0, grp, pl.ds(r0, blk), :] = o if shift == 0 else pltpu.roll(o, shift, axis=1)
        for r in range(0, blk, LANES):
            rows_ref[0, :, pl.ds(r0 + r, LANES)] = o[r:r + LANES, :].T[:n_rows, :]
        return carry

    lax.fori_loop(0, s // blk, emit, 0)


def _gate_prep(gates_raw, params, groups):
    b, s, _ = gates_raw.shape
    n_rows = 8 * HEADS
    return pl.pallas_call(
        functools.partial(_gate_prep_kernel, lm=M_CHUNK, lg=G_CHUNK, lanes_per_group=n_rows // groups),
        grid=(b,),
        in_specs=[pl.BlockSpec((1, s, LANES), lambda i: (i, 0, 0)),
                  pl.BlockSpec((8, LANES), lambda i: (0, 0))],
        out_specs=[pl.BlockSpec((1, groups, s, LANES), lambda i: (i, 0, 0, 0)),
                   pl.BlockSpec((1, n_rows, s), lambda i: (i, 0, 0))],
        out_shape=[jax.ShapeDtypeStruct((b, groups, s, LANES), F32),
                   jax.ShapeDtypeStruct((b, n_rows, s), F32)],
        scratch_shapes=[pltpu.VMEM((s, LANES), F32), pltpu.VMEM((s, LANES), F32)],
        compiler_params=_params("parallel"),
        name="gate_prep",
    )(gates_raw, params)


def _chunk_rows(rows_t, first, l):
    b, _, _, s = rows_t.shape
    rows = rows_t[:, :, first:first + 4].reshape(b, HEADS, 4, s // l, l).transpose(0, 1, 3, 2, 4)
    return jnp.pad(rows, ((0, 0), (0, 0), (0, 0), (0, 4), (0, 0)))


def _mlstm_kernel(q_ref, k_ref, v_ref, og_ref, gcol_ref, grow_ref, nw_ref, o_ref,
                  qt_scr, vt_scr, h_scr, c_scr, m_scr, *, l, nc, hb, cu):
    d = HEAD_DIM
    da = vt_scr.shape[2]
    scale = d ** -0.5
    s_idx = lax.broadcasted_iota(jnp.int32, (l, l), 0)
    t_idx = lax.broadcasted_iota(jnp.int32, (l, l), 1)
    ones_rows = (lax.broadcasted_iota(jnp.int32, (da - d, l), 0) == 0).astype(BF16)

    for hd in range(hb):
        for c in range(nc):
            qt_scr[hd, c] = q_ref[0, c * l:(c + 1) * l, hd * d:(hd + 1) * d].astype(F32).T.astype(BF16)
            vt = v_ref[0, c * l:(c + 1) * l, hd * d:(hd + 1) * d].astype(F32).T.astype(BF16)
            vt_scr[hd, c] = jnp.concatenate([vt, ones_rows], axis=0)

    def gate_idx(dr):
        return (M_X_F, M_C_F) if dr == 0 else (M_X_B, M_C_B)

    def chunk_of(dr, step):
        return step if dr == 0 else nc - 1 - step

    def total_logf(b_row, dr):
        return b_row[:, l - 1:l] if dr == 0 else b_row[:, 0:1]

    row_id = lax.broadcasted_iota(jnp.int32, (8 * nc, 1), 0) & 7
    for hd in range(hb):
        rows_all = grow_ref[0, hd].reshape(8 * nc, l)
        b_up = pltpu.roll(rows_all, 8 * nc - 1, axis=0)
        g_col = jnp.where(row_id == M_X_F, b_up[:, l - 1:l], b_up[:, 0:1])
        a_max = jnp.max(g_col - b_up + rows_all, axis=-1, keepdims=True)
        g_rep = jnp.broadcast_to(g_col, (8 * nc, l))
        a_rep = jnp.broadcast_to(a_max, (8 * nc, l))
        for dr in range(2):
            xi, _ = gate_idx(dr)
            m = jnp.zeros((1, l), F32)
            for step in range(nc):
                c = chunk_of(dr, step)
                r = 8 * c + xi
                m_scr[2 * hd + dr, c] = jnp.broadcast_to(m, (8, l))
                m = jnp.maximum(g_rep[r:r + 1, :] + m, a_rep[r:r + 1, :])

    c_scr[...] = jnp.zeros_like(c_scr)

    def body(it, carry):
        probs = []
        for k_step in range(cu):
            for hd in range(hb):
                for dr in range(2):
                    probs.append(dict(hd=hd, dr=dr, c=chunk_of(dr, it * cu + k_step)))
        for pr in probs:
            hd, c = pr["hd"], pr["c"]
            pr["k"] = k_ref[0, pl.ds(pl.multiple_of(c * l, l), l), hd * d:(hd + 1) * d]
            pr["qt"] = qt_scr[hd, c]
            pr["st"] = _dot(pr["k"], pr["qt"])
        for pr in probs:
            hd, dr, c = pr["hd"], pr["dr"], pr["c"]
            xi, ci = gate_idx(dr)
            cols = gcol_ref[0, 0, pl.ds(pl.multiple_of(c * l, l), l), :]
            rows = grow_ref[0, hd, c]
            lane0 = 8 * hd + M_GATE0
            c_col = cols[:, lane0 + ci:lane0 + ci + 1] - cols[:, lane0 + xi:lane0 + xi + 1]
            i_row, b_row = rows[xi:xi + 1, :], rows[ci:ci + 1, :]
            m_prev = m_scr[2 * hd + dr, c][0:1, :]
            mask = (s_idx <= t_idx) if dr == 0 else (s_idx >= t_idx)
            dmat = jnp.where(mask, b_row - c_col, -jnp.inf)
            inter = b_row + m_prev
            m_t = jnp.maximum(inter, jnp.max(dmat, axis=0, keepdims=True))
            pst = (pr["st"] * scale * jnp.exp(dmat - m_t)).astype(BF16)
            vt = vt_scr[hd, c]
            pr["num"] = _dot(vt, pst)
            pr["w_inter"] = jnp.exp(inter - m_t) * scale
            pr["floor"] = jnp.exp(-m_t)
            g = total_logf(b_row, dr)
            m_new = jnp.maximum(g + m_prev, jnp.max(g - b_row + i_row, axis=-1, keepdims=True))
            pr["decay"] = jnp.exp(g + m_prev - m_new)
            vw = (vt.astype(F32) * jnp.exp(g - b_row + i_row - m_new)).astype(BF16)
            pr["x"] = _dot(vw, pr["k"])
        cmats = {}
        for pr in probs:
            key = 2 * pr["hd"] + pr["dr"]
            cmat = cmats[key] if key in cmats else c_scr[key]
            pr["qc"] = _dot(cmat.astype(BF16), pr["qt"])
            cmats[key] = pr["decay"] * cmat + pr["x"]
        for key, cmat in cmats.items():
            c_scr[key] = cmat
        for pr in probs:
            hd, dr, c = pr["hd"], pr["dr"], pr["c"]
            num_aug = pr["num"] + pr["w_inter"] * pr["qc"]
            den = num_aug[d:d + 1, :]
            h_scr[2 * hd + dr, c] = num_aug[:d, :] / jnp.maximum(jnp.abs(den), pr["floor"])
        return carry

    lax.fori_loop(0, nc // cu, body, 0)
    for hd in range(hb):
        for c in range(nc):
            ht = h_scr[2 * hd, c] + h_scr[2 * hd + 1, c]
            ht = ht * lax.rsqrt(jnp.mean(ht * ht, axis=0, keepdims=True) + EPS)
            og = og_ref[0, c * l:(c + 1) * l, hd * d:(hd + 1) * d].astype(F32)
            o_ref[0, c * l:(c + 1) * l, hd * d:(hd + 1) * d] = (
                ht.T * nw_ref[...] * _sigmoid(og)).astype(BF16)


def _mlstm(proj, gcols, grows, norm_w):
    b, s, _ = proj.shape
    l = M_CHUNK
    nc = s // l
    hb = M_HEADS_PER_STEP
    cu = min(M_CHUNKS_PER_ITER, nc)
    d = HEAD_DIM
    blk = lambda off: pl.BlockSpec((1, s, hb * d), lambda i, h: (i, 0, off * HEADS // hb + h))
    return pl.pallas_call(
        functools.partial(_mlstm_kernel, l=l, nc=nc, hb=hb, cu=cu),
        grid=(b, HEADS // hb),
        in_specs=[blk(0), blk(1), blk(2), blk(3),
                  pl.BlockSpec((1, 1, s, LANES), lambda i, h: (i, h, 0, 0)),
                  pl.BlockSpec((1, hb, nc, 8, l), lambda i, h: (i, h, 0, 0, 0)),
                  pl.BlockSpec((1, d), lambda i, h: (0, 0))],
        out_specs=pl.BlockSpec((1, s, hb * d), lambda i, h: (i, 0, h)),
        out_shape=jax.ShapeDtypeStruct((b, s, HEADS * d), BF16),
        scratch_shapes=[pltpu.VMEM((hb, nc, d, l), BF16),
                        pltpu.VMEM((hb, nc, d + BF16_SUBLANES, l), BF16),
                        pltpu.VMEM((2 * hb, nc, d, l), F32),
                        pltpu.VMEM((2 * hb, d + BF16_SUBLANES, d), F32),
                        pltpu.VMEM((2 * hb, nc, 8, LANES), F32)],
        compiler_params=_params("parallel", "parallel"),
        name="mlstm",
    )(proj, proj, proj, proj, gcols, grows, norm_w.reshape(1, d))


def _pair_rows(rows_t, first, l):
    b, _, _, s = rows_t.shape
    rows = rows_t[:, :, first:first + 4].reshape(b, HEADS, 2, 2, s // l, l)
    rows = rows.transpose(0, 1, 4, 3, 2, 5).reshape(b, HEADS, s // l, 2, 2 * l)
    return jnp.pad(rows, ((0, 0), (0, 0), (0, 0), (0, 6), (0, 0)))


def _gdn_kernel(q_ref, k_ref, v_ref, z_ref, cw_ref, gcol_ref, grow_ref, nw_ref, o_ref,
                pad_scr, q_scr, k_scr, v_scr, kt_scr, qm_scr, n_scr, oa_scr, od_scr, st_scr,
                *, l, nc, hb, cu):
    s = q_ref.shape[1]
    d = HEAD_DIM
    scale = d ** -0.5
    half = CONV_K // 2

    for which in range(3):
        pad_scr[which, 0:8, :] = jnp.zeros((8, d), F32)
        pad_scr[which, 8 + s:16 + s, :] = jnp.zeros((8, d), F32)

    rb = min(s, 256)
    srcs = (q_ref, k_ref, v_ref)
    dsts = (q_scr, k_scr, v_scr)

    for hd in range(hb):
        def fill(i, carry, hd=hd):
            r0 = pl.multiple_of(i * rb, rb)
            for which in range(3):
                pad_scr[which, pl.ds(8 + r0, rb), :] = (
                    srcs[which][0, pl.ds(r0, rb), hd * d:(hd + 1) * d].astype(F32))
            return carry

        lax.fori_loop(0, s // rb, fill, 0)

        def block(i, carry, hd=hd):
            r0 = pl.multiple_of(i * rb, rb)
            for which in range(3):
                acc = pad_scr[which, pl.ds(r0 + 8 - half, rb), :] * cw_ref[which, hd, 0:1, :]
                for j in range(1, CONV_K):
                    acc = acc + (pad_scr[which, pl.ds(r0 + 8 - half + j, rb), :]
                                 * cw_ref[which, hd, j:j + 1, :])
                y = acc * _sigmoid(acc)
                if which < 2:
                    y = y * lax.rsqrt(jnp.sum(y * y, axis=-1, keepdims=True) + EPS)
                dsts[which][hd, pl.ds(r0, rb), :] = y.astype(BF16)
                if which == 1:
                    for cc in range(rb // l):
                        yc = y[cc * l:(cc + 1) * l, :]
                        kt_scr[hd, i * (rb // l) + cc] = jnp.concatenate([yc, yc], axis=0).T.astype(BF16)
            return carry

        lax.fori_loop(0, s // rb, block, 0)

    t_idx = lax.broadcasted_iota(jnp.int32, (l, 2 * l), 0)
    lane = lax.broadcasted_iota(jnp.int32, (l, 2 * l), 1)
    is_f = lane < l
    s_idx = jnp.where(is_f, lane, lane - l)
    incl = jnp.where(is_f, t_idx - s_idx, s_idx - t_idx) >= 0
    diag = t_idx == s_idx
    eye2 = diag.astype(F32)
    zeros_b = jnp.zeros((l, d), BF16)

    def blockdiag(x2):
        return jnp.concatenate([jnp.where(is_f, x2, 0.0), jnp.where(is_f, 0.0, x2)], axis=0).astype(BF16)

    base = 8
    shift = base.bit_length() - 1
    same_base = (t_idx >> shift) == (s_idx >> shift)
    merge_masks = []
    size = base
    while size < l:
        sh = size.bit_length() - 1
        merge_masks.append(((t_idx >> (sh + 1)) == (s_idx >> (sh + 1))) & ((t_idx >> sh) != (s_idx >> sh)))
        size *= 2

    def body_a(it, carry):
        probs = [dict(c=it * cu + cc, hd=hd) for cc in range(cu) for hd in range(hb)]
        for pr in probs:
            c, hd = pr["c"], pr["hd"]
            r0 = pl.multiple_of(c * l, l)
            qb = q_scr[hd, pl.ds(r0, l), :]
            kb = k_scr[hd, pl.ds(r0, l), :]
            pr["kq"] = _dot_nt(jnp.concatenate([qb, kb], axis=0),
                               jnp.concatenate([kb, kb], axis=0))
        for pr in probs:
            c, hd = pr["c"], pr["hd"]
            cols = gcol_ref[0, 0, pl.ds(pl.multiple_of(c * l, l), l), :]
            col = lambda j: cols[:, 8 * hd + G_GATE0 + j:8 * hd + G_GATE0 + j + 1]
            gam_f, beta_f, gam_b, beta_b = col(G_C_F), col(G_X_F), col(G_C_B), col(G_X_B)
            gam_row = grow_ref[0, hd, c][0:1, :]
            e_incl = jnp.exp(jnp.where(incl, jnp.where(is_f, gam_f, gam_b) - gam_row, -jnp.inf))
            pr["attn2"] = pr["kq"][:l] * scale * e_incl
            a2 = jnp.where(is_f, beta_f, beta_b) * pr["kq"][l:] * jnp.where(diag, 0.0, e_incl)
            pr["a2"] = a2
            dblk = jnp.where(same_base, a2, 0.0)
            pr["p"] = eye2 - dblk
            pr["d2"] = _dot(dblk.astype(BF16), blockdiag(dblk))
        for pr in probs:
            res = _dot(jnp.concatenate([pr["p"], pr["d2"]], axis=0).astype(BF16), blockdiag(pr["d2"]))
            pr["p"] = pr["p"] + res[:l]
            pr["d4"] = res[l:]
        for pr in probs:
            pr["p"] = pr["p"] + _dot(pr["p"].astype(BF16), blockdiag(pr["d4"]))
        for mask in merge_masks:
            for pr in probs:
                pr["x"] = _dot(jnp.where(mask, pr["a2"], 0.0).astype(BF16), blockdiag(pr["p"]))
            for pr in probs:
                pr["p"] = pr["p"] - _dot(pr["p"].astype(BF16), blockdiag(pr["x"]))
        for pr in probs:
            c, hd = pr["c"], pr["hd"]
            r0 = pl.multiple_of(c * l, l)
            kb = k_scr[hd, pl.ds(r0, l), :]
            vb = v_scr[hd, pl.ds(r0, l), :]
            rows = grow_ref[0, hd, c]
            gam_row, beta_row = rows[0:1, :], rows[1:2, :]
            tk = (pr["p"] * (-beta_row * jnp.exp(gam_row))).astype(BF16)
            tv = (pr["p"] * beta_row).astype(BF16)
            pr["wk"] = _dot(tk, jnp.concatenate([jnp.concatenate([kb, zeros_b], axis=-1),
                                                 jnp.concatenate([zeros_b, kb], axis=-1)], axis=0))
            pr["wv"] = _dot(tv, jnp.concatenate([jnp.concatenate([vb, zeros_b], axis=-1),
                                                 jnp.concatenate([zeros_b, vb], axis=-1)], axis=0))
        for pr in probs:
            c, hd = pr["c"], pr["hd"]
            gam_row = grow_ref[0, hd, c][0:1, :]
            g_last = jnp.where(is_f[0:1, :], gam_row[:, l - 1:l], gam_row[:, l:l + 1])
            wk, wv = pr["wk"].astype(BF16), pr["wv"].astype(BF16)
            wu_bd = jnp.concatenate(
                [jnp.concatenate([wk[:, :d], wv[:, :d], zeros_b, zeros_b], axis=-1),
                 jnp.concatenate([zeros_b, zeros_b, wk[:, d:], wv[:, d:]], axis=-1)], axis=0)
            k_dec_t = (kt_scr[hd, c].astype(F32) * jnp.exp(g_last - gam_row)).astype(BF16)
            pr["res"] = _dot(jnp.concatenate([pr["attn2"].astype(BF16), k_dec_t], axis=0), wu_bd)
        for pr in probs:
            c, hd = pr["c"], pr["hd"]
            r0 = pl.multiple_of(c * l, l)
            q = q_scr[hd, pl.ds(r0, l), :].astype(F32)
            cols = gcol_ref[0, 0, pl.ds(r0, l), :]
            col = lambda j: cols[:, 8 * hd + G_GATE0 + j:8 * hd + G_GATE0 + j + 1]
            gam_f, gam_b = col(G_C_F), col(G_C_B)
            res = pr["res"]
            qd_f = q * (jnp.exp(gam_f) * scale) + res[:l, 0:d]
            qd_b = q * (jnp.exp(gam_b) * scale) + res[:l, 2 * d:3 * d]
            qm_scr[2 * hd, c] = jnp.concatenate([res[l:, 0:d], qd_f], axis=0).astype(BF16)
            qm_scr[2 * hd + 1, c] = jnp.concatenate([res[l:, 2 * d:3 * d], qd_b], axis=0).astype(BF16)
            n_scr[2 * hd, c] = res[l:, d:2 * d].astype(BF16)
            n_scr[2 * hd + 1, c] = res[l:, 3 * d:].astype(BF16)
            oa_scr[hd, pl.ds(r0, l), :] = res[:l, d:2 * d] + res[:l, 3 * d:]
        return carry

    lax.fori_loop(0, nc // cu, body_a, 0)

    st_scr[...] = jnp.zeros_like(st_scr)

    def body_b(j, carry):
        chains = [(hd, dr, j if dr == 0 else nc - 1 - j) for hd in range(hb) for dr in range(2)]
        states = [st_scr[2 * hd + dr] for hd, dr, _ in chains]
        rs = [_dot(qm_scr[2 * hd + dr, c], st.astype(BF16)) for (hd, dr, c), st in zip(chains, states)]
        for (hd, dr, c), state, r in zip(chains, states, rs):
            idx = 2 * hd + dr
            gam_row = grow_ref[0, hd, c][0:1, :]
            g_last = gam_row[:, l - 1:l] if dr == 0 else gam_row[:, l:l + 1]
            st_scr[idx] = jnp.exp(g_last) * state + r[:d] + n_scr[idx, c]
            od_scr[idx, pl.ds(pl.multiple_of(c * l, l), l), :] = r[d:].astype(BF16)
        return carry

    lax.fori_loop(0, nc, body_b, 0)

    for hd in range(hb):
        z = z_ref[0, :, hd * d:(hd + 1) * d].astype(F32)
        o = oa_scr[hd] + od_scr[2 * hd] + od_scr[2 * hd + 1]
        o_ref[0, :, hd * d:(hd + 1) * d] = (_rms(o, nw_ref[...]) * (z * _sigmoid(z))).astype(BF16)


def _gdn(proj, conv_w, gcols, grows, norm_w):
    b, s, _ = proj.shape
    l = G_CHUNK
    assert 2 * l == LANES
    nc = s // l
    hb = G_HEADS_PER_STEP
    cu = min(G_CHUNKS_PER_ITER, nc)
    d = HEAD_DIM
    blk = lambda off: pl.BlockSpec((1, s, hb * d), lambda i, h: (i, 0, off * HEADS // hb + h))
    return pl.pallas_call(
        functools.partial(_gdn_kernel, l=l, nc=nc, hb=hb, cu=cu),
        grid=(b, HEADS // hb),
        in_specs=[blk(4), blk(5), blk(6), blk(7),
                  pl.BlockSpec((3, hb, 8, d), lambda i, h: (0, h, 0, 0)),
                  pl.BlockSpec((1, 1, s, LANES), lambda i, h: (i, h, 0, 0)),
                  pl.BlockSpec((1, hb, nc, 8, 2 * l), lambda i, h: (i, h, 0, 0, 0)),
                  pl.BlockSpec((1, d), lambda i, h: (0, 0))],
        out_specs=pl.BlockSpec((1, s, hb * d), lambda i, h: (i, 0, h)),
        out_shape=jax.ShapeDtypeStruct((b, s, HEADS * d), BF16),
        scratch_shapes=[pltpu.VMEM((3, s + 16, d), F32),
                        pltpu.VMEM((hb, s, d), BF16), pltpu.VMEM((hb, s, d), BF16),
                        pltpu.VMEM((hb, s, d), BF16),
                        pltpu.VMEM((hb, nc, d, 2 * l), BF16),
                        pltpu.VMEM((2 * hb, nc, l + d, d), BF16),
                        pltpu.VMEM((2 * hb, nc, d, d), BF16),
                        pltpu.VMEM((hb, s, d), F32),
                        pltpu.VMEM((2 * hb, s, d), BF16),
                        pltpu.VMEM((2 * hb, d, d), F32)],
        compiler_params=_params("parallel", "parallel"),
        name="gdn",
    )(proj, proj, proj, proj, conv_w, gcols, grows, norm_w.reshape(1, d))


def _merge_kernel(x_ref, mod_ref, hm_ref, hg_ref, mm_ref, mg_ref, wm_ref, wg_ref, wo_ref, o_ref):
    gate = mod_ref[0, 5:6, :]
    y = (_sigmoid(mm_ref[0].astype(F32)) * _dot(hm_ref[0], wm_ref[...])
         + _sigmoid(mg_ref[0].astype(F32)) * _dot(hg_ref[0], wg_ref[...]))
    o_ref[0] = x_ref[0] + gate * _dot(y.astype(BF16), wo_ref[...])


def _merge(x, mod, h_m, h_g, proj, w_m, w_g, w_o):
    b, s, d = x.shape
    tm = min(512, s)
    width = HEADS * HEAD_DIM
    tok = lambda w, off: pl.BlockSpec((1, tm, w), lambda i, j: (i, j, off))
    wspec = lambda r: pl.BlockSpec((r, d), lambda i, j: (0, 0))
    merge_off = 8 * width // d
    return pl.pallas_call(
        _merge_kernel,
        grid=(b, s // tm),
        in_specs=[tok(d, 0),
                  pl.BlockSpec((1, 9, d), lambda i, j: (i, 0, 0)),
                  tok(width, 0), tok(width, 0),
                  tok(d, merge_off), tok(d, merge_off + 1),
                  wspec(width), wspec(width), wspec(d)],
        out_specs=tok(d, 0),
        out_shape=jax.ShapeDtypeStruct((b, s, d), F32),
        compiler_params=_params("parallel", "parallel"),
        name="merge",
    )(x, mod, h_m, h_g, proj, proj, w_m, w_g, w_o)


def kernel(x, c, w_ada, b_ada, norm_ffn1, w_ffn1_in, w_ffn1_out, norm_mix, w_in, mlstm_gate_bias,
           gdn_a_log, gdn_dt_bias, gdn_conv_w, mlstm_out_norm, gdn_out_norm, w_branch_mlstm,
           w_branch_gdn, w_out, norm_ffn2, w_ffn2_in, w_ffn2_out, norm_final):
    b, s, d = x.shape
    depth = w_ada.shape[0]
    width = HEADS * HEAD_DIM
    ng = 4 * HEADS
    sizes = (width,) * 4 + (ng,) + (width,) * 4 + (ng,) + (d, d)
    offs = [0]
    for sz in sizes:
        offs.append(offs[-1] + sz)
    for layer in range(depth):
        mod = _ada(c, w_ada[layer], b_ada[layer]).reshape(b, 9, d)
        x = _ffn(x, mod, norm_ffn1[layer], w_ffn1_in[layer], w_ffn1_out[layer], norm_final,
                 sub=0, final=False)

        wl = w_in[layer]
        col = lambda i: wl[:, offs[i]:offs[i + 1]]
        w_big = jnp.concatenate([col(i) for i in (0, 1, 2, 3, 5, 6, 7, 8, 10, 11)], axis=1).astype(BF16)
        head_major = lambda t: t.reshape(t.shape[:-1] + (4, HEADS)).swapaxes(-1, -2)
        pad_lanes = lambda t: jnp.pad(t.reshape(t.shape[:-2] + (8 * HEADS,)),
                                      [(0, 0)] * (t.ndim - 2) + [(0, LANES - 8 * HEADS)])
        w_gates = pad_lanes(jnp.concatenate([head_major(col(4)), head_major(col(9))], axis=-1)).astype(BF16)
        proj, gates_raw = _proj(x, mod, norm_mix[layer], w_big, w_gates)

        zeros_h = jnp.zeros((HEADS,), F32)
        dt_rows = jnp.stack([gdn_dt_bias[layer][0], zeros_h, gdn_dt_bias[layer][1], zeros_h]).reshape(ng)
        alog_rows = jnp.stack([gdn_a_log[layer][0], zeros_h, gdn_a_log[layer][1], zeros_h]).reshape(ng)
        bias_row = pad_lanes(jnp.concatenate([head_major(mlstm_gate_bias[layer].reshape(ng)),
                                              head_major(dt_rows)], axis=-1))
        alog_row = pad_lanes(jnp.concatenate([jnp.zeros((HEADS, 4), F32), head_major(alog_rows)], axis=-1))
        gparams = jnp.zeros((8, LANES), F32).at[0].set(bias_row).at[1].set(alog_row)
        assert M_HEADS_PER_STEP == G_HEADS_PER_STEP
        gate_cols, gate_rows = _gate_prep(gates_raw, gparams, HEADS // M_HEADS_PER_STEP)
        gate_rows = gate_rows.reshape(b, HEADS, 8, s)
        m_rows = _chunk_rows(gate_rows, 0, M_CHUNK)
        g_rows = _pair_rows(gate_rows, 4, G_CHUNK)

        h_m = _mlstm(proj, gate_cols, m_rows, mlstm_out_norm[layer])
        conv_w = gdn_conv_w[layer].reshape(CONV_K, 3, HEADS, HEAD_DIM).transpose(1, 2, 0, 3)
        conv_w = jnp.pad(conv_w, ((0, 0), (0, 0), (0, 8 - CONV_K), (0, 0)))
        h_g = _gdn(proj, conv_w, gate_cols, g_rows, gdn_out_norm[layer])

        x = _merge(x, mod, h_m, h_g, proj, w_branch_mlstm[layer].astype(BF16),
                   w_branch_gdn[layer].astype(BF16), w_out[layer].astype(BF16))
        last = layer == depth - 1
        x = _ffn(x, mod, norm_ffn2[layer], w_ffn2_in[layer], w_ffn2_out[layer], norm_final,
                 sub=2, final=last)
    return x
```

```python
import functools

import jax
import jax.numpy as jnp
from jax import lax
from jax.experimental import pallas as pl
from jax.experimental.pallas import tpu as pltpu

F32 = jnp.float32
BF16 = jnp.bfloat16

EPS = 1e-6
FFN_RES = 0.5
HEADS = 8
HEAD_DIM = 128
CONV_K = 5
LANES = 128
BF16_SUBLANES = 16
M_CHUNK = 128
M_HEADS_PER_STEP = 2
M_CHUNKS_PER_ITER = 4
G_CHUNK = 64
G_HEADS_PER_STEP = 2
G_CHUNKS_PER_ITER = 8
VMEM_LIMIT = 56 * 1024 * 1024

M_X_F, M_C_F, M_X_B, M_C_B = 0, 1, 2, 3
G_C_F, G_X_F, G_C_B, G_X_B = 0, 1, 2, 3
M_GATE0, G_GATE0 = 0, 4


def _dot(a, b):
    return jnp.dot(a, b, preferred_element_type=F32)


def _dot_nt(a, b):
    return lax.dot_general(a, b, (((1,), (1,)), ((), ())), preferred_element_type=F32)


def _dot_tn(a, b):
    return lax.dot_general(a, b, (((0,), (0,)), ((), ())), preferred_element_type=F32)


def _rms(x, w):
    return (x * lax.rsqrt(jnp.mean(x * x, axis=-1, keepdims=True) + EPS)) * w


def _sigmoid(x):
    return 1.0 / (1.0 + jnp.exp(-x))


def _params(*sem):
    return pltpu.CompilerParams(dimension_semantics=sem, vmem_limit_bytes=VMEM_LIMIT)


def _ada_kernel(c_ref, w_ref, b_ref, o_ref):
    c = c_ref[...]
    cs = (c * _sigmoid(c)).astype(BF16)
    o_ref[...] = _dot(cs, w_ref[...].astype(BF16)) + b_ref[...]


def _ada(c, w_ada, b_ada):
    b, d = c.shape
    n = w_ada.shape[1]
    tn = 1024
    return pl.pallas_call(
        _ada_kernel,
        grid=(n // tn,),
        in_specs=[pl.BlockSpec((b, d), lambda j: (0, 0)),
                  pl.BlockSpec((d, tn), lambda j: (0, j)),
                  pl.BlockSpec((1, tn), lambda j: (0, j))],
        out_specs=pl.BlockSpec((b, tn), lambda j: (0, j)),
        out_shape=jax.ShapeDtypeStruct((b, n), F32),
        compiler_params=_params("parallel"),
        name="ada",
    )(c, w_ada, b_ada.reshape(1, n))


def _ffn_kernel(x_ref, mod_ref, nw_ref, wg_ref, wu_ref, wo_ref, nf_ref, o_ref, a_scr,
                *, sub, final, fc):
    x = x_ref[0]
    shift = mod_ref[0, 3 * sub + 0:3 * sub + 1, :]
    scale = mod_ref[0, 3 * sub + 1:3 * sub + 2, :]
    gate = mod_ref[0, 3 * sub + 2:3 * sub + 3, :]
    h = (_rms(x, nw_ref[...]) * (1.0 + scale) + shift).astype(BF16)
    f = wg_ref.shape[1]
    for c0 in range(0, f, fc):
        c1 = min(c0 + fc, f)
        g = _dot(h, wg_ref[:, c0:c1])
        u = _dot(h, wu_ref[:, c0:c1])
        a_scr[:, c0:c1] = (g * _sigmoid(g) * u).astype(BF16)
    y = x + FFN_RES * gate * _dot(a_scr[...], wo_ref[...])
    if final:
        y = _rms(y, nf_ref[...])
    o_ref[0] = y


def _ffn(x, mod, norm_w, w_in, w_out, norm_final, *, sub, final):
    b, s, d = x.shape
    f = w_out.shape[0]
    tm = min(512, s)
    w_in = w_in.astype(BF16)
    wo = w_out.astype(BF16)
    const2 = lambda i, j: (0, 0)
    single = pl.Buffered(1)
    return pl.pallas_call(
        functools.partial(_ffn_kernel, sub=sub, final=final, fc=512),
        grid=(b, s // tm),
        in_specs=[pl.BlockSpec((1, tm, d), lambda i, j: (i, j, 0)),
                  pl.BlockSpec((1, 9, d), lambda i, j: (i, 0, 0)),
                  pl.BlockSpec((1, d), const2),
                  pl.BlockSpec((d, f), const2, pipeline_mode=single),
                  pl.BlockSpec((d, f), lambda i, j: (0, 1), pipeline_mode=single),
                  pl.BlockSpec((f, d), const2, pipeline_mode=single),
                  pl.BlockSpec((1, d), const2)],
        out_specs=pl.BlockSpec((1, tm, d), lambda i, j: (i, j, 0)),
        out_shape=jax.ShapeDtypeStruct((b, s, d), F32),
        scratch_shapes=[pltpu.VMEM((tm, f), BF16)],
        compiler_params=_params("parallel", "parallel"),
        name="ffn_final" if final else "ffn",
    )(x, mod, norm_w.reshape(1, d), w_in, w_in, wo, norm_final.reshape(1, d))


def _proj_kernel(x_ref, mod_ref, nw_ref, w_ref, wgate_ref, o_ref, gate_ref, h_scr):
    @pl.when(pl.program_id(2) == 0)
    def _():
        shift = mod_ref[0, 3:4, :]
        scale = mod_ref[0, 4:5, :]
        h = (_rms(x_ref[0], nw_ref[...]) * (1.0 + scale) + shift).astype(BF16)
        h_scr[...] = h
        gate_ref[0] = _dot(h, wgate_ref[...])

    o_ref[0] = _dot(h_scr[...], w_ref[...]).astype(BF16)


def _proj(x, mod, norm_w, w_big, w_gates):
    b, s, d = x.shape
    n = w_big.shape[1]
    tm, tn = min(1024, s), 2048
    return pl.pallas_call(
        _proj_kernel,
        grid=(b, s // tm, n // tn),
        in_specs=[pl.BlockSpec((1, tm, d), lambda i, j, k: (i, j, 0)),
                  pl.BlockSpec((1, 9, d), lambda i, j, k: (i, 0, 0)),
                  pl.BlockSpec((1, d), lambda i, j, k: (0, 0)),
                  pl.BlockSpec((d, tn), lambda i, j, k: (0, k)),
                  pl.BlockSpec((d, LANES), lambda i, j, k: (0, 0))],
        out_specs=[pl.BlockSpec((1, tm, tn), lambda i, j, k: (i, j, k)),
                   pl.BlockSpec((1, tm, LANES), lambda i, j, k: (i, j, 0))],
        out_shape=[jax.ShapeDtypeStruct((b, s, n), BF16),
                   jax.ShapeDtypeStruct((b, s, LANES), F32)],
        scratch_shapes=[pltpu.VMEM((tm, d), BF16)],
        compiler_params=_params("parallel", "parallel", "arbitrary"),
        name="proj",
    )(x, mod, norm_w.reshape(1, d), w_big, w_gates)


def _split_dot(tri, y):
    hi = y.astype(BF16)
    r1 = y - hi.astype(F32)
    mid = r1.astype(BF16)
    lo = (r1 - mid.astype(F32)).astype(BF16)
    return _dot(tri, hi) + _dot(tri, mid) + _dot(tri, lo)


def _gate_prep_kernel(g_ref, p_ref, cols_ref, rows_ref, y_scr, o_scr, *, lm, lg, lanes_per_group):
    s = g_ref.shape[1]
    lane = lax.broadcasted_iota(jnp.int32, (1, LANES), 1)
    j = lane & 7
    pre = g_ref[0] + p_ref[0:1, :]
    a_coef = -jnp.exp(p_ref[1:2, :])
    tail = jnp.log(1.0 + jnp.exp(-jnp.abs(pre)))
    sp = jnp.maximum(pre, 0.0) + tail
    log_sig = jnp.minimum(pre, 0.0) - tail
    is_logf = (j == 1) | (j == 3)
    is_a = (j == 4) | (j == 6)
    is_beta = (j == 5) | (j == 7)
    y = jnp.where(is_logf, log_sig, pre)
    y = jnp.where(is_a, a_coef * sp, y)
    y = jnp.where(is_beta, _sigmoid(pre), y)
    y_scr[...] = y
    is_prefix = (j == 1) | (j == 4)
    is_suffix = (j == 3) | (j == 6)

    def cumsum_pass(l, lanes_sel):
        t = lax.broadcasted_iota(jnp.int32, (l, l), 0)
        u = lax.broadcasted_iota(jnp.int32, (l, l), 1)
        tri = (t >= u).astype(BF16)

        per_iter = min(4, s // l)

        def body(it, carry):
            starts = [pl.multiple_of((it * per_iter + u) * l, l) for u in range(per_iter)]
            ycs = [y_scr[pl.ds(r0, l), :] for r0 in starts]
            pres = [_split_dot(tri, yc) for yc in ycs]
            for r0, yc, pre_c in zip(starts, ycs, pres):
                tot = pre_c[l - 1:l, :]
                suf_c = tot - pre_c + yc
                out = jnp.where(is_prefix, pre_c, jnp.where(is_suffix, suf_c, yc))
                cur = o_scr[pl.ds(r0, l), :]
                o_scr[pl.ds(r0, l), :] = jnp.where(lanes_sel, out, cur)
            return carry

        lax.fori_loop(0, s // (l * per_iter), body, 0)

    o_scr[...] = y
    cumsum_pass(lm, j < 4)
    cumsum_pass(lg, j >= 4)

    n_rows = rows_ref.shape[1]
    blk = min(s, 256)

    def emit(i, carry):
        r0 = pl.multiple_of(i * blk, blk)
        o = o_scr[pl.ds(r0, blk), :]
        for grp in range(cols_ref.shape[1]):
            shift = (LANES - lanes_per_group * grp) % LANES
            cols_ref[0, grp, pl.ds(r0, blk), :] = o if shift == 0 else pltpu.roll(o, shift, axis=1)
        for r in range(0, blk, LANES):
            rows_ref[0, :, pl.ds(r0 + r, LANES)] = o[r:r + LANES, :].T[:n_rows, :]
        return carry

    lax.fori_loop(0, s // blk, emit, 0)


def _gate_prep(gates_raw, params, groups):
    b, s, _ = gates_raw.shape
    n_rows = 8 * HEADS
    return pl.pallas_call(
        functools.partial(_gate_prep_kernel, lm=M_CHUNK, lg=G_CHUNK, lanes_per_group=n_rows // groups),
        grid=(b,),
        in_specs=[pl.BlockSpec((1, s, LANES), lambda i: (i, 0, 0)),
                  pl.BlockSpec((8, LANES), lambda i: (0, 0))],
        out_specs=[pl.BlockSpec((1, groups, s, LANES), lambda i: (i, 0, 0, 0)),
                   pl.BlockSpec((1, n_rows, s), lambda i: (i, 0, 0))],
        out_shape=[jax.ShapeDtypeStruct((b, groups, s, LANES), F32),
                   jax.ShapeDtypeStruct((b, n_rows, s), F32)],
        scratch_shapes=[pltpu.VMEM((s, LANES), F32), pltpu.VMEM((s, LANES), F32)],
        compiler_params=_params("parallel"),
        name="gate_prep",
    )(gates_raw, params)


def _chunk_rows(rows_t, first, l):
    b, _, _, s = rows_t.shape
    rows = rows_t[:, :, first:first + 4].reshape(b, HEADS, 4, s // l, l).transpose(0, 1, 3, 2, 4)
    return jnp.pad(rows, ((0, 0), (0, 0), (0, 0), (0, 4), (0, 0)))


def _mlstm_kernel(q_ref, k_ref, v_ref, og_ref, gcol_ref, grow_ref, nw_ref, o_ref,
                  qt_scr, vt_scr, h_scr, c_scr, m_scr, *, l, nc, hb, cu):
    d = HEAD_DIM
    da = vt_scr.shape[2]
    scale = d ** -0.5
    s_idx = lax.broadcasted_iota(jnp.int32, (l, l), 0)
    t_idx = lax.broadcasted_iota(jnp.int32, (l, l), 1)
    ones_rows = (lax.broadcasted_iota(jnp.int32, (da - d, l), 0) == 0).astype(BF16)

    for hd in range(hb):
        for c in range(nc):
            qt_scr[hd, c] = q_ref[0, c * l:(c + 1) * l, hd * d:(hd + 1) * d].astype(F32).T.astype(BF16)
            vt = v_ref[0, c * l:(c + 1) * l, hd * d:(hd + 1) * d].astype(F32).T.astype(BF16)
            vt_scr[hd, c] = jnp.concatenate([vt, ones_rows], axis=0)

    def gate_idx(dr):
        return (M_X_F, M_C_F) if dr == 0 else (M_X_B, M_C_B)

    def chunk_of(dr, step):
        return step if dr == 0 else nc - 1 - step

    def total_logf(b_row, dr):
        return b_row[:, l - 1:l] if dr == 0 else b_row[:, 0:1]

    row_id = lax.broadcasted_iota(jnp.int32, (8 * nc, 1), 0) & 7
    for hd in range(hb):
        rows_all = grow_ref[0, hd].reshape(8 * nc, l)
        b_up = pltpu.roll(rows_all, 8 * nc - 1, axis=0)
        g_col = jnp.where(row_id == M_X_F, b_up[:, l - 1:l], b_up[:, 0:1])
        a_max = jnp.max(g_col - b_up + rows_all, axis=-1, keepdims=True)
        g_rep = jnp.broadcast_to(g_col, (8 * nc, l))
        a_rep = jnp.broadcast_to(a_max, (8 * nc, l))
        for dr in range(2):
            xi, _ = gate_idx(dr)
            m = jnp.zeros((1, l), F32)
            for step in range(nc):
                c = chunk_of(dr, step)
                r = 8 * c + xi
                m_scr[2 * hd + dr, c] = jnp.broadcast_to(m, (8, l))
                m = jnp.maximum(g_rep[r:r + 1, :] + m, a_rep[r:r + 1, :])

    c_scr[...] = jnp.zeros_like(c_scr)

    def body(it, carry):
        probs = []
        for k_step in range(cu):
            for hd in range(hb):
                for dr in range(2):
                    probs.append(dict(hd=hd, dr=dr, c=chunk_of(dr, it * cu + k_step)))
        for pr in probs:
            hd, c = pr["hd"], pr["c"]
            pr["k"] = k_ref[0, pl.ds(pl.multiple_of(c * l, l), l), hd * d:(hd + 1) * d]
            pr["qt"] = qt_scr[hd, c]
            pr["st"] = _dot(pr["k"], pr["qt"])
        for pr in probs:
            hd, dr, c = pr["hd"], pr["dr"], pr["c"]
            xi, ci = gate_idx(dr)
            cols = gcol_ref[0, 0, pl.ds(pl.multiple_of(c * l, l), l), :]
            rows = grow_ref[0, hd, c]
            lane0 = 8 * hd + M_GATE0
            c_col = cols[:, lane0 + ci:lane0 + ci + 1] - cols[:, lane0 + xi:lane0 + xi + 1]
            i_row, b_row = rows[xi:xi + 1, :], rows[ci:ci + 1, :]
            m_prev = m_scr[2 * hd + dr, c][0:1, :]
            mask = (s_idx <= t_idx) if dr == 0 else (s_idx >= t_idx)
            dmat = jnp.where(mask, b_row - c_col, -jnp.inf)
            inter = b_row + m_prev
            m_t = jnp.maximum(inter, jnp.max(dmat, axis=0, keepdims=True))
            pst = (pr["st"] * scale * jnp.exp(dmat - m_t)).astype(BF16)
            vt = vt_scr[hd, c]
            pr["num"] = _dot(vt, pst)
            pr["w_inter"] = jnp.exp(inter - m_t) * scale
            pr["floor"] = jnp.exp(-m_t)
            g = total_logf(b_row, dr)
            m_new = jnp.maximum(g + m_prev, jnp.max(g - b_row + i_row, axis=-1, keepdims=True))
            pr["decay"] = jnp.exp(g + m_prev - m_new)
            vw = (vt.astype(F32) * jnp.exp(g - b_row + i_row - m_new)).astype(BF16)
            pr["x"] = _dot(vw, pr["k"])
        cmats = {}
        for pr in probs:
            key = 2 * pr["hd"] + pr["dr"]
            cmat = cmats[key] if key in cmats else c_scr[key]
            pr["qc"] = _dot(cmat.astype(BF16), pr["qt"])
            cmats[key] = pr["decay"] * cmat + pr["x"]
        for key, cmat in cmats.items():
            c_scr[key] = cmat
        for pr in probs:
            hd, dr, c = pr["hd"], pr["dr"], pr["c"]
            num_aug = pr["num"] + pr["w_inter"] * pr["qc"]
            den = num_aug[d:d + 1, :]
            h_scr[2 * hd + dr, c] = num_aug[:d, :] / jnp.maximum(jnp.abs(den), pr["floor"])
        return carry

    lax.fori_loop(0, nc // cu, body, 0)
    for hd in range(hb):
        for c in range(nc):
            ht = h_scr[2 * hd, c] + h_scr[2 * hd + 1, c]
            ht = ht * lax.rsqrt(jnp.mean(ht * ht, axis=0, keepdims=True) + EPS)
            og = og_ref[0, c * l:(c + 1) * l, hd * d:(hd + 1) * d].astype(F32)
            o_ref[0, c * l:(c + 1) * l, hd * d:(hd + 1) * d] = (
                ht.T * nw_ref[...] * _sigmoid(og)).astype(BF16)


def _mlstm(proj, gcols, grows, norm_w):
    b, s, _ = proj.shape
    l = M_CHUNK
    nc = s // l
    hb = M_HEADS_PER_STEP
    cu = min(M_CHUNKS_PER_ITER, nc)
    d = HEAD_DIM
    blk = lambda off: pl.BlockSpec((1, s, hb * d), lambda i, h: (i, 0, off * HEADS // hb + h))
    return pl.pallas_call(
        functools.partial(_mlstm_kernel, l=l, nc=nc, hb=hb, cu=cu),
        grid=(b, HEADS // hb),
        in_specs=[blk(0), blk(1), blk(2), blk(3),
                  pl.BlockSpec((1, 1, s, LANES), lambda i, h: (i, h, 0, 0)),
                  pl.BlockSpec((1, hb, nc, 8, l), lambda i, h: (i, h, 0, 0, 0)),
                  pl.BlockSpec((1, d), lambda i, h: (0, 0))],
        out_specs=pl.BlockSpec((1, s, hb * d), lambda i, h: (i, 0, h)),
        out_shape=jax.ShapeDtypeStruct((b, s, HEADS * d), BF16),
        scratch_shapes=[pltpu.VMEM((hb, nc, d, l), BF16),
                        pltpu.VMEM((hb, nc, d + BF16_SUBLANES, l), BF16),
                        pltpu.VMEM((2 * hb, nc, d, l), F32),
                        pltpu.VMEM((2 * hb, d + BF16_SUBLANES, d), F32),
                        pltpu.VMEM((2 * hb, nc, 8, LANES), F32)],
        compiler_params=_params("parallel", "parallel"),
        name="mlstm",
    )(proj, proj, proj, proj, gcols, grows, norm_w.reshape(1, d))


def _pair_rows(rows_t, first, l):
    b, _, _, s = rows_t.shape
    rows = rows_t[:, :, first:first + 4].reshape(b, HEADS, 2, 2, s // l, l)
    rows = rows.transpose(0, 1, 4, 3, 2, 5).reshape(b, HEADS, s // l, 2, 2 * l)
    return jnp.pad(rows, ((0, 0), (0, 0), (0, 0), (0, 6), (0, 0)))


def _gdn_kernel(q_ref, k_ref, v_ref, z_ref, cw_ref, gcol_ref, grow_ref, nw_ref, o_ref,
                pad_scr, q_scr, k_scr, v_scr, kt_scr, qm_scr, n_scr, oa_scr, od_scr, st_scr,
                *, l, nc, hb, cu):
    s = q_ref.shape[1]
    d = HEAD_DIM
    scale = d ** -0.5
    half = CONV_K // 2

    for which in range(3):
        pad_scr[which, 0:8, :] = jnp.zeros((8, d), F32)
        pad_scr[which, 8 + s:16 + s, :] = jnp.zeros((8, d), F32)

    rb = min(s, 256)
    srcs = (q_ref, k_ref, v_ref)
    dsts = (q_scr, k_scr, v_scr)

    for hd in range(hb):
        def fill(i, carry, hd=hd):
            r0 = pl.multiple_of(i * rb, rb)
            for which in range(3):
                pad_scr[which, pl.ds(8 + r0, rb), :] = (
                    srcs[which][0, pl.ds(r0, rb), hd * d:(hd + 1) * d].astype(F32))
            return carry

        lax.fori_loop(0, s // rb, fill, 0)

        def block(i, carry, hd=hd):
            r0 = pl.multiple_of(i * rb, rb)
            for which in range(3):
                acc = pad_scr[which, pl.ds(r0 + 8 - half, rb), :] * cw_ref[which, hd, 0:1, :]
                for j in range(1, CONV_K):
                    acc = acc + (pad_scr[which, pl.ds(r0 + 8 - half + j, rb), :]
                                 * cw_ref[which, hd, j:j + 1, :])
                y = acc * _sigmoid(acc)
                if which < 2:
                    y = y * lax.rsqrt(jnp.sum(y * y, axis=-1, keepdims=True) + EPS)
                dsts[which][hd, pl.ds(r0, rb), :] = y.astype(BF16)
                if which == 1:
                    for cc in range(rb // l):
                        yc = y[cc * l:(cc + 1) * l, :]
                        kt_scr[hd, i * (rb // l) + cc] = jnp.concatenate([yc, yc], axis=0).T.astype(BF16)
            return carry

        lax.fori_loop(0, s // rb, block, 0)

    t_idx = lax.broadcasted_iota(jnp.int32, (l, 2 * l), 0)
    lane = lax.broadcasted_iota(jnp.int32, (l, 2 * l), 1)
    is_f = lane < l
    s_idx = jnp.where(is_f, lane, lane - l)
    incl = jnp.where(is_f, t_idx - s_idx, s_idx - t_idx) >= 0
    diag = t_idx == s_idx
    eye2 = diag.astype(F32)
    zeros_b = jnp.zeros((l, d), BF16)

    def blockdiag(x2):
        return jnp.concatenate([jnp.where(is_f, x2, 0.0), jnp.where(is_f, 0.0, x2)], axis=0).astype(BF16)

    base = 8
    shift = base.bit_length() - 1
    same_base = (t_idx >> shift) == (s_idx >> shift)
    merge_masks = []
    size = base
    while size < l:
        sh = size.bit_length() - 1
        merge_masks.append(((t_idx >> (sh + 1)) == (s_idx >> (sh + 1))) & ((t_idx >> sh) != (s_idx >> sh)))
        size *= 2

    half_cu = cu // 2

    def b_step(j):
        chains = [(hd, dr, j if dr == 0 else nc - 1 - j) for hd in range(hb) for dr in range(2)]
        states = [st_scr[2 * hd + dr] for hd, dr, _ in chains]
        rs = [_dot(qm_scr[2 * hd + dr, c], st.astype(BF16)) for (hd, dr, c), st in zip(chains, states)]
        for (hd, dr, c), state, r in zip(chains, states, rs):
            idx = 2 * hd + dr
            gam_row = grow_ref[0, hd, c][0:1, :]
            g_last = gam_row[:, l - 1:l] if dr == 0 else gam_row[:, l:l + 1]
            st_scr[idx] = jnp.exp(g_last) * state + r[:d] + n_scr[idx, c]
            od_scr[idx, pl.ds(pl.multiple_of(c * l, l), l), :] = r[d:].astype(BF16)

    def phase_a(it, b_steps):
        chunk_ids = ([it * half_cu + cc for cc in range(half_cu)]
                     + [nc - half_cu * (it + 1) + cc for cc in range(half_cu)])
        probs = [dict(c=c, hd=hd) for c in chunk_ids for hd in range(hb)]
        pending = list(b_steps)

        def slot():
            if pending:
                b_step(pending.pop(0))

        for pr in probs:
            c, hd = pr["c"], pr["hd"]
            r0 = pl.multiple_of(c * l, l)
            qb = q_scr[hd, pl.ds(r0, l), :]
            kb = k_scr[hd, pl.ds(r0, l), :]
            pr["kq"] = _dot_nt(jnp.concatenate([qb, kb], axis=0),
                               jnp.concatenate([kb, kb], axis=0))
        for pr in probs:
            c, hd = pr["c"], pr["hd"]
            cols = gcol_ref[0, 0, pl.ds(pl.multiple_of(c * l, l), l), :]
            col = lambda j: cols[:, 8 * hd + G_GATE0 + j:8 * hd + G_GATE0 + j + 1]
            gam_f, beta_f, gam_b, beta_b = col(G_C_F), col(G_X_F), col(G_C_B), col(G_X_B)
            gam_row = grow_ref[0, hd, c][0:1, :]
            e_incl = jnp.exp(jnp.where(incl, jnp.where(is_f, gam_f, gam_b) - gam_row, -jnp.inf))
            pr["attn2"] = pr["kq"][:l] * scale * e_incl
            a2 = jnp.where(is_f, beta_f, beta_b) * pr["kq"][l:] * jnp.where(diag, 0.0, e_incl)
            pr["a2"] = a2
            dblk = jnp.where(same_base, a2, 0.0)
            pr["p"] = eye2 - dblk
            pr["d2"] = _dot(dblk.astype(BF16), blockdiag(dblk))
        for pr in probs:
            res = _dot(jnp.concatenate([pr["p"], pr["d2"]], axis=0).astype(BF16), blockdiag(pr["d2"]))
            pr["p"] = pr["p"] + res[:l]
            pr["d4"] = res[l:]
        for pr in probs:
            pr["p"] = pr["p"] + _dot(pr["p"].astype(BF16), blockdiag(pr["d4"]))
        slot()
        for mask in merge_masks:
            for pr in probs:
                pr["x"] = _dot(jnp.where(mask, pr["a2"], 0.0).astype(BF16), blockdiag(pr["p"]))
            for pr in probs:
                pr["p"] = pr["p"] - _dot(pr["p"].astype(BF16), blockdiag(pr["x"]))
            slot()
        for pr in probs:
            c, hd = pr["c"], pr["hd"]
            r0 = pl.multiple_of(c * l, l)
            kb = k_scr[hd, pl.ds(r0, l), :]
            vb = v_scr[hd, pl.ds(r0, l), :]
            rows = grow_ref[0, hd, c]
            gam_row, beta_row = rows[0:1, :], rows[1:2, :]
            tk = (pr["p"] * (-beta_row * jnp.exp(gam_row))).astype(BF16)
            tv = (pr["p"] * beta_row).astype(BF16)
            pr["wk"] = _dot(tk, jnp.concatenate([jnp.concatenate([kb, zeros_b], axis=-1),
                                                 jnp.concatenate([zeros_b, kb], axis=-1)], axis=0))
            pr["wv"] = _dot(tv, jnp.concatenate([jnp.concatenate([vb, zeros_b], axis=-1),
                                                 jnp.concatenate([zeros_b, vb], axis=-1)], axis=0))
        for pr in probs:
            c, hd = pr["c"], pr["hd"]
            gam_row = grow_ref[0, hd, c][0:1, :]
            g_last = jnp.where(is_f[0:1, :], gam_row[:, l - 1:l], gam_row[:, l:l + 1])
            wk, wv = pr["wk"].astype(BF16), pr["wv"].astype(BF16)
            wu_bd = jnp.concatenate(
                [jnp.concatenate([wk[:, :d], wv[:, :d], zeros_b, zeros_b], axis=-1),
                 jnp.concatenate([zeros_b, zeros_b, wk[:, d:], wv[:, d:]], axis=-1)], axis=0)
            k_dec_t = (kt_scr[hd, c].astype(F32) * jnp.exp(g_last - gam_row)).astype(BF16)
            pr["res"] = _dot(jnp.concatenate([pr["attn2"].astype(BF16), k_dec_t], axis=0), wu_bd)
        for pr in probs:
            c, hd = pr["c"], pr["hd"]
            r0 = pl.multiple_of(c * l, l)
            q = q_scr[hd, pl.ds(r0, l), :].astype(F32)
            cols = gcol_ref[0, 0, pl.ds(r0, l), :]
            col = lambda j: cols[:, 8 * hd + G_GATE0 + j:8 * hd + G_GATE0 + j + 1]
            gam_f, gam_b = col(G_C_F), col(G_C_B)
            res = pr["res"]
            qd_f = q * (jnp.exp(gam_f) * scale) + res[:l, 0:d]
            qd_b = q * (jnp.exp(gam_b) * scale) + res[:l, 2 * d:3 * d]
            qm_scr[2 * hd, c] = jnp.concatenate([res[l:, 0:d], qd_f], axis=0).astype(BF16)
            qm_scr[2 * hd + 1, c] = jnp.concatenate([res[l:, 2 * d:3 * d], qd_b], axis=0).astype(BF16)
            n_scr[2 * hd, c] = res[l:, d:2 * d].astype(BF16)
            n_scr[2 * hd + 1, c] = res[l:, 3 * d:].astype(BF16)
            oa_scr[hd, pl.ds(r0, l), :] = res[:l, d:2 * d] + res[:l, 3 * d:]
        while pending:
            slot()

    st_scr[...] = jnp.zeros_like(st_scr)
    n_iter = nc // cu
    phase_a(0, [])

    def body_ab(it, carry):
        phase_a(it, [(it - 1) * half_cu + u for u in range(half_cu)])
        return carry

    lax.fori_loop(1, n_iter, body_ab, 0)

    def body_b(j, carry):
        b_step(j)
        return carry

    lax.fori_loop((n_iter - 1) * half_cu, nc, body_b, 0)

    for hd in range(hb):
        z = z_ref[0, :, hd * d:(hd + 1) * d].astype(F32)
        o = oa_scr[hd] + od_scr[2 * hd] + od_scr[2 * hd + 1]
        o_ref[0, :, hd * d:(hd + 1) * d] = (_rms(o, nw_ref[...]) * (z * _sigmoid(z))).astype(BF16)


def _gdn(proj, conv_w, gcols, grows, norm_w):
    b, s, _ = proj.shape
    l = G_CHUNK
    assert 2 * l == LANES
    nc = s // l
    hb = G_HEADS_PER_STEP
    cu = min(G_CHUNKS_PER_ITER, nc)
    d = HEAD_DIM
    blk = lambda off: pl.BlockSpec((1, s, hb * d), lambda i, h: (i, 0, off * HEADS // hb + h))
    return pl.pallas_call(
        functools.partial(_gdn_kernel, l=l, nc=nc, hb=hb, cu=cu),
        grid=(b, HEADS // hb),
        in_specs=[blk(4), blk(5), blk(6), blk(7),
                  pl.BlockSpec((3, hb, 8, d), lambda i, h: (0, h, 0, 0)),
                  pl.BlockSpec((1, 1, s, LANES), lambda i, h: (i, h, 0, 0)),
                  pl.BlockSpec((1, hb, nc, 8, 2 * l), lambda i, h: (i, h, 0, 0, 0)),
                  pl.BlockSpec((1, d), lambda i, h: (0, 0))],
        out_specs=pl.BlockSpec((1, s, hb * d), lambda i, h: (i, 0, h)),
        out_shape=jax.ShapeDtypeStruct((b, s, HEADS * d), BF16),
        scratch_shapes=[pltpu.VMEM((3, s + 16, d), F32),
                        pltpu.VMEM((hb, s, d), BF16), pltpu.VMEM((hb, s, d), BF16),
                        pltpu.VMEM((hb, s, d), BF16),
                        pltpu.VMEM((hb, nc, d, 2 * l), BF16),
                        pltpu.VMEM((2 * hb, nc, l + d, d), BF16),
                        pltpu.VMEM((2 * hb, nc, d, d), BF16),
                        pltpu.VMEM((hb, s, d), F32),
                        pltpu.VMEM((2 * hb, s, d), BF16),
                        pltpu.VMEM((2 * hb, d, d), F32)],
        compiler_params=_params("parallel", "parallel"),
        name="gdn",
    )(proj, proj, proj, proj, conv_w, gcols, grows, norm_w.reshape(1, d))


def _merge_kernel(x_ref, mod_ref, hm_ref, hg_ref, mm_ref, mg_ref, wm_ref, wg_ref, wo_ref, o_ref):
    gate = mod_ref[0, 5:6, :]
    y = (_sigmoid(mm_ref[0].astype(F32)) * _dot(hm_ref[0], wm_ref[...])
         + _sigmoid(mg_ref[0].astype(F32)) * _dot(hg_ref[0], wg_ref[...]))
    o_ref[0] = x_ref[0] + gate * _dot(y.astype(BF16), wo_ref[...])


def _merge(x, mod, h_m, h_g, proj, w_m, w_g, w_o):
    b, s, d = x.shape
    tm = min(512, s)
    width = HEADS * HEAD_DIM
    tok = lambda w, off: pl.BlockSpec((1, tm, w), lambda i, j: (i, j, off))
    wspec = lambda r: pl.BlockSpec((r, d), lambda i, j: (0, 0))
    merge_off = 8 * width // d
    return pl.pallas_call(
        _merge_kernel,
        grid=(b, s // tm),
        in_specs=[tok(d, 0),
                  pl.BlockSpec((1, 9, d), lambda i, j: (i, 0, 0)),
                  tok(width, 0), tok(width, 0),
                  tok(d, merge_off), tok(d, merge_off + 1),
                  wspec(width), wspec(width), wspec(d)],
        out_specs=tok(d, 0),
        out_shape=jax.ShapeDtypeStruct((b, s, d), F32),
        compiler_params=_params("parallel", "parallel"),
        name="merge",
    )(x, mod, h_m, h_g, proj, proj, w_m, w_g, w_o)


def kernel(x, c, w_ada, b_ada, norm_ffn1, w_ffn1_in, w_ffn1_out, norm_mix, w_in, mlstm_gate_bias,
           gdn_a_log, gdn_dt_bias, gdn_conv_w, mlstm_out_norm, gdn_out_norm, w_branch_mlstm,
           w_branch_gdn, w_out, norm_ffn2, w_ffn2_in, w_ffn2_out, norm_final):
    b, s, d = x.shape
    depth = w_ada.shape[0]
    width = HEADS * HEAD_DIM
    ng = 4 * HEADS
    sizes = (width,) * 4 + (ng,) + (width,) * 4 + (ng,) + (d, d)
    offs = [0]
    for sz in sizes:
        offs.append(offs[-1] + sz)
    for layer in range(depth):
        mod = _ada(c, w_ada[layer], b_ada[layer]).reshape(b, 9, d)
        x = _ffn(x, mod, norm_ffn1[layer], w_ffn1_in[layer], w_ffn1_out[layer], norm_final,
                 sub=0, final=False)

        wl = w_in[layer]
        col = lambda i: wl[:, offs[i]:offs[i + 1]]
        w_big = jnp.concatenate([col(i).astype(BF16) for i in (0, 1, 2, 3, 5, 6, 7, 8, 10, 11)], axis=1)
        head_major = lambda t: t.reshape(t.shape[:-1] + (4, HEADS)).swapaxes(-1, -2)
        pad_lanes = lambda t: jnp.pad(t.reshape(t.shape[:-2] + (8 * HEADS,)),
                                      [(0, 0)] * (t.ndim - 2) + [(0, LANES - 8 * HEADS)])
        w_gates = pad_lanes(jnp.concatenate([head_major(col(4)), head_major(col(9))], axis=-1)).astype(BF16)
        proj, gates_raw = _proj(x, mod, norm_mix[layer], w_big, w_gates)

        zeros_h = jnp.zeros((HEADS,), F32)
        dt_rows = jnp.stack([gdn_dt_bias[layer][0], zeros_h, gdn_dt_bias[layer][1], zeros_h]).reshape(ng)
        alog_rows = jnp.stack([gdn_a_log[layer][0], zeros_h, gdn_a_log[layer][1], zeros_h]).reshape(ng)
        bias_row = pad_lanes(jnp.concatenate([head_major(mlstm_gate_bias[layer].reshape(ng)),
                                              head_major(dt_rows)], axis=-1))
        alog_row = pad_lanes(jnp.concatenate([jnp.zeros((HEADS, 4), F32), head_major(alog_rows)], axis=-1))
        gparams = jnp.zeros((8, LANES), F32).at[0].set(bias_row).at[1].set(alog_row)
        assert M_HEADS_PER_STEP == G_HEADS_PER_STEP
        gate_cols, gate_rows = _gate_prep(gates_raw, gparams, HEADS // M_HEADS_PER_STEP)
        gate_rows = gate_rows.reshape(b, HEADS, 8, s)
        m_rows = _chunk_rows(gate_rows, 0, M_CHUNK)
        g_rows = _pair_rows(gate_rows, 4, G_CHUNK)

        h_m = _mlstm(proj, gate_cols, m_rows, mlstm_out_norm[layer])
        conv_w = gdn_conv_w[layer].reshape(CONV_K, 3, HEADS, HEAD_DIM).transpose(1, 2, 0, 3)
        conv_w = jnp.pad(conv_w, ((0, 0), (0, 0), (0, 8 - CONV_K), (0, 0)))
        h_g = _gdn(proj, conv_w, gate_cols, g_rows, gdn_out_norm[layer])

        x = _merge(x, mod, h_m, h_g, proj, w_branch_mlstm[layer].astype(BF16),
                   w_branch_gdn[layer].astype(BF16), w_out[layer].astype(BF16))
        last = layer == depth - 1
        x = _ffn(x, mod, norm_ffn2[layer], w_ffn2_in[layer], w_ffn2_out[layer], norm_final,
                 sub=2, final=last)
    return x
```

```python
import functools

import jax
import jax.numpy as jnp
from jax import lax
from jax.experimental import pallas as pl
from jax.experimental.pallas import tpu as pltpu

F32 = jnp.float32
BF16 = jnp.bfloat16

EPS = 1e-6
FFN_RES = 0.5
HEADS = 8
HEAD_DIM = 128
CONV_K = 5
LANES = 128
BF16_SUBLANES = 16
M_CHUNK = 128
M_HEADS_PER_STEP = 2
M_CHUNKS_PER_ITER = 4
G_CHUNK = 64
G_HEADS_PER_STEP = 2
G_CHUNKS_PER_ITER = 8
VMEM_LIMIT = 56 * 1024 * 1024

M_X_F, M_C_F, M_X_B, M_C_B = 0, 1, 2, 3
G_C_F, G_X_F, G_C_B, G_X_B = 0, 1, 2, 3
M_GATE0, G_GATE0 = 0, 4


def _dot(a, b):
    return jnp.dot(a, b, preferred_element_type=F32)


def _dot_nt(a, b):
    return lax.dot_general(a, b, (((1,), (1,)), ((), ())), preferred_element_type=F32)


def _dot_tn(a, b):
    return lax.dot_general(a, b, (((0,), (0,)), ((), ())), preferred_element_type=F32)


def _rms(x, w):
    return (x * lax.rsqrt(jnp.mean(x * x, axis=-1, keepdims=True) + EPS)) * w


def _sigmoid(x):
    return 1.0 / (1.0 + jnp.exp(-x))


def _params(*sem):
    return pltpu.CompilerParams(dimension_semantics=sem, vmem_limit_bytes=VMEM_LIMIT)


def _ada_kernel(c_ref, w_ref, b_ref, o_ref):
    c = c_ref[...]
    cs = (c * _sigmoid(c)).astype(BF16)
    o_ref[...] = _dot(cs, w_ref[...].astype(BF16)) + b_ref[...]


def _ada(c, w_ada, b_ada):
    b, d = c.shape
    n = w_ada.shape[1]
    tn = 1024
    return pl.pallas_call(
        _ada_kernel,
        grid=(n // tn,),
        in_specs=[pl.BlockSpec((b, d), lambda j: (0, 0)),
                  pl.BlockSpec((d, tn), lambda j: (0, j)),
                  pl.BlockSpec((1, tn), lambda j: (0, j))],
        out_specs=pl.BlockSpec((b, tn), lambda j: (0, j)),
        out_shape=jax.ShapeDtypeStruct((b, n), F32),
        compiler_params=_params("parallel"),
        name="ada",
    )(c, w_ada, b_ada.reshape(1, n))


def _ffn_kernel(x_ref, mod_ref, nw_ref, wg_ref, wu_ref, wo_ref, nf_ref, o_ref, a_scr,
                *, sub, final, fc):
    x = x_ref[0]
    shift = mod_ref[0, 3 * sub + 0:3 * sub + 1, :]
    scale = mod_ref[0, 3 * sub + 1:3 * sub + 2, :]
    gate = mod_ref[0, 3 * sub + 2:3 * sub + 3, :]
    h = (_rms(x, nw_ref[...]) * (1.0 + scale) + shift).astype(BF16)
    f = wg_ref.shape[1]
    for c0 in range(0, f, fc):
        c1 = min(c0 + fc, f)
        g = _dot(h, wg_ref[:, c0:c1])
        u = _dot(h, wu_ref[:, c0:c1])
        a_scr[:, c0:c1] = (g * _sigmoid(g) * u).astype(BF16)
    y = x + FFN_RES * gate * _dot(a_scr[...], wo_ref[...])
    if final:
        y = _rms(y, nf_ref[...])
    o_ref[0] = y


def _ffn(x, mod, norm_w, w_in, w_out, norm_final, *, sub, final):
    b, s, d = x.shape
    f = w_out.shape[0]
    tm = min(512, s)
    w_in = w_in.astype(BF16)
    wo = w_out.astype(BF16)
    const2 = lambda i, j: (0, 0)
    single = pl.Buffered(1)
    return pl.pallas_call(
        functools.partial(_ffn_kernel, sub=sub, final=final, fc=512),
        grid=(b, s // tm),
        in_specs=[pl.BlockSpec((1, tm, d), lambda i, j: (i, j, 0)),
                  pl.BlockSpec((1, 9, d), lambda i, j: (i, 0, 0)),
                  pl.BlockSpec((1, d), const2),
                  pl.BlockSpec((d, f), const2, pipeline_mode=single),
                  pl.BlockSpec((d, f), lambda i, j: (0, 1), pipeline_mode=single),
                  pl.BlockSpec((f, d), const2, pipeline_mode=single),
                  pl.BlockSpec((1, d), const2)],
        out_specs=pl.BlockSpec((1, tm, d), lambda i, j: (i, j, 0)),
        out_shape=jax.ShapeDtypeStruct((b, s, d), F32),
        scratch_shapes=[pltpu.VMEM((tm, f), BF16)],
        compiler_params=_params("parallel", "parallel"),
        name="ffn_final" if final else "ffn",
    )(x, mod, norm_w.reshape(1, d), w_in, w_in, wo, norm_final.reshape(1, d))


def _proj_kernel(x_ref, mod_ref, nw_ref, w_ref, wgate_ref, o_ref, gate_ref, h_scr):
    @pl.when(pl.program_id(2) == 0)
    def _():
        shift = mod_ref[0, 3:4, :]
        scale = mod_ref[0, 4:5, :]
        h = (_rms(x_ref[0], nw_ref[...]) * (1.0 + scale) + shift).astype(BF16)
        h_scr[...] = h
        gate_ref[0] = _dot(h, wgate_ref[...])

    o_ref[0] = _dot(h_scr[...], w_ref[...]).astype(BF16)


def _proj(x, mod, norm_w, w_big, w_gates):
    b, s, d = x.shape
    n = w_big.shape[1]
    tm, tn = min(1024, s), 2048
    return pl.pallas_call(
        _proj_kernel,
        grid=(b, s // tm, n // tn),
        in_specs=[pl.BlockSpec((1, tm, d), lambda i, j, k: (i, j, 0)),
                  pl.BlockSpec((1, 9, d), lambda i, j, k: (i, 0, 0)),
                  pl.BlockSpec((1, d), lambda i, j, k: (0, 0)),
                  pl.BlockSpec((d, tn), lambda i, j, k: (0, k)),
                  pl.BlockSpec((d, LANES), lambda i, j, k: (0, 0))],
        out_specs=[pl.BlockSpec((1, tm, tn), lambda i, j, k: (i, j, k)),
                   pl.BlockSpec((1, tm, LANES), lambda i, j, k: (i, j, 0))],
        out_shape=[jax.ShapeDtypeStruct((b, s, n), BF16),
                   jax.ShapeDtypeStruct((b, s, LANES), F32)],
        scratch_shapes=[pltpu.VMEM((tm, d), BF16)],
        compiler_params=_params("parallel", "parallel", "arbitrary"),
        name="proj",
    )(x, mod, norm_w.reshape(1, d), w_big, w_gates)


def _split_dot(tri, y):
    hi = y.astype(BF16)
    r1 = y - hi.astype(F32)
    mid = r1.astype(BF16)
    lo = (r1 - mid.astype(F32)).astype(BF16)
    return _dot(tri, hi) + _dot(tri, mid) + _dot(tri, lo)


def _gate_prep_kernel(g_ref, p_ref, cols_ref, rows_ref, y_scr, o_scr, *, lm, lg, lanes_per_group):
    s = g_ref.shape[1]
    lane = lax.broadcasted_iota(jnp.int32, (1, LANES), 1)
    j = lane & 7
    pre = g_ref[0] + p_ref[0:1, :]
    a_coef = -jnp.exp(p_ref[1:2, :])
    tail = jnp.log(1.0 + jnp.exp(-jnp.abs(pre)))
    sp = jnp.maximum(pre, 0.0) + tail
    log_sig = jnp.minimum(pre, 0.0) - tail
    is_logf = (j == 1) | (j == 3)
    is_a = (j == 4) | (j == 6)
    is_beta = (j == 5) | (j == 7)
    y = jnp.where(is_logf, log_sig, pre)
    y = jnp.where(is_a, a_coef * sp, y)
    y = jnp.where(is_beta, _sigmoid(pre), y)
    y_scr[...] = y
    is_prefix = (j == 1) | (j == 4)
    is_suffix = (j == 3) | (j == 6)

    def cumsum_pass(l, lanes_sel):
        t = lax.broadcasted_iota(jnp.int32, (l, l), 0)
        u = lax.broadcasted_iota(jnp.int32, (l, l), 1)
        tri = (t >= u).astype(BF16)

        per_iter = min(4, s // l)

        def body(it, carry):
            starts = [pl.multiple_of((it * per_iter + u) * l, l) for u in range(per_iter)]
            ycs = [y_scr[pl.ds(r0, l), :] for r0 in starts]
            pres = [_split_dot(tri, yc) for yc in ycs]
            for r0, yc, pre_c in zip(starts, ycs, pres):
                tot = pre_c[l - 1:l, :]
                suf_c = tot - pre_c + yc
                out = jnp.where(is_prefix, pre_c, jnp.where(is_suffix, suf_c, yc))
                cur = o_scr[pl.ds(r0, l), :]
                o_scr[pl.ds(r0, l), :] = jnp.where(lanes_sel, out, cur)
            return carry

        lax.fori_loop(0, s // (l * per_iter), body, 0)

    o_scr[...] = y
    cumsum_pass(lm, j < 4)
    cumsum_pass(lg, j >= 4)

    n_rows = rows_ref.shape[1]
    blk = min(s, 256)

    def emit(i, carry):
        r0 = pl.multiple_of(i * blk, blk)
        o = o_scr[pl.ds(r0, blk), :]
        for grp in range(cols_ref.shape[1]):
            shift = (LANES - lanes_per_group * grp) % LANES
            cols_ref[0, grp, pl.ds(r0, blk), :] = o if shift == 0 else pltpu.roll(o, shift, axis=1)
        for r in range(0, blk, LANES):
            rows_ref[0, :, pl.ds(r0 + r, LANES)] = o[r:r + LANES, :].T[:n_rows, :]
        return carry

    lax.fori_loop(0, s // blk, emit, 0)


def _gate_prep(gates_raw, params, groups):
    b, s, _ = gates_raw.shape
    n_rows = 8 * HEADS
    return pl.pallas_call(
        functools.partial(_gate_prep_kernel, lm=M_CHUNK, lg=G_CHUNK, lanes_per_group=n_rows // groups),
        grid=(b,),
        in_specs=[pl.BlockSpec((1, s, LANES), lambda i: (i, 0, 0)),
                  pl.BlockSpec((8, LANES), lambda i: (0, 0))],
        out_specs=[pl.BlockSpec((1, groups, s, LANES), lambda i: (i, 0, 0, 0)),
                   pl.BlockSpec((1, n_rows, s), lambda i: (i, 0, 0))],
        out_shape=[jax.ShapeDtypeStruct((b, groups, s, LANES), F32),
                   jax.ShapeDtypeStruct((b, n_rows, s), F32)],
        scratch_shapes=[pltpu.VMEM((s, LANES), F32), pltpu.VMEM((s, LANES), F32)],
        compiler_params=_params("parallel"),
        name="gate_prep",
    )(gates_raw, params)


def _chunk_rows(rows_t, first, l):
    b, _, _, s = rows_t.shape
    rows = rows_t[:, :, first:first + 4].reshape(b, HEADS, 4, s // l, l).transpose(0, 1, 3, 2, 4)
    return jnp.pad(rows, ((0, 0), (0, 0), (0, 0), (0, 4), (0, 0)))


def _mlstm_kernel(q_ref, k_ref, v_ref, og_ref, gcol_ref, grow_ref, nw_ref, o_ref,
                  qt_scr, vt_scr, h_scr, c_scr, m_scr, *, l, nc, hb, cu):
    d = HEAD_DIM
    da = vt_scr.shape[2]
    scale = d ** -0.5
    s_idx = lax.broadcasted_iota(jnp.int32, (l, l), 0)
    t_idx = lax.broadcasted_iota(jnp.int32, (l, l), 1)
    ones_rows = (lax.broadcasted_iota(jnp.int32, (da - d, l), 0) == 0).astype(BF16)

    for hd in range(hb):
        for c in range(nc):
            qt_scr[hd, c] = q_ref[0, c * l:(c + 1) * l, hd * d:(hd + 1) * d].astype(F32).T.astype(BF16)
            vt = v_ref[0, c * l:(c + 1) * l, hd * d:(hd + 1) * d].astype(F32).T.astype(BF16)
            vt_scr[hd, c] = jnp.concatenate([vt, ones_rows], axis=0)

    def gate_idx(dr):
        return (M_X_F, M_C_F) if dr == 0 else (M_X_B, M_C_B)

    def chunk_of(dr, step):
        return step if dr == 0 else nc - 1 - step

    def total_logf(b_row, dr):
        return b_row[:, l - 1:l] if dr == 0 else b_row[:, 0:1]

    row_id = lax.broadcasted_iota(jnp.int32, (8 * nc, 1), 0) & 7
    for hd in range(hb):
        rows_all = grow_ref[0, hd].reshape(8 * nc, l)
        b_up = pltpu.roll(rows_all, 8 * nc - 1, axis=0)
        g_col = jnp.where(row_id == M_X_F, b_up[:, l - 1:l], b_up[:, 0:1])
        a_max = jnp.max(g_col - b_up + rows_all, axis=-1, keepdims=True)
        g_rep = jnp.broadcast_to(g_col, (8 * nc, l))
        a_rep = jnp.broadcast_to(a_max, (8 * nc, l))
        for dr in range(2):
            xi, _ = gate_idx(dr)
            m = jnp.zeros((1, l), F32)
            for step in range(nc):
                c = chunk_of(dr, step)
                r = 8 * c + xi
                m_scr[2 * hd + dr, c] = jnp.broadcast_to(m, (8, l))
                m = jnp.maximum(g_rep[r:r + 1, :] + m, a_rep[r:r + 1, :])

    c_scr[...] = jnp.zeros_like(c_scr)

    def body(it, carry):
        probs = []
        for k_step in range(cu):
            for hd in range(hb):
                for dr in range(2):
                    probs.append(dict(hd=hd, dr=dr, c=chunk_of(dr, it * cu + k_step)))
        for pr in probs:
            hd, c = pr["hd"], pr["c"]
            pr["k"] = k_ref[0, pl.ds(pl.multiple_of(c * l, l), l), hd * d:(hd + 1) * d]
            pr["qt"] = qt_scr[hd, c]
            pr["st"] = _dot(pr["k"], pr["qt"])
        for pr in probs:
            hd, dr, c = pr["hd"], pr["dr"], pr["c"]
            xi, ci = gate_idx(dr)
            cols = gcol_ref[0, 0, pl.ds(pl.multiple_of(c * l, l), l), :]
            rows = grow_ref[0, hd, c]
            lane0 = 8 * hd + M_GATE0
            c_col = cols[:, lane0 + ci:lane0 + ci + 1] - cols[:, lane0 + xi:lane0 + xi + 1]
            i_row, b_row = rows[xi:xi + 1, :], rows[ci:ci + 1, :]
            m_prev = m_scr[2 * hd + dr, c][0:1, :]
            mask = (s_idx <= t_idx) if dr == 0 else (s_idx >= t_idx)
            dmat = jnp.where(mask, b_row - c_col, -jnp.inf)
            inter = b_row + m_prev
            m_t = jnp.maximum(inter, jnp.max(dmat, axis=0, keepdims=True))
            pst = (pr["st"] * scale * jnp.exp(dmat - m_t)).astype(BF16)
            vt = vt_scr[hd, c]
            q_w = (pr["qt"].astype(F32) * (jnp.exp(inter - m_t) * scale)).astype(BF16)
            pr["vt"] = vt
            pr["rhs"] = jnp.concatenate([pst, q_w], axis=0)
            pr["floor"] = jnp.exp(-m_t)
            g = total_logf(b_row, dr)
            m_new = jnp.maximum(g + m_prev, jnp.max(g - b_row + i_row, axis=-1, keepdims=True))
            pr["decay"] = jnp.exp(g + m_prev - m_new)
            vw = (vt.astype(F32) * jnp.exp(g - b_row + i_row - m_new)).astype(BF16)
            pr["x"] = _dot(vw, pr["k"])
        cmats = {}
        for pr in probs:
            key = 2 * pr["hd"] + pr["dr"]
            cmat = cmats[key] if key in cmats else c_scr[key]
            pr["num"] = _dot(jnp.concatenate([pr["vt"], cmat.astype(BF16)], axis=1), pr["rhs"])
            cmats[key] = pr["decay"] * cmat + pr["x"]
        for key, cmat in cmats.items():
            c_scr[key] = cmat
        for pr in probs:
            hd, dr, c = pr["hd"], pr["dr"], pr["c"]
            num_aug = pr["num"]
            den = num_aug[d:d + 1, :]
            h_scr[2 * hd + dr, c] = num_aug[:d, :] / jnp.maximum(jnp.abs(den), pr["floor"])
        return carry

    lax.fori_loop(0, nc // cu, body, 0)
    for hd in range(hb):
        for c in range(nc):
            ht = h_scr[2 * hd, c] + h_scr[2 * hd + 1, c]
            ht = ht * lax.rsqrt(jnp.mean(ht * ht, axis=0, keepdims=True) + EPS)
            og = og_ref[0, c * l:(c + 1) * l, hd * d:(hd + 1) * d].astype(F32)
            o_ref[0, c * l:(c + 1) * l, hd * d:(hd + 1) * d] = (
                ht.T * nw_ref[...] * _sigmoid(og)).astype(BF16)


def _mlstm(proj, gcols, grows, norm_w):
    b, s, _ = proj.shape
    l = M_CHUNK
    nc = s // l
    hb = M_HEADS_PER_STEP
    cu = min(M_CHUNKS_PER_ITER, nc)
    d = HEAD_DIM
    blk = lambda off: pl.BlockSpec((1, s, hb * d), lambda i, h: (i, 0, off * HEADS // hb + h))
    return pl.pallas_call(
        functools.partial(_mlstm_kernel, l=l, nc=nc, hb=hb, cu=cu),
        grid=(b, HEADS // hb),
        in_specs=[blk(0), blk(1), blk(2), blk(3),
                  pl.BlockSpec((1, 1, s, LANES), lambda i, h: (i, h, 0, 0)),
                  pl.BlockSpec((1, hb, nc, 8, l), lambda i, h: (i, h, 0, 0, 0)),
                  pl.BlockSpec((1, d), lambda i, h: (0, 0))],
        out_specs=pl.BlockSpec((1, s, hb * d), lambda i, h: (i, 0, h)),
        out_shape=jax.ShapeDtypeStruct((b, s, HEADS * d), BF16),
        scratch_shapes=[pltpu.VMEM((hb, nc, d, l), BF16),
                        pltpu.VMEM((hb, nc, d + BF16_SUBLANES, l), BF16),
                        pltpu.VMEM((2 * hb, nc, d, l), F32),
                        pltpu.VMEM((2 * hb, d + BF16_SUBLANES, d), F32),
                        pltpu.VMEM((2 * hb, nc, 8, LANES), F32)],
        compiler_params=_params("parallel", "parallel"),
        name="mlstm",
    )(proj, proj, proj, proj, gcols, grows, norm_w.reshape(1, d))


def _pair_rows(rows_t, first, l):
    b, _, _, s = rows_t.shape
    rows = rows_t[:, :, first:first + 4].reshape(b, HEADS, 2, 2, s // l, l)
    rows = rows.transpose(0, 1, 4, 3, 2, 5).reshape(b, HEADS, s // l, 2, 2 * l)
    return jnp.pad(rows, ((0, 0), (0, 0), (0, 0), (0, 6), (0, 0)))


def _gdn_kernel(q_ref, k_ref, v_ref, z_ref, cw_ref, gcol_ref, grow_ref, nw_ref, o_ref,
                pad_scr, q_scr, k_scr, v_scr, kt_scr, qm_scr, n_scr, oa_scr, od_scr, st_scr,
                *, l, nc, hb, cu):
    s = q_ref.shape[1]
    d = HEAD_DIM
    scale = d ** -0.5
    half = CONV_K // 2

    for which in range(3):
        pad_scr[which, 0:8, :] = jnp.zeros((8, d), F32)
        pad_scr[which, 8 + s:16 + s, :] = jnp.zeros((8, d), F32)

    rb = min(s, 512)
    srcs = (q_ref, k_ref, v_ref)
    dsts = (q_scr, k_scr, v_scr)

    for hd in range(hb):
        def fill(i, carry, hd=hd):
            r0 = pl.multiple_of(i * rb, rb)
            for which in range(3):
                pad_scr[which, pl.ds(8 + r0, rb), :] = (
                    srcs[which][0, pl.ds(r0, rb), hd * d:(hd + 1) * d].astype(F32))
            return carry

        lax.fori_loop(0, s // rb, fill, 0)

        def block(i, carry, hd=hd):
            r0 = pl.multiple_of(i * rb, rb)
            for which in range(3):
                acc = pad_scr[which, pl.ds(r0 + 8 - half, rb), :] * cw_ref[which, hd, 0:1, :]
                for j in range(1, CONV_K):
                    acc = acc + (pad_scr[which, pl.ds(r0 + 8 - half + j, rb), :]
                                 * cw_ref[which, hd, j:j + 1, :])
                y = acc * _sigmoid(acc)
                if which < 2:
                    y = y * lax.rsqrt(jnp.sum(y * y, axis=-1, keepdims=True) + EPS)
                dsts[which][hd, pl.ds(r0, rb), :] = y.astype(BF16)
                if which == 1:
                    for cc in range(rb // l):
                        yc = y[cc * l:(cc + 1) * l, :]
                        kt_scr[hd, i * (rb // l) + cc] = jnp.concatenate([yc, yc], axis=0).T.astype(BF16)
            return carry

        lax.fori_loop(0, s // rb, block, 0)

    t_idx = lax.broadcasted_iota(jnp.int32, (l, 2 * l), 0)
    lane = lax.broadcasted_iota(jnp.int32, (l, 2 * l), 1)
    is_f = lane < l
    s_idx = jnp.where(is_f, lane, lane - l)
    incl = jnp.where(is_f, t_idx - s_idx, s_idx - t_idx) >= 0
    diag = t_idx == s_idx
    eye2 = diag.astype(F32)
    zeros_b = jnp.zeros((l, d), BF16)

    def blockdiag(x2):
        return jnp.concatenate([jnp.where(is_f, x2, 0.0), jnp.where(is_f, 0.0, x2)], axis=0).astype(BF16)

    base = 8
    shift = base.bit_length() - 1
    same_base = (t_idx >> shift) == (s_idx >> shift)
    merge_masks = []
    size = base
    while size < l:
        sh = size.bit_length() - 1
        merge_masks.append(((t_idx >> (sh + 1)) == (s_idx >> (sh + 1))) & ((t_idx >> sh) != (s_idx >> sh)))
        size *= 2

    half_cu = cu // 2

    def b_step(j):
        chains = [(hd, dr, j if dr == 0 else nc - 1 - j) for hd in range(hb) for dr in range(2)]
        states = [st_scr[2 * hd + dr] for hd, dr, _ in chains]
        rs = [_dot(qm_scr[2 * hd + dr, c], st.astype(BF16)) for (hd, dr, c), st in zip(chains, states)]
        for (hd, dr, c), state, r in zip(chains, states, rs):
            idx = 2 * hd + dr
            gam_row = grow_ref[0, hd, c][0:1, :]
            g_last = gam_row[:, l - 1:l] if dr == 0 else gam_row[:, l:l + 1]
            st_scr[idx] = jnp.exp(g_last) * state + r[:d] + n_scr[idx, c]
            od_scr[idx, pl.ds(pl.multiple_of(c * l, l), l), :] = r[d:].astype(BF16)

    def phase_a(it, b_steps):
        chunk_ids = ([it * half_cu + cc for cc in range(half_cu)]
                     + [nc - half_cu * (it + 1) + cc for cc in range(half_cu)])
        probs = [dict(c=c, hd=hd) for c in chunk_ids for hd in range(hb)]
        pending = list(b_steps)

        def slot():
            if pending:
                b_step(pending.pop(0))

        for pr in probs:
            c, hd = pr["c"], pr["hd"]
            r0 = pl.multiple_of(c * l, l)
            qb = q_scr[hd, pl.ds(r0, l), :]
            kb = k_scr[hd, pl.ds(r0, l), :]
            pr["kq"] = _dot_nt(jnp.concatenate([qb, kb], axis=0),
                               jnp.concatenate([kb, kb], axis=0))
        for pr in probs:
            c, hd = pr["c"], pr["hd"]
            cols = gcol_ref[0, 0, pl.ds(pl.multiple_of(c * l, l), l), :]
            col = lambda j: cols[:, 8 * hd + G_GATE0 + j:8 * hd + G_GATE0 + j + 1]
            gam_f, beta_f, gam_b, beta_b = col(G_C_F), col(G_X_F), col(G_C_B), col(G_X_B)
            gam_row = grow_ref[0, hd, c][0:1, :]
            e_incl = jnp.exp(jnp.where(incl, jnp.where(is_f, gam_f, gam_b) - gam_row, -jnp.inf))
            pr["attn2"] = pr["kq"][:l] * scale * e_incl
            a2 = jnp.where(is_f, beta_f, beta_b) * pr["kq"][l:] * jnp.where(diag, 0.0, e_incl)
            pr["a2"] = a2
            dblk = jnp.where(same_base, a2, 0.0)
            pr["p"] = eye2 - dblk
            pr["d2"] = _dot(dblk.astype(BF16), blockdiag(dblk))
        for pr in probs:
            res = _dot(jnp.concatenate([pr["p"], pr["d2"]], axis=0).astype(BF16), blockdiag(pr["d2"]))
            pr["p"] = pr["p"] + res[:l]
            pr["d4"] = res[l:]
        for pr in probs:
            pr["p"] = pr["p"] + _dot(pr["p"].astype(BF16), blockdiag(pr["d4"]))
        slot()
        for mask in merge_masks:
            for pr in probs:
                pr["x"] = _dot(jnp.where(mask, pr["a2"], 0.0).astype(BF16), blockdiag(pr["p"]))
            for pr in probs:
                pr["p"] = pr["p"] - _dot(pr["p"].astype(BF16), blockdiag(pr["x"]))
            slot()
        for pr in probs:
            c, hd = pr["c"], pr["hd"]
            r0 = pl.multiple_of(c * l, l)
            kb = k_scr[hd, pl.ds(r0, l), :]
            vb = v_scr[hd, pl.ds(r0, l), :]
            rows = grow_ref[0, hd, c]
            gam_row, beta_row = rows[0:1, :], rows[1:2, :]
            tk = (pr["p"] * (-beta_row * jnp.exp(gam_row))).astype(BF16)
            tv = (pr["p"] * beta_row).astype(BF16)
            pr["wk"] = _dot(tk, jnp.concatenate([jnp.concatenate([kb, zeros_b], axis=-1),
                                                 jnp.concatenate([zeros_b, kb], axis=-1)], axis=0))
            pr["wv"] = _dot(tv, jnp.concatenate([jnp.concatenate([vb, zeros_b], axis=-1),
                                                 jnp.concatenate([zeros_b, vb], axis=-1)], axis=0))
        for pr in probs:
            c, hd = pr["c"], pr["hd"]
            gam_row = grow_ref[0, hd, c][0:1, :]
            g_last = jnp.where(is_f[0:1, :], gam_row[:, l - 1:l], gam_row[:, l:l + 1])
            wk, wv = pr["wk"].astype(BF16), pr["wv"].astype(BF16)
            wu_bd = jnp.concatenate(
                [jnp.concatenate([wk[:, :d], wv[:, :d], zeros_b, zeros_b], axis=-1),
                 jnp.concatenate([zeros_b, zeros_b, wk[:, d:], wv[:, d:]], axis=-1)], axis=0)
            k_dec_t = (kt_scr[hd, c].astype(F32) * jnp.exp(g_last - gam_row)).astype(BF16)
            pr["res"] = _dot(jnp.concatenate([pr["attn2"].astype(BF16), k_dec_t], axis=0), wu_bd)
        for pr in probs:
            c, hd = pr["c"], pr["hd"]
            r0 = pl.multiple_of(c * l, l)
            q = q_scr[hd, pl.ds(r0, l), :].astype(F32)
            cols = gcol_ref[0, 0, pl.ds(r0, l), :]
            col = lambda j: cols[:, 8 * hd + G_GATE0 + j:8 * hd + G_GATE0 + j + 1]
            gam_f, gam_b = col(G_C_F), col(G_C_B)
            res = pr["res"]
            qd_f = q * (jnp.exp(gam_f) * scale) + res[:l, 0:d]
            qd_b = q * (jnp.exp(gam_b) * scale) + res[:l, 2 * d:3 * d]
            qm_scr[2 * hd, c] = jnp.concatenate([res[l:, 0:d], qd_f], axis=0).astype(BF16)
            qm_scr[2 * hd + 1, c] = jnp.concatenate([res[l:, 2 * d:3 * d], qd_b], axis=0).astype(BF16)
            n_scr[2 * hd, c] = res[l:, d:2 * d].astype(BF16)
            n_scr[2 * hd + 1, c] = res[l:, 3 * d:].astype(BF16)
            oa_scr[hd, pl.ds(r0, l), :] = res[:l, d:2 * d] + res[:l, 3 * d:]
        while pending:
            slot()

    st_scr[...] = jnp.zeros_like(st_scr)
    n_iter = nc // cu
    phase_a(0, [])

    def body_ab(it, carry):
        phase_a(it, [(it - 1) * half_cu + u for u in range(half_cu)])
        return carry

    lax.fori_loop(1, n_iter, body_ab, 0)

    def body_b(j, carry):
        b_step(j)
        return carry

    lax.fori_loop((n_iter - 1) * half_cu, nc, body_b, 0)

    for hd in range(hb):
        z = z_ref[0, :, hd * d:(hd + 1) * d].astype(F32)
        o = oa_scr[hd] + od_scr[2 * hd] + od_scr[2 * hd + 1]
        o_ref[0, :, hd * d:(hd + 1) * d] = (_rms(o, nw_ref[...]) * (z * _sigmoid(z))).astype(BF16)


def _gdn(proj, conv_w, gcols, grows, norm_w):
    b, s, _ = proj.shape
    l = G_CHUNK
    assert 2 * l == LANES
    nc = s // l
    hb = G_HEADS_PER_STEP
    cu = min(G_CHUNKS_PER_ITER, nc)
    d = HEAD_DIM
    blk = lambda off: pl.BlockSpec((1, s, hb * d), lambda i, h: (i, 0, off * HEADS // hb + h))
    return pl.pallas_call(
        functools.partial(_gdn_kernel, l=l, nc=nc, hb=hb, cu=cu),
        grid=(b, HEADS // hb),
        in_specs=[blk(4), blk(5), blk(6), blk(7),
                  pl.BlockSpec((3, hb, 8, d), lambda i, h: (0, h, 0, 0)),
                  pl.BlockSpec((1, 1, s, LANES), lambda i, h: (i, h, 0, 0)),
                  pl.BlockSpec((1, hb, nc, 8, 2 * l), lambda i, h: (i, h, 0, 0, 0)),
                  pl.BlockSpec((1, d), lambda i, h: (0, 0))],
        out_specs=pl.BlockSpec((1, s, hb * d), lambda i, h: (i, 0, h)),
        out_shape=jax.ShapeDtypeStruct((b, s, HEADS * d), BF16),
        scratch_shapes=[pltpu.VMEM((3, s + 16, d), F32),
                        pltpu.VMEM((hb, s, d), BF16), pltpu.VMEM((hb, s, d), BF16),
                        pltpu.VMEM((hb, s, d), BF16),
                        pltpu.VMEM((hb, nc, d, 2 * l), BF16),
                        pltpu.VMEM((2 * hb, nc, l + d, d), BF16),
                        pltpu.VMEM((2 * hb, nc, d, d), BF16),
                        pltpu.VMEM((hb, s, d), F32),
                        pltpu.VMEM((2 * hb, s, d), BF16),
                        pltpu.VMEM((2 * hb, d, d), F32)],
        compiler_params=_params("parallel", "parallel"),
        name="gdn",
    )(proj, proj, proj, proj, conv_w, gcols, grows, norm_w.reshape(1, d))


def _merge_kernel(x_ref, mod_ref, hm_ref, hg_ref, mm_ref, mg_ref, wm_ref, wg_ref, wo_ref, o_ref):
    gate = mod_ref[0, 5:6, :]
    y = (_sigmoid(mm_ref[0].astype(F32)) * _dot(hm_ref[0], wm_ref[...])
         + _sigmoid(mg_ref[0].astype(F32)) * _dot(hg_ref[0], wg_ref[...]))
    o_ref[0] = x_ref[0] + gate * _dot(y.astype(BF16), wo_ref[...])


def _merge(x, mod, h_m, h_g, proj, w_m, w_g, w_o):
    b, s, d = x.shape
    tm = min(512, s)
    width = HEADS * HEAD_DIM
    tok = lambda w, off: pl.BlockSpec((1, tm, w), lambda i, j: (i, j, off))
    wspec = lambda r: pl.BlockSpec((r, d), lambda i, j: (0, 0))
    merge_off = 8 * width // d
    return pl.pallas_call(
        _merge_kernel,
        grid=(b, s // tm),
        in_specs=[tok(d, 0),
                  pl.BlockSpec((1, 9, d), lambda i, j: (i, 0, 0)),
                  tok(width, 0), tok(width, 0),
                  tok(d, merge_off), tok(d, merge_off + 1),
                  wspec(width), wspec(width), wspec(d)],
        out_specs=tok(d, 0),
        out_shape=jax.ShapeDtypeStruct((b, s, d), F32),
        compiler_params=_params("parallel", "parallel"),
        name="merge",
    )(x, mod, h_m, h_g, proj, proj, w_m, w_g, w_o)


def kernel(x, c, w_ada, b_ada, norm_ffn1, w_ffn1_in, w_ffn1_out, norm_mix, w_in, mlstm_gate_bias,
           gdn_a_log, gdn_dt_bias, gdn_conv_w, mlstm_out_norm, gdn_out_norm, w_branch_mlstm,
           w_branch_gdn, w_out, norm_ffn2, w_ffn2_in, w_ffn2_out, norm_final):
    b, s, d = x.shape
    depth = w_ada.shape[0]
    width = HEADS * HEAD_DIM
    ng = 4 * HEADS
    sizes = (width,) * 4 + (ng,) + (width,) * 4 + (ng,) + (d, d)
    offs = [0]
    for sz in sizes:
        offs.append(offs[-1] + sz)
    for layer in range(depth):
        mod = _ada(c, w_ada[layer], b_ada[layer]).reshape(b, 9, d)
        x = _ffn(x, mod, norm_ffn1[layer], w_ffn1_in[layer], w_ffn1_out[layer], norm_final,
                 sub=0, final=False)

        wl = w_in[layer]
        col = lambda i: wl[:, offs[i]:offs[i + 1]]
        w_big = jnp.concatenate([col(i).astype(BF16) for i in (0, 1, 2, 3, 5, 6, 7, 8, 10, 11)], axis=1)
        head_major = lambda t: t.reshape(t.shape[:-1] + (4, HEADS)).swapaxes(-1, -2)
        pad_lanes = lambda t: jnp.pad(t.reshape(t.shape[:-2] + (8 * HEADS,)),
                                      [(0, 0)] * (t.ndim - 2) + [(0, LANES - 8 * HEADS)])
        w_gates = pad_lanes(jnp.concatenate([head_major(col(4)), head_major(col(9))], axis=-1)).astype(BF16)
        proj, gates_raw = _proj(x, mod, norm_mix[layer], w_big, w_gates)

        zeros_h = jnp.zeros((HEADS,), F32)
        dt_rows = jnp.stack([gdn_dt_bias[layer][0], zeros_h, gdn_dt_bias[layer][1], zeros_h]).reshape(ng)
        alog_rows = jnp.stack([gdn_a_log[layer][0], zeros_h, gdn_a_log[layer][1], zeros_h]).reshape(ng)
        bias_row = pad_lanes(jnp.concatenate([head_major(mlstm_gate_bias[layer].reshape(ng)),
                                              head_major(dt_rows)], axis=-1))
        alog_row = pad_lanes(jnp.concatenate([jnp.zeros((HEADS, 4), F32), head_major(alog_rows)], axis=-1))
        gparams = jnp.zeros((8, LANES), F32).at[0].set(bias_row).at[1].set(alog_row)
        assert M_HEADS_PER_STEP == G_HEADS_PER_STEP
        gate_cols, gate_rows = _gate_prep(gates_raw, gparams, HEADS // M_HEADS_PER_STEP)
        gate_rows = gate_rows.reshape(b, HEADS, 8, s)
        m_rows = _chunk_rows(gate_rows, 0, M_CHUNK)
        g_rows = _pair_rows(gate_rows, 4, G_CHUNK)

        h_m = _mlstm(proj, gate_cols, m_rows, mlstm_out_norm[layer])
        conv_w = gdn_conv_w[layer].reshape(CONV_K, 3, HEADS, HEAD_DIM).transpose(1, 2, 0, 3)
        conv_w = jnp.pad(conv_w, ((0, 0), (0, 0), (0, 8 - CONV_K), (0, 0)))
        h_g = _gdn(proj, conv_w, gate_cols, g_rows, gdn_out_norm[layer])

        x = _merge(x, mod, h_m, h_g, proj, w_branch_mlstm[layer].astype(BF16),
                   w_branch_gdn[layer].astype(BF16), w_out[layer].astype(BF16))
        last = layer == depth - 1
        x = _ffn(x, mod, norm_ffn2[layer], w_ffn2_in[layer], w_ffn2_out[layer], norm_final,
                 sub=2, final=last)
    return x
```

```python
import functools

import jax
import jax.numpy as jnp
from jax import lax
from jax.experimental import pallas as pl
from jax.experimental.pallas import tpu as pltpu

F32 = jnp.float32
BF16 = jnp.bfloat16

EPS = 1e-6
FFN_RES = 0.5
HEADS = 8
HEAD_DIM = 128
CONV_K = 5
LANES = 128
BF16_SUBLANES = 16
M_CHUNK = 128
M_HEADS_PER_STEP = 2
M_CHUNKS_PER_ITER = 4
G_CHUNK = 64
G_HEADS_PER_STEP = 2
G_CHUNKS_PER_ITER = 8
VMEM_LIMIT = 56 * 1024 * 1024

M_X_F, M_C_F, M_X_B, M_C_B = 0, 1, 2, 3
G_C_F, G_X_F, G_C_B, G_X_B = 0, 1, 2, 3
M_GATE0, G_GATE0 = 0, 4


def _dot(a, b):
    return jnp.dot(a, b, preferred_element_type=F32)


def _dot_nt(a, b):
    return lax.dot_general(a, b, (((1,), (1,)), ((), ())), preferred_element_type=F32)


def _dot_tn(a, b):
    return lax.dot_general(a, b, (((0,), (0,)), ((), ())), preferred_element_type=F32)


def _rms(x, w):
    return (x * lax.rsqrt(jnp.mean(x * x, axis=-1, keepdims=True) + EPS)) * w


def _sigmoid(x):
    return 1.0 / (1.0 + jnp.exp(-x))


def _params(*sem):
    return pltpu.CompilerParams(dimension_semantics=sem, vmem_limit_bytes=VMEM_LIMIT)


def _ada_kernel(c_ref, w_ref, b_ref, o_ref):
    c = c_ref[...]
    cs = (c * _sigmoid(c)).astype(BF16)
    o_ref[...] = _dot(cs, w_ref[...].astype(BF16)) + b_ref[...]


def _ada(c, w_ada, b_ada):
    b, d = c.shape
    n = w_ada.shape[1]
    tn = 1024
    return pl.pallas_call(
        _ada_kernel,
        grid=(n // tn,),
        in_specs=[pl.BlockSpec((b, d), lambda j: (0, 0)),
                  pl.BlockSpec((d, tn), lambda j: (0, j)),
                  pl.BlockSpec((1, tn), lambda j: (0, j))],
        out_specs=pl.BlockSpec((b, tn), lambda j: (0, j)),
        out_shape=jax.ShapeDtypeStruct((b, n), F32),
        compiler_params=_params("parallel"),
        name="ada",
    )(c, w_ada, b_ada.reshape(1, n))


def _ffn_rows(x, mod_ref, nw_ref, wg_ref, wu_ref, wo_ref, nf_ref, a_scr, *, sub, final, fc):
    shift = mod_ref[0, 3 * sub + 0:3 * sub + 1, :]
    scale = mod_ref[0, 3 * sub + 1:3 * sub + 2, :]
    gate = mod_ref[0, 3 * sub + 2:3 * sub + 3, :]
    h = (_rms(x, nw_ref[...]) * (1.0 + scale) + shift).astype(BF16)
    f = wg_ref.shape[1]
    for c0 in range(0, f, fc):
        c1 = min(c0 + fc, f)
        g = _dot(h, wg_ref[:, c0:c1])
        u = _dot(h, wu_ref[:, c0:c1])
        a_scr[:, c0:c1] = (g * _sigmoid(g) * u).astype(BF16)
    y = x + FFN_RES * gate * _dot(a_scr[...], wo_ref[...])
    if final:
        y = _rms(y, nf_ref[...])
    return y


def _ffn_kernel(x_ref, mod_ref, nw_ref, wg_ref, wu_ref, wo_ref, nf_ref, o_ref, a_scr, **static):
    o_ref[0] = _ffn_rows(x_ref[0], mod_ref, nw_ref, wg_ref, wu_ref, wo_ref, nf_ref, a_scr, **static)


def _merge_ffn_kernel(x_ref, mod_ref, hm_ref, hg_ref, mm_ref, mg_ref, wbm_ref, wbg_ref, wmo_ref,
                      nw_ref, wg_ref, wu_ref, wo_ref, nf_ref, o_ref, a_scr, **static):
    gate = mod_ref[0, 5:6, :]
    y = (_sigmoid(mm_ref[0].astype(F32)) * _dot(hm_ref[0], wbm_ref[...])
         + _sigmoid(mg_ref[0].astype(F32)) * _dot(hg_ref[0], wbg_ref[...]))
    x = x_ref[0] + gate * _dot(y.astype(BF16), wmo_ref[...])
    o_ref[0] = _ffn_rows(x, mod_ref, nw_ref, wg_ref, wu_ref, wo_ref, nf_ref, a_scr, **static)


def _ffn(x, mod, norm_w, w_in, w_out, norm_final, *, sub, final, merge=None):
    b, s, d = x.shape
    f = w_out.shape[0]
    tm = min(512, s)
    w_in = w_in.astype(BF16)
    wo = w_out.astype(BF16)
    const2 = lambda i, j: (0, 0)
    single = pl.Buffered(1)
    tok = lambda w, off: pl.BlockSpec((1, tm, w), lambda i, j: (i, j, off))
    x_specs = [tok(d, 0), pl.BlockSpec((1, 9, d), lambda i, j: (i, 0, 0))]
    ffn_specs = [pl.BlockSpec((1, d), const2),
                 pl.BlockSpec((d, f), const2, pipeline_mode=single),
                 pl.BlockSpec((d, f), lambda i, j: (0, 1), pipeline_mode=single),
                 pl.BlockSpec((f, d), const2, pipeline_mode=single),
                 pl.BlockSpec((1, d), const2)]
    ffn_args = (norm_w.reshape(1, d), w_in, w_in, wo, norm_final.reshape(1, d))
    static = dict(sub=sub, final=final, fc=512)
    if merge is None:
        body, in_specs, args = _ffn_kernel, x_specs + ffn_specs, (x, mod) + ffn_args
        name = "ffn"
    else:
        h_m, h_g, proj, w_m, w_g, w_o = merge
        width = h_m.shape[-1]
        merge_off = 8 * width // d
        wspec = lambda r: pl.BlockSpec((r, d), const2, pipeline_mode=single)
        merge_specs = [tok(width, 0), tok(width, 0), tok(d, merge_off), tok(d, merge_off + 1),
                       wspec(width), wspec(width), wspec(d)]
        body, in_specs = _merge_ffn_kernel, x_specs + merge_specs + ffn_specs
        args = (x, mod, h_m, h_g, proj, proj, w_m, w_g, w_o) + ffn_args
        name = "merge_ffn"
    return pl.pallas_call(
        functools.partial(body, **static),
        grid=(b, s // tm),
        in_specs=in_specs,
        out_specs=tok(d, 0),
        out_shape=jax.ShapeDtypeStruct((b, s, d), F32),
        scratch_shapes=[pltpu.VMEM((tm, f), BF16)],
        compiler_params=_params("parallel", "parallel"),
        name=name,
    )(*args)


def _proj_kernel(x_ref, mod_ref, nw_ref, w_ref, wgate_ref, o_ref, gate_ref, h_scr):
    @pl.when(pl.program_id(2) == 0)
    def _():
        shift = mod_ref[0, 3:4, :]
        scale = mod_ref[0, 4:5, :]
        h = (_rms(x_ref[0], nw_ref[...]) * (1.0 + scale) + shift).astype(BF16)
        h_scr[...] = h
        gate_ref[0] = _dot(h, wgate_ref[...])

    o_ref[0] = _dot(h_scr[...], w_ref[...]).astype(BF16)


def _proj(x, mod, norm_w, w_big, w_gates):
    b, s, d = x.shape
    n = w_big.shape[1]
    tm, tn = min(1024, s), 2048
    return pl.pallas_call(
        _proj_kernel,
        grid=(b, s // tm, n // tn),
        in_specs=[pl.BlockSpec((1, tm, d), lambda i, j, k: (i, j, 0)),
                  pl.BlockSpec((1, 9, d), lambda i, j, k: (i, 0, 0)),
                  pl.BlockSpec((1, d), lambda i, j, k: (0, 0)),
                  pl.BlockSpec((d, tn), lambda i, j, k: (0, k)),
                  pl.BlockSpec((d, LANES), lambda i, j, k: (0, 0))],
        out_specs=[pl.BlockSpec((1, tm, tn), lambda i, j, k: (i, j, k)),
                   pl.BlockSpec((1, tm, LANES), lambda i, j, k: (i, j, 0))],
        out_shape=[jax.ShapeDtypeStruct((b, s, n), BF16),
                   jax.ShapeDtypeStruct((b, s, LANES), F32)],
        scratch_shapes=[pltpu.VMEM((tm, d), BF16)],
        compiler_params=_params("parallel", "parallel", "arbitrary"),
        name="proj",
    )(x, mod, norm_w.reshape(1, d), w_big, w_gates)


def _split_dot(tri, y):
    hi = y.astype(BF16)
    r1 = y - hi.astype(F32)
    mid = r1.astype(BF16)
    lo = (r1 - mid.astype(F32)).astype(BF16)
    return _dot(tri, hi) + _dot(tri, mid) + _dot(tri, lo)


def _gate_prep_kernel(g_ref, p_ref, cols_ref, rows_ref, y_scr, o_scr, *, lm, lg, lanes_per_group):
    s = g_ref.shape[1]
    lane = lax.broadcasted_iota(jnp.int32, (1, LANES), 1)
    j = lane & 7
    pre = g_ref[0] + p_ref[0:1, :]
    a_coef = -jnp.exp(p_ref[1:2, :])
    tail = jnp.log(1.0 + jnp.exp(-jnp.abs(pre)))
    sp = jnp.maximum(pre, 0.0) + tail
    log_sig = jnp.minimum(pre, 0.0) - tail
    is_logf = (j == 1) | (j == 3)
    is_a = (j == 4) | (j == 6)
    is_beta = (j == 5) | (j == 7)
    y = jnp.where(is_logf, log_sig, pre)
    y = jnp.where(is_a, a_coef * sp, y)
    y = jnp.where(is_beta, _sigmoid(pre), y)
    y_scr[...] = y
    is_prefix = (j == 1) | (j == 4)
    is_suffix = (j == 3) | (j == 6)

    def cumsum_pass(l, lanes_sel):
        t = lax.broadcasted_iota(jnp.int32, (l, l), 0)
        u = lax.broadcasted_iota(jnp.int32, (l, l), 1)
        tri = (t >= u).astype(BF16)

        per_iter = min(4, s // l)

        def body(it, carry):
            starts = [pl.multiple_of((it * per_iter + u) * l, l) for u in range(per_iter)]
            ycs = [y_scr[pl.ds(r0, l), :] for r0 in starts]
            pres = [_split_dot(tri, yc) for yc in ycs]
            for r0, yc, pre_c in zip(starts, ycs, pres):
                tot = pre_c[l - 1:l, :]
                suf_c = tot - pre_c + yc
                out = jnp.where(is_prefix, pre_c, jnp.where(is_suffix, suf_c, yc))
                cur = o_scr[pl.ds(r0, l), :]
                o_scr[pl.ds(r0, l), :] = jnp.where(lanes_sel, out, cur)
            return carry

        lax.fori_loop(0, s // (l * per_iter), body, 0)

    o_scr[...] = y
    cumsum_pass(lm, j < 4)
    cumsum_pass(lg, j >= 4)

    n_rows = rows_ref.shape[1]
    blk = min(s, 256)

    def emit(i, carry):
        r0 = pl.multiple_of(i * blk, blk)
        o = o_scr[pl.ds(r0, blk), :]
        for grp in range(cols_ref.shape[1]):
            shift = (LANES - lanes_per_group * grp) % LANES
            cols_ref[0, grp, pl.ds(r0, blk), :] = o if shift == 0 else pltpu.roll(o, shift, axis=1)
        for r in range(0, blk, LANES):
            rows_ref[0, :, pl.ds(r0 + r, LANES)] = o[r:r + LANES, :].T[:n_rows, :]
        return carry

    lax.fori_loop(0, s // blk, emit, 0)


def _gate_prep(gates_raw, params, groups):
    b, s, _ = gates_raw.shape
    n_rows = 8 * HEADS
    return pl.pallas_call(
        functools.partial(_gate_prep_kernel, lm=M_CHUNK, lg=G_CHUNK, lanes_per_group=n_rows // groups),
        grid=(b,),
        in_specs=[pl.BlockSpec((1, s, LANES), lambda i: (i, 0, 0)),
                  pl.BlockSpec((8, LANES), lambda i: (0, 0))],
        out_specs=[pl.BlockSpec((1, groups, s, LANES), lambda i: (i, 0, 0, 0)),
                   pl.BlockSpec((1, n_rows, s), lambda i: (i, 0, 0))],
        out_shape=[jax.ShapeDtypeStruct((b, groups, s, LANES), F32),
                   jax.ShapeDtypeStruct((b, n_rows, s), F32)],
        scratch_shapes=[pltpu.VMEM((s, LANES), F32), pltpu.VMEM((s, LANES), F32)],
        compiler_params=_params("parallel"),
        name="gate_prep",
    )(gates_raw, params)


def _chunk_rows(rows_t, first, l):
    b, _, _, s = rows_t.shape
    rows = rows_t[:, :, first:first + 4].reshape(b, HEADS, 4, s // l, l).transpose(0, 1, 3, 2, 4)
    return jnp.pad(rows, ((0, 0), (0, 0), (0, 0), (0, 4), (0, 0)))


def _mlstm_kernel(q_ref, k_ref, v_ref, og_ref, gcol_ref, grow_ref, nw_ref, o_ref,
                  qt_scr, vt_scr, h_scr, c_scr, m_scr, *, l, nc, hb, cu):
    d = HEAD_DIM
    da = vt_scr.shape[2]
    scale = d ** -0.5
    s_idx = lax.broadcasted_iota(jnp.int32, (l, l), 0)
    t_idx = lax.broadcasted_iota(jnp.int32, (l, l), 1)
    ones_rows = (lax.broadcasted_iota(jnp.int32, (da - d, l), 0) == 0).astype(BF16)

    for hd in range(hb):
        for c in range(nc):
            qt_scr[hd, c] = q_ref[0, c * l:(c + 1) * l, hd * d:(hd + 1) * d].astype(F32).T.astype(BF16)
            vt = v_ref[0, c * l:(c + 1) * l, hd * d:(hd + 1) * d].astype(F32).T.astype(BF16)
            vt_scr[hd, c] = jnp.concatenate([vt, ones_rows], axis=0)

    def gate_idx(dr):
        return (M_X_F, M_C_F) if dr == 0 else (M_X_B, M_C_B)

    def chunk_of(dr, step):
        return step if dr == 0 else nc - 1 - step

    def total_logf(b_row, dr):
        return b_row[:, l - 1:l] if dr == 0 else b_row[:, 0:1]

    row_id = lax.broadcasted_iota(jnp.int32, (8 * nc, 1), 0) & 7
    for hd in range(hb):
        rows_all = grow_ref[0, hd].reshape(8 * nc, l)
        b_up = pltpu.roll(rows_all, 8 * nc - 1, axis=0)
        g_col = jnp.where(row_id == M_X_F, b_up[:, l - 1:l], b_up[:, 0:1])
        a_max = jnp.max(g_col - b_up + rows_all, axis=-1, keepdims=True)
        g_rep = jnp.broadcast_to(g_col, (8 * nc, l))
        a_rep = jnp.broadcast_to(a_max, (8 * nc, l))
        for dr in range(2):
            xi, _ = gate_idx(dr)
            m = jnp.zeros((1, l), F32)
            for step in range(nc):
                c = chunk_of(dr, step)
                r = 8 * c + xi
                m_scr[2 * hd + dr, c] = jnp.broadcast_to(m, (8, l))
                m = jnp.maximum(g_rep[r:r + 1, :] + m, a_rep[r:r + 1, :])

    c_scr[...] = jnp.zeros_like(c_scr)

    def body(it, carry):
        probs = []
        for k_step in range(cu):
            for hd in range(hb):
                for dr in range(2):
                    probs.append(dict(hd=hd, dr=dr, c=chunk_of(dr, it * cu + k_step)))
        for pr in probs:
            hd, c = pr["hd"], pr["c"]
            pr["k"] = k_ref[0, pl.ds(pl.multiple_of(c * l, l), l), hd * d:(hd + 1) * d]
            pr["qt"] = qt_scr[hd, c]
            pr["st"] = _dot(pr["k"], pr["qt"])
        for pr in probs:
            hd, dr, c = pr["hd"], pr["dr"], pr["c"]
            xi, ci = gate_idx(dr)
            cols = gcol_ref[0, 0, pl.ds(pl.multiple_of(c * l, l), l), :]
            rows = grow_ref[0, hd, c]
            lane0 = 8 * hd + M_GATE0
            c_col = cols[:, lane0 + ci:lane0 + ci + 1] - cols[:, lane0 + xi:lane0 + xi + 1]
            i_row, b_row = rows[xi:xi + 1, :], rows[ci:ci + 1, :]
            m_prev = m_scr[2 * hd + dr, c][0:1, :]
            mask = (s_idx <= t_idx) if dr == 0 else (s_idx >= t_idx)
            dmat = jnp.where(mask, b_row - c_col, -jnp.inf)
            inter = b_row + m_prev
            m_t = jnp.maximum(inter, jnp.max(dmat, axis=0, keepdims=True))
            pst = (pr["st"] * scale * jnp.exp(dmat - m_t)).astype(BF16)
            vt = vt_scr[hd, c]
            q_w = (pr["qt"].astype(F32) * (jnp.exp(inter - m_t) * scale)).astype(BF16)
            pr["vt"] = vt
            pr["rhs"] = jnp.concatenate([pst, q_w], axis=0)
            pr["floor"] = jnp.exp(-m_t)
            g = total_logf(b_row, dr)
            m_new = jnp.maximum(g + m_prev, jnp.max(g - b_row + i_row, axis=-1, keepdims=True))
            pr["decay"] = jnp.exp(g + m_prev - m_new)
            vw = (vt.astype(F32) * jnp.exp(g - b_row + i_row - m_new)).astype(BF16)
            pr["x"] = _dot(vw, pr["k"])
        cmats = {}
        for pr in probs:
            key = 2 * pr["hd"] + pr["dr"]
            cmat = cmats[key] if key in cmats else c_scr[key]
            pr["num"] = _dot(jnp.concatenate([pr["vt"], cmat.astype(BF16)], axis=1), pr["rhs"])
            cmats[key] = pr["decay"] * cmat + pr["x"]
        for key, cmat in cmats.items():
            c_scr[key] = cmat
        for pr in probs:
            hd, dr, c = pr["hd"], pr["dr"], pr["c"]
            num_aug = pr["num"]
            den = num_aug[d:d + 1, :]
            h_scr[2 * hd + dr, c] = num_aug[:d, :] / jnp.maximum(jnp.abs(den), pr["floor"])
        return carry

    lax.fori_loop(0, nc // cu, body, 0)
    for hd in range(hb):
        for c in range(nc):
            ht = h_scr[2 * hd, c] + h_scr[2 * hd + 1, c]
            ht = ht * lax.rsqrt(jnp.mean(ht * ht, axis=0, keepdims=True) + EPS)
            og = og_ref[0, c * l:(c + 1) * l, hd * d:(hd + 1) * d].astype(F32)
            o_ref[0, c * l:(c + 1) * l, hd * d:(hd + 1) * d] = (
                ht.T * nw_ref[...] * _sigmoid(og)).astype(BF16)


def _mlstm(proj, gcols, grows, norm_w):
    b, s, _ = proj.shape
    l = M_CHUNK
    nc = s // l
    hb = M_HEADS_PER_STEP
    cu = min(M_CHUNKS_PER_ITER, nc)
    d = HEAD_DIM
    blk = lambda off: pl.BlockSpec((1, s, hb * d), lambda i, h: (i, 0, off * HEADS // hb + h))
    return pl.pallas_call(
        functools.partial(_mlstm_kernel, l=l, nc=nc, hb=hb, cu=cu),
        grid=(b, HEADS // hb),
        in_specs=[blk(0), blk(1), blk(2), blk(3),
                  pl.BlockSpec((1, 1, s, LANES), lambda i, h: (i, h, 0, 0)),
                  pl.BlockSpec((1, hb, nc, 8, l), lambda i, h: (i, h, 0, 0, 0)),
                  pl.BlockSpec((1, d), lambda i, h: (0, 0))],
        out_specs=pl.BlockSpec((1, s, hb * d), lambda i, h: (i, 0, h)),
        out_shape=jax.ShapeDtypeStruct((b, s, HEADS * d), BF16),
        scratch_shapes=[pltpu.VMEM((hb, nc, d, l), BF16),
                        pltpu.VMEM((hb, nc, d + BF16_SUBLANES, l), BF16),
                        pltpu.VMEM((2 * hb, nc, d, l), F32),
                        pltpu.VMEM((2 * hb, d + BF16_SUBLANES, d), F32),
                        pltpu.VMEM((2 * hb, nc, 8, LANES), F32)],
        compiler_params=_params("parallel", "parallel"),
        name="mlstm",
    )(proj, proj, proj, proj, gcols, grows, norm_w.reshape(1, d))


def _pair_rows(rows_t, first, l):
    b, _, _, s = rows_t.shape
    rows = rows_t[:, :, first:first + 4].reshape(b, HEADS, 2, 2, s // l, l)
    rows = rows.transpose(0, 1, 4, 3, 2, 5).reshape(b, HEADS, s // l, 2, 2 * l)
    return jnp.pad(rows, ((0, 0), (0, 0), (0, 0), (0, 6), (0, 0)))


def _gdn_kernel(q_ref, k_ref, v_ref, z_ref, cw_ref, gcol_ref, grow_ref, nw_ref, o_ref,
                pad_scr, q_scr, k_scr, v_scr, kt_scr, qm_scr, n_scr, oa_scr, od_scr, st_scr,
                *, l, nc, hb, cu):
    s = q_ref.shape[1]
    d = HEAD_DIM
    scale = d ** -0.5
    half = CONV_K // 2

    for which in range(3):
        pad_scr[which, 0:8, :] = jnp.zeros((8, d), F32)
        pad_scr[which, 8 + s:16 + s, :] = jnp.zeros((8, d), F32)

    rb = min(s, 512)
    srcs = (q_ref, k_ref, v_ref)
    dsts = (q_scr, k_scr, v_scr)

    for hd in range(hb):
        def fill(i, carry, hd=hd):
            r0 = pl.multiple_of(i * rb, rb)
            for which in range(3):
                pad_scr[which, pl.ds(8 + r0, rb), :] = (
                    srcs[which][0, pl.ds(r0, rb), hd * d:(hd + 1) * d].astype(F32))
            return carry

        lax.fori_loop(0, s // rb, fill, 0)

        def block(i, carry, hd=hd):
            r0 = pl.multiple_of(i * rb, rb)
            for which in range(3):
                acc = pad_scr[which, pl.ds(r0 + 8 - half, rb), :] * cw_ref[which, hd, 0:1, :]
                for j in range(1, CONV_K):
                    acc = acc + (pad_scr[which, pl.ds(r0 + 8 - half + j, rb), :]
                                 * cw_ref[which, hd, j:j + 1, :])
                y = acc * _sigmoid(acc)
                if which < 2:
                    y = y * lax.rsqrt(jnp.sum(y * y, axis=-1, keepdims=True) + EPS)
                dsts[which][hd, pl.ds(r0, rb), :] = y.astype(BF16)
                if which == 1:
                    for cc in range(rb // l):
                        yc = y[cc * l:(cc + 1) * l, :]
                        kt_scr[hd, i * (rb // l) + cc] = jnp.concatenate([yc, yc], axis=0).T.astype(BF16)
            return carry

        lax.fori_loop(0, s // rb, block, 0)

    t_idx = lax.broadcasted_iota(jnp.int32, (l, 2 * l), 0)
    lane = lax.broadcasted_iota(jnp.int32, (l, 2 * l), 1)
    is_f = lane < l
    s_idx = jnp.where(is_f, lane, lane - l)
    incl = jnp.where(is_f, t_idx - s_idx, s_idx - t_idx) >= 0
    diag = t_idx == s_idx
    eye2 = diag.astype(F32)
    zeros_b = jnp.zeros((l, d), BF16)

    def blockdiag(x2):
        return jnp.concatenate([jnp.where(is_f, x2, 0.0), jnp.where(is_f, 0.0, x2)], axis=0).astype(BF16)

    base = 8
    shift = base.bit_length() - 1
    same_base = (t_idx >> shift) == (s_idx >> shift)
    merge_masks = []
    size = base
    while size < l:
        sh = size.bit_length() - 1
        merge_masks.append(((t_idx >> (sh + 1)) == (s_idx >> (sh + 1))) & ((t_idx >> sh) != (s_idx >> sh)))
        size *= 2

    half_cu = cu // 2

    def b_step(j):
        chains = [(hd, dr, j if dr == 0 else nc - 1 - j) for hd in range(hb) for dr in range(2)]
        states = [st_scr[2 * hd + dr] for hd, dr, _ in chains]
        rs = [_dot(qm_scr[2 * hd + dr, c], st.astype(BF16)) for (hd, dr, c), st in zip(chains, states)]
        for (hd, dr, c), state, r in zip(chains, states, rs):
            idx = 2 * hd + dr
            gam_row = grow_ref[0, hd, c][0:1, :]
            g_last = gam_row[:, l - 1:l] if dr == 0 else gam_row[:, l:l + 1]
            st_scr[idx] = jnp.exp(g_last) * state + r[:d] + n_scr[idx, c]
            od_scr[idx, pl.ds(pl.multiple_of(c * l, l), l), :] = r[d:].astype(BF16)

    def phase_a(it, b_steps):
        chunk_ids = ([it * half_cu + cc for cc in range(half_cu)]
                     + [nc - half_cu * (it + 1) + cc for cc in range(half_cu)])
        probs = [dict(c=c, hd=hd) for c in chunk_ids for hd in range(hb)]
        pending = list(b_steps)

        def slot():
            if pending:
                b_step(pending.pop(0))

        for pr in probs:
            c, hd = pr["c"], pr["hd"]
            r0 = pl.multiple_of(c * l, l)
            qb = q_scr[hd, pl.ds(r0, l), :]
            kb = k_scr[hd, pl.ds(r0, l), :]
            pr["kq"] = _dot_nt(jnp.concatenate([qb, kb], axis=0),
                               jnp.concatenate([kb, kb], axis=0))
        for pr in probs:
            c, hd = pr["c"], pr["hd"]
            cols = gcol_ref[0, 0, pl.ds(pl.multiple_of(c * l, l), l), :]
            col = lambda j: cols[:, 8 * hd + G_GATE0 + j:8 * hd + G_GATE0 + j + 1]
            gam_f, beta_f, gam_b, beta_b = col(G_C_F), col(G_X_F), col(G_C_B), col(G_X_B)
            gam_row = grow_ref[0, hd, c][0:1, :]
            e_incl = jnp.exp(jnp.where(incl, jnp.where(is_f, gam_f, gam_b) - gam_row, -jnp.inf))
            pr["attn2"] = pr["kq"][:l] * scale * e_incl
            a2 = jnp.where(is_f, beta_f, beta_b) * pr["kq"][l:] * jnp.where(diag, 0.0, e_incl)
            pr["a2"] = a2
            dblk = jnp.where(same_base, a2, 0.0)
            pr["p"] = eye2 - dblk
            pr["d2"] = _dot(dblk.astype(BF16), blockdiag(dblk))
        for pr in probs:
            res = _dot(jnp.concatenate([pr["p"], pr["d2"]], axis=0).astype(BF16), blockdiag(pr["d2"]))
            pr["p"] = pr["p"] + res[:l]
            pr["d4"] = res[l:]
        for pr in probs:
            pr["p"] = pr["p"] + _dot(pr["p"].astype(BF16), blockdiag(pr["d4"]))
        slot()
        for mask in merge_masks:
            for pr in probs:
                pr["x"] = _dot(jnp.where(mask, pr["a2"], 0.0).astype(BF16), blockdiag(pr["p"]))
            for pr in probs:
                pr["p"] = pr["p"] - _dot(pr["p"].astype(BF16), blockdiag(pr["x"]))
            slot()
        for pr in probs:
            c, hd = pr["c"], pr["hd"]
            r0 = pl.multiple_of(c * l, l)
            kb = k_scr[hd, pl.ds(r0, l), :]
            vb = v_scr[hd, pl.ds(r0, l), :]
            rows = grow_ref[0, hd, c]
            gam_row, beta_row = rows[0:1, :], rows[1:2, :]
            tk = (pr["p"] * (-beta_row * jnp.exp(gam_row))).astype(BF16)
            tv = (pr["p"] * beta_row).astype(BF16)
            pr["wk"] = _dot(tk, jnp.concatenate([jnp.concatenate([kb, zeros_b], axis=-1),
                                                 jnp.concatenate([zeros_b, kb], axis=-1)], axis=0))
            pr["wv"] = _dot(tv, jnp.concatenate([jnp.concatenate([vb, zeros_b], axis=-1),
                                                 jnp.concatenate([zeros_b, vb], axis=-1)], axis=0))
        for pr in probs:
            c, hd = pr["c"], pr["hd"]
            gam_row = grow_ref[0, hd, c][0:1, :]
            g_last = jnp.where(is_f[0:1, :], gam_row[:, l - 1:l], gam_row[:, l:l + 1])
            wk, wv = pr["wk"].astype(BF16), pr["wv"].astype(BF16)
            wu_bd = jnp.concatenate(
                [jnp.concatenate([wk[:, :d], wv[:, :d], zeros_b, zeros_b], axis=-1),
                 jnp.concatenate([zeros_b, zeros_b, wk[:, d:], wv[:, d:]], axis=-1)], axis=0)
            k_dec_t = (kt_scr[hd, c].astype(F32) * jnp.exp(g_last - gam_row)).astype(BF16)
            pr["res"] = _dot(jnp.concatenate([pr["attn2"].astype(BF16), k_dec_t], axis=0), wu_bd)
        for pr in probs:
            c, hd = pr["c"], pr["hd"]
            r0 = pl.multiple_of(c * l, l)
            q = q_scr[hd, pl.ds(r0, l), :].astype(F32)
            cols = gcol_ref[0, 0, pl.ds(r0, l), :]
            col = lambda j: cols[:, 8 * hd + G_GATE0 + j:8 * hd + G_GATE0 + j + 1]
            gam_f, gam_b = col(G_C_F), col(G_C_B)
            res = pr["res"]
            qd_f = q * (jnp.exp(gam_f) * scale) + res[:l, 0:d]
            qd_b = q * (jnp.exp(gam_b) * scale) + res[:l, 2 * d:3 * d]
            qm_scr[2 * hd, c] = jnp.concatenate([res[l:, 0:d], qd_f], axis=0).astype(BF16)
            qm_scr[2 * hd + 1, c] = jnp.concatenate([res[l:, 2 * d:3 * d], qd_b], axis=0).astype(BF16)
            n_scr[2 * hd, c] = res[l:, d:2 * d].astype(BF16)
            n_scr[2 * hd + 1, c] = res[l:, 3 * d:].astype(BF16)
            oa_scr[hd, pl.ds(r0, l), :] = res[:l, d:2 * d] + res[:l, 3 * d:]
        while pending:
            slot()

    st_scr[...] = jnp.zeros_like(st_scr)
    n_iter = nc // cu
    phase_a(0, [])

    def body_ab(it, carry):
        phase_a(it, [(it - 1) * half_cu + u for u in range(half_cu)])
        return carry

    lax.fori_loop(1, n_iter, body_ab, 0)

    def body_b(j, carry):
        b_step(j)
        return carry

    def finish_chunk(c):
        r0 = pl.multiple_of(c * l, l)
        for hd in range(hb):
            z = z_ref[0, pl.ds(r0, l), hd * d:(hd + 1) * d].astype(F32)
            o = (oa_scr[hd, pl.ds(r0, l), :] + od_scr[2 * hd, pl.ds(r0, l), :]
                 + od_scr[2 * hd + 1, pl.ds(r0, l), :])
            o_ref[0, pl.ds(r0, l), hd * d:(hd + 1) * d] = (
                _rms(o, nw_ref[...]) * (z * _sigmoid(z))).astype(BF16)

    def body_b_finish(j, carry):
        finish_chunk(j - 1)
        finish_chunk(nc - j)
        b_step(j)
        return carry

    first_finish = nc // 2 + 1
    lax.fori_loop((n_iter - 1) * half_cu, first_finish, body_b, 0)
    lax.fori_loop(first_finish, nc, body_b_finish, 0)
    finish_chunk(nc - 1)
    finish_chunk(0)


def _gdn(proj, conv_w, gcols, grows, norm_w):
    b, s, _ = proj.shape
    l = G_CHUNK
    assert 2 * l == LANES
    nc = s // l
    hb = G_HEADS_PER_STEP
    cu = min(G_CHUNKS_PER_ITER, nc)
    d = HEAD_DIM
    blk = lambda off: pl.BlockSpec((1, s, hb * d), lambda i, h: (i, 0, off * HEADS // hb + h))
    return pl.pallas_call(
        functools.partial(_gdn_kernel, l=l, nc=nc, hb=hb, cu=cu),
        grid=(b, HEADS // hb),
        in_specs=[blk(4), blk(5), blk(6), blk(7),
                  pl.BlockSpec((3, hb, 8, d), lambda i, h: (0, h, 0, 0)),
                  pl.BlockSpec((1, 1, s, LANES), lambda i, h: (i, h, 0, 0)),
                  pl.BlockSpec((1, hb, nc, 8, 2 * l), lambda i, h: (i, h, 0, 0, 0)),
                  pl.BlockSpec((1, d), lambda i, h: (0, 0))],
        out_specs=pl.BlockSpec((1, s, hb * d), lambda i, h: (i, 0, h)),
        out_shape=jax.ShapeDtypeStruct((b, s, HEADS * d), BF16),
        scratch_shapes=[pltpu.VMEM((3, s + 16, d), F32),
                        pltpu.VMEM((hb, s, d), BF16), pltpu.VMEM((hb, s, d), BF16),
                        pltpu.VMEM((hb, s, d), BF16),
                        pltpu.VMEM((hb, nc, d, 2 * l), BF16),
                        pltpu.VMEM((2 * hb, nc, l + d, d), BF16),
                        pltpu.VMEM((2 * hb, nc, d, d), BF16),
                        pltpu.VMEM((hb, s, d), F32),
                        pltpu.VMEM((2 * hb, s, d), BF16),
                        pltpu.VMEM((2 * hb, d, d), F32)],
        compiler_params=_params("parallel", "parallel"),
        name="gdn",
    )(proj, proj, proj, proj, conv_w, gcols, grows, norm_w.reshape(1, d))


def kernel(x, c, w_ada, b_ada, norm_ffn1, w_ffn1_in, w_ffn1_out, norm_mix, w_in, mlstm_gate_bias,
           gdn_a_log, gdn_dt_bias, gdn_conv_w, mlstm_out_norm, gdn_out_norm, w_branch_mlstm,
           w_branch_gdn, w_out, norm_ffn2, w_ffn2_in, w_ffn2_out, norm_final):
    b, s, d = x.shape
    depth = w_ada.shape[0]
    width = HEADS * HEAD_DIM
    ng = 4 * HEADS
    sizes = (width,) * 4 + (ng,) + (width,) * 4 + (ng,) + (d, d)
    offs = [0]
    for sz in sizes:
        offs.append(offs[-1] + sz)
    for layer in range(depth):
        mod = _ada(c, w_ada[layer], b_ada[layer]).reshape(b, 9, d)
        x = _ffn(x, mod, norm_ffn1[layer], w_ffn1_in[layer], w_ffn1_out[layer], norm_final,
                 sub=0, final=False)

        wl = w_in[layer]
        col = lambda i: wl[:, offs[i]:offs[i + 1]]
        w_big = jnp.concatenate([col(i).astype(BF16) for i in (0, 1, 2, 3, 5, 6, 7, 8, 10, 11)], axis=1)
        head_major = lambda t: t.reshape(t.shape[:-1] + (4, HEADS)).swapaxes(-1, -2)
        pad_lanes = lambda t: jnp.pad(t.reshape(t.shape[:-2] + (8 * HEADS,)),
                                      [(0, 0)] * (t.ndim - 2) + [(0, LANES - 8 * HEADS)])
        w_gates = pad_lanes(jnp.concatenate([head_major(col(4)), head_major(col(9))], axis=-1)).astype(BF16)
        proj, gates_raw = _proj(x, mod, norm_mix[layer], w_big, w_gates)

        zeros_h = jnp.zeros((HEADS,), F32)
        dt_rows = jnp.stack([gdn_dt_bias[layer][0], zeros_h, gdn_dt_bias[layer][1], zeros_h]).reshape(ng)
        alog_rows = jnp.stack([gdn_a_log[layer][0], zeros_h, gdn_a_log[layer][1], zeros_h]).reshape(ng)
        bias_row = pad_lanes(jnp.concatenate([head_major(mlstm_gate_bias[layer].reshape(ng)),
                                              head_major(dt_rows)], axis=-1))
        alog_row = pad_lanes(jnp.concatenate([jnp.zeros((HEADS, 4), F32), head_major(alog_rows)], axis=-1))
        gparams = jnp.zeros((8, LANES), F32).at[0].set(bias_row).at[1].set(alog_row)
        assert M_HEADS_PER_STEP == G_HEADS_PER_STEP
        gate_cols, gate_rows = _gate_prep(gates_raw, gparams, HEADS // M_HEADS_PER_STEP)
        gate_rows = gate_rows.reshape(b, HEADS, 8, s)
        m_rows = _chunk_rows(gate_rows, 0, M_CHUNK)
        g_rows = _pair_rows(gate_rows, 4, G_CHUNK)

        h_m = _mlstm(proj, gate_cols, m_rows, mlstm_out_norm[layer])
        conv_w = gdn_conv_w[layer].reshape(CONV_K, 3, HEADS, HEAD_DIM).transpose(1, 2, 0, 3)
        conv_w = jnp.pad(conv_w, ((0, 0), (0, 0), (0, 8 - CONV_K), (0, 0)))
        h_g = _gdn(proj, conv_w, gate_cols, g_rows, gdn_out_norm[layer])

        last = layer == depth - 1
        x = _ffn(x, mod, norm_ffn2[layer], w_ffn2_in[layer], w_ffn2_out[layer], norm_final,
                 sub=2, final=last,
                 merge=(h_m, h_g, proj, w_branch_mlstm[layer].astype(BF16),
                        w_branch_gdn[layer].astype(BF16), w_out[layer].astype(BF16)))
    return x
```

```python
import functools

import jax
import jax.numpy as jnp
from jax import lax
from jax.experimental import pallas as pl
from jax.experimental.pallas import tpu as pltpu

F32 = jnp.float32
BF16 = jnp.bfloat16

EPS = 1e-6
FFN_RES = 0.5
HEADS = 8
HEAD_DIM = 128
CONV_K = 5
LANES = 128
BF16_SUBLANES = 16
M_CHUNK = 128
M_HEADS_PER_STEP = 2
M_CHUNKS_PER_ITER = 4
G_CHUNK = 64
G_HEADS_PER_STEP = 2
G_CHUNKS_PER_ITER = 8
VMEM_LIMIT = 56 * 1024 * 1024

M_X_F, M_C_F, M_X_B, M_C_B = 0, 1, 2, 3
G_C_F, G_X_F, G_C_B, G_X_B = 0, 1, 2, 3
M_GATE0, G_GATE0 = 0, 4


def _dot(a, b):
    return jnp.dot(a, b, preferred_element_type=F32)


def _rms(x, w):
    return (x * lax.rsqrt(jnp.mean(x * x, axis=-1, keepdims=True) + EPS)) * w


def _sigmoid(x):
    return 1.0 / (1.0 + jnp.exp(-x))


def _params(*sem):
    return pltpu.CompilerParams(dimension_semantics=sem, vmem_limit_bytes=VMEM_LIMIT)


def _ada_kernel(c_ref, w_ref, b_ref, o_ref):
    c = c_ref[...]
    cs = (c * _sigmoid(c)).astype(BF16)
    o_ref[...] = _dot(cs, w_ref[...].astype(BF16)) + b_ref[...]


def _ada(c, w_ada, b_ada):
    b, d = c.shape
    n = w_ada.shape[1]
    tn = 1024
    return pl.pallas_call(
        _ada_kernel,
        grid=(n // tn,),
        in_specs=[pl.BlockSpec((b, d), lambda j: (0, 0)),
                  pl.BlockSpec((d, tn), lambda j: (0, j)),
                  pl.BlockSpec((1, tn), lambda j: (0, j))],
        out_specs=pl.BlockSpec((b, tn), lambda j: (0, j)),
        out_shape=jax.ShapeDtypeStruct((b, n), F32),
        compiler_params=_params("parallel"),
        name="ada",
    )(c, w_ada, b_ada.reshape(1, n))


def _ffn_rows(x, mod_ref, nw_ref, wg_ref, wu_ref, wo_ref, nf_ref, a_scr, *, sub, final, fc):
    shift = mod_ref[0, 3 * sub + 0:3 * sub + 1, :]
    scale = mod_ref[0, 3 * sub + 1:3 * sub + 2, :]
    gate = mod_ref[0, 3 * sub + 2:3 * sub + 3, :]
    h = (_rms(x, nw_ref[...]) * (1.0 + scale) + shift).astype(BF16)
    f = wg_ref.shape[1]
    for c0 in range(0, f, fc):
        c1 = min(c0 + fc, f)
        g = _dot(h, wg_ref[:, c0:c1])
        u = _dot(h, wu_ref[:, c0:c1])
        a_scr[:, c0:c1] = (g * _sigmoid(g) * u).astype(BF16)
    y = x + FFN_RES * gate * _dot(a_scr[...], wo_ref[...])
    if final:
        y = _rms(y, nf_ref[...])
    return y


def _ffn_kernel(x_ref, mod_ref, nw_ref, wg_ref, wu_ref, wo_ref, nf_ref, o_ref, a_scr, **static):
    o_ref[0] = _ffn_rows(x_ref[0], mod_ref, nw_ref, wg_ref, wu_ref, wo_ref, nf_ref, a_scr, **static)


def _merge_ffn_kernel(x_ref, mod_ref, hm_ref, hg_ref, mm_ref, mg_ref, wbm_ref, wbg_ref, wmo_ref,
                      nw_ref, wg_ref, wu_ref, wo_ref, nf_ref, o_ref, a_scr, **static):
    gate = mod_ref[0, 5:6, :]
    y = (_sigmoid(mm_ref[0].astype(F32)) * _dot(hm_ref[0], wbm_ref[...])
         + _sigmoid(mg_ref[0].astype(F32)) * _dot(hg_ref[0], wbg_ref[...]))
    x = x_ref[0] + gate * _dot(y.astype(BF16), wmo_ref[...])
    o_ref[0] = _ffn_rows(x, mod_ref, nw_ref, wg_ref, wu_ref, wo_ref, nf_ref, a_scr, **static)


def _ffn(x, mod, norm_w, w_in, w_out, norm_final, *, sub, final, merge=None):
    b, s, d = x.shape
    f = w_out.shape[0]
    tm = min(512, s)
    w_in = w_in.astype(BF16)
    wo = w_out.astype(BF16)
    const2 = lambda i, j: (0, 0)
    single = pl.Buffered(1)
    tok = lambda w, off: pl.BlockSpec((1, tm, w), lambda i, j: (i, j, off))
    x_specs = [tok(d, 0), pl.BlockSpec((1, 9, d), lambda i, j: (i, 0, 0))]
    ffn_specs = [pl.BlockSpec((1, d), const2),
                 pl.BlockSpec((d, f), const2, pipeline_mode=single),
                 pl.BlockSpec((d, f), lambda i, j: (0, 1), pipeline_mode=single),
                 pl.BlockSpec((f, d), const2, pipeline_mode=single),
                 pl.BlockSpec((1, d), const2)]
    ffn_args = (norm_w.reshape(1, d), w_in, w_in, wo, norm_final.reshape(1, d))
    static = dict(sub=sub, final=final, fc=512)
    if merge is None:
        body, in_specs, args = _ffn_kernel, x_specs + ffn_specs, (x, mod) + ffn_args
        name = "ffn"
    else:
        h_m, h_g, proj, w_m, w_g, w_o = merge
        width = h_m.shape[-1]
        merge_off = 8 * width // d
        wspec = lambda r: pl.BlockSpec((r, d), const2, pipeline_mode=single)
        merge_specs = [tok(width, 0), tok(width, 0), tok(d, merge_off), tok(d, merge_off + 1),
                       wspec(width), wspec(width), wspec(d)]
        body, in_specs = _merge_ffn_kernel, x_specs + merge_specs + ffn_specs
        args = (x, mod, h_m, h_g, proj, proj, w_m, w_g, w_o) + ffn_args
        name = "merge_ffn"
    return pl.pallas_call(
        functools.partial(body, **static),
        grid=(b, s // tm),
        in_specs=in_specs,
        out_specs=tok(d, 0),
        out_shape=jax.ShapeDtypeStruct((b, s, d), F32),
        scratch_shapes=[pltpu.VMEM((tm, f), BF16)],
        compiler_params=_params("parallel", "parallel"),
        name=name,
    )(*args)


def _proj_kernel(x_ref, mod_ref, nw_ref, w_ref, wgate_ref, o_ref, gate_ref, h_scr):
    @pl.when(pl.program_id(2) == 0)
    def _():
        shift = mod_ref[0, 3:4, :]
        scale = mod_ref[0, 4:5, :]
        h = (_rms(x_ref[0], nw_ref[...]) * (1.0 + scale) + shift).astype(BF16)
        h_scr[...] = h
        gate_ref[0] = _dot(h, wgate_ref[...])

    o_ref[0] = _dot(h_scr[...], w_ref[...]).astype(BF16)


def _proj(x, mod, norm_w, w_big, w_gates):
    b, s, d = x.shape
    n = w_big.shape[1]
    tm, tn = min(2048, s), 1024
    return pl.pallas_call(
        _proj_kernel,
        grid=(b, s // tm, n // tn),
        in_specs=[pl.BlockSpec((1, tm, d), lambda i, j, k: (i, j, 0)),
                  pl.BlockSpec((1, 9, d), lambda i, j, k: (i, 0, 0)),
                  pl.BlockSpec((1, d), lambda i, j, k: (0, 0)),
                  pl.BlockSpec((d, tn), lambda i, j, k: (0, k)),
                  pl.BlockSpec((d, LANES), lambda i, j, k: (0, 0))],
        out_specs=[pl.BlockSpec((1, tm, tn), lambda i, j, k: (i, j, k)),
                   pl.BlockSpec((1, tm, LANES), lambda i, j, k: (i, j, 0))],
        out_shape=[jax.ShapeDtypeStruct((b, s, n), BF16),
                   jax.ShapeDtypeStruct((b, s, LANES), F32)],
        scratch_shapes=[pltpu.VMEM((tm, d), BF16)],
        compiler_params=_params("parallel", "parallel", "arbitrary"),
        name="proj",
    )(x, mod, norm_w.reshape(1, d), w_big, w_gates)


def _split_dot(tri, y):
    hi = y.astype(BF16)
    r1 = y - hi.astype(F32)
    mid = r1.astype(BF16)
    lo = (r1 - mid.astype(F32)).astype(BF16)
    return _dot(tri, hi) + _dot(tri, mid) + _dot(tri, lo)


def _gate_prep_kernel(g_ref, p_ref, cols_ref, rows_ref, y_scr, o_scr, *, lm, lg, lanes_per_group):
    s = g_ref.shape[1]
    lane = lax.broadcasted_iota(jnp.int32, (1, LANES), 1)
    j = lane & 7
    pre = g_ref[0] + p_ref[0:1, :]
    a_coef = -jnp.exp(p_ref[1:2, :])
    tail = jnp.log(1.0 + jnp.exp(-jnp.abs(pre)))
    sp = jnp.maximum(pre, 0.0) + tail
    log_sig = jnp.minimum(pre, 0.0) - tail
    is_logf = (j == 1) | (j == 3)
    is_a = (j == 4) | (j == 6)
    is_beta = (j == 5) | (j == 7)
    y = jnp.where(is_logf, log_sig, pre)
    y = jnp.where(is_a, a_coef * sp, y)
    y = jnp.where(is_beta, _sigmoid(pre), y)
    y_scr[...] = y
    is_prefix = (j == 1) | (j == 4)
    is_suffix = (j == 3) | (j == 6)

    def cumsum_pass(l, lanes_sel):
        t = lax.broadcasted_iota(jnp.int32, (l, l), 0)
        u = lax.broadcasted_iota(jnp.int32, (l, l), 1)
        tri = (t >= u).astype(BF16)

        per_iter = min(4, s // l)

        def body(it, carry):
            starts = [pl.multiple_of((it * per_iter + u) * l, l) for u in range(per_iter)]
            ycs = [y_scr[pl.ds(r0, l), :] for r0 in starts]
            pres = [_split_dot(tri, yc) for yc in ycs]
            for r0, yc, pre_c in zip(starts, ycs, pres):
                tot = pre_c[l - 1:l, :]
                suf_c = tot - pre_c + yc
                out = jnp.where(is_prefix, pre_c, jnp.where(is_suffix, suf_c, yc))
                cur = o_scr[pl.ds(r0, l), :]
                o_scr[pl.ds(r0, l), :] = jnp.where(lanes_sel, out, cur)
            return carry

        lax.fori_loop(0, s // (l * per_iter), body, 0)

    o_scr[...] = y
    cumsum_pass(lm, j < 4)
    cumsum_pass(lg, j >= 4)

    n_rows = rows_ref.shape[1]
    blk = min(s, 256)

    def emit(i, carry):
        r0 = pl.multiple_of(i * blk, blk)
        o = o_scr[pl.ds(r0, blk), :]
        for grp in range(cols_ref.shape[1]):
            shift = (LANES - lanes_per_group * grp) % LANES
            cols_ref[0, grp, pl.ds(r0, blk), :] = o if shift == 0 else pltpu.roll(o, shift, axis=1)
        for r in range(0, blk, LANES):
            rows_ref[0, :, pl.ds(r0 + r, LANES)] = o[r:r + LANES, :].T[:n_rows, :]
        return carry

    lax.fori_loop(0, s // blk, emit, 0)


def _gate_prep(gates_raw, params, groups):
    b, s, _ = gates_raw.shape
    n_rows = 8 * HEADS
    return pl.pallas_call(
        functools.partial(_gate_prep_kernel, lm=M_CHUNK, lg=G_CHUNK, lanes_per_group=n_rows // groups),
        grid=(b,),
        in_specs=[pl.BlockSpec((1, s, LANES), lambda i: (i, 0, 0)),
                  pl.BlockSpec((8, LANES), lambda i: (0, 0))],
        out_specs=[pl.BlockSpec((1, groups, s, LANES), lambda i: (i, 0, 0, 0)),
                   pl.BlockSpec((1, n_rows, s), lambda i: (i, 0, 0))],
        out_shape=[jax.ShapeDtypeStruct((b, groups, s, LANES), F32),
                   jax.ShapeDtypeStruct((b, n_rows, s), F32)],
        scratch_shapes=[pltpu.VMEM((s, LANES), F32), pltpu.VMEM((s, LANES), F32)],
        compiler_params=_params("parallel"),
        name="gate_prep",
    )(gates_raw, params)


def _chunk_rows(rows_t, first, l):
    b, _, _, s = rows_t.shape
    rows = rows_t[:, :, first:first + 4].reshape(b, HEADS, 4, s // l, l).transpose(0, 1, 3, 2, 4)
    return jnp.pad(rows, ((0, 0), (0, 0), (0, 0), (0, 4), (0, 0)))


def _mlstm_kernel(q_ref, k_ref, v_ref, og_ref, gcol_ref, grow_ref, nw_ref, o_ref,
                  qt_scr, vt_scr, h_scr, c_scr, m_scr, *, l, nc, hb, cu):
    d = HEAD_DIM
    da = vt_scr.shape[2]
    scale = d ** -0.5
    s_idx = lax.broadcasted_iota(jnp.int32, (l, l), 0)
    t_idx = lax.broadcasted_iota(jnp.int32, (l, l), 1)
    ones_rows = (lax.broadcasted_iota(jnp.int32, (da - d, l), 0) == 0).astype(BF16)

    for hd in range(hb):
        for c in range(nc):
            qt_scr[hd, c] = q_ref[0, c * l:(c + 1) * l, hd * d:(hd + 1) * d].astype(F32).T.astype(BF16)
            vt = v_ref[0, c * l:(c + 1) * l, hd * d:(hd + 1) * d].astype(F32).T.astype(BF16)
            vt_scr[hd, c] = jnp.concatenate([vt, ones_rows], axis=0)

    def gate_idx(dr):
        return (M_X_F, M_C_F) if dr == 0 else (M_X_B, M_C_B)

    def chunk_of(dr, step):
        return step if dr == 0 else nc - 1 - step

    def total_logf(b_row, dr):
        return b_row[:, l - 1:l] if dr == 0 else b_row[:, 0:1]

    row_id = lax.broadcasted_iota(jnp.int32, (8 * nc, 1), 0) & 7
    for hd in range(hb):
        rows_all = grow_ref[0, hd].reshape(8 * nc, l)
        b_up = pltpu.roll(rows_all, 8 * nc - 1, axis=0)
        g_col = jnp.where(row_id == M_X_F, b_up[:, l - 1:l], b_up[:, 0:1])
        a_max = jnp.max(g_col - b_up + rows_all, axis=-1, keepdims=True)
        g_rep = jnp.broadcast_to(g_col, (8 * nc, l))
        a_rep = jnp.broadcast_to(a_max, (8 * nc, l))
        for dr in range(2):
            xi, _ = gate_idx(dr)
            m = jnp.zeros((1, l), F32)
            for step in range(nc):
                c = chunk_of(dr, step)
                r = 8 * c + xi
                m_scr[2 * hd + dr, c] = jnp.broadcast_to(m, (8, l))
                m = jnp.maximum(g_rep[r:r + 1, :] + m, a_rep[r:r + 1, :])

    c_scr[...] = jnp.zeros_like(c_scr)

    def body(it, carry):
        probs = []
        for k_step in range(cu):
            for hd in range(hb):
                for dr in range(2):
                    probs.append(dict(hd=hd, dr=dr, c=chunk_of(dr, it * cu + k_step)))
        for pr in probs:
            hd, c = pr["hd"], pr["c"]
            pr["k"] = k_ref[0, pl.ds(pl.multiple_of(c * l, l), l), hd * d:(hd + 1) * d]
            pr["qt"] = qt_scr[hd, c]
            pr["st"] = _dot(pr["k"], pr["qt"])
        for pr in probs:
            hd, dr, c = pr["hd"], pr["dr"], pr["c"]
            xi, ci = gate_idx(dr)
            cols = gcol_ref[0, 0, pl.ds(pl.multiple_of(c * l, l), l), :]
            rows = grow_ref[0, hd, c]
            lane0 = 8 * hd + M_GATE0
            c_col = cols[:, lane0 + ci:lane0 + ci + 1] - cols[:, lane0 + xi:lane0 + xi + 1]
            i_row, b_row = rows[xi:xi + 1, :], rows[ci:ci + 1, :]
            m_prev = m_scr[2 * hd + dr, c][0:1, :]
            mask = (s_idx <= t_idx) if dr == 0 else (s_idx >= t_idx)
            dmat = jnp.where(mask, b_row - c_col, -jnp.inf)
            inter = b_row + m_prev
            m_t = jnp.maximum(inter, jnp.max(dmat, axis=0, keepdims=True))
            pst = (pr["st"] * scale * jnp.exp(dmat - m_t)).astype(BF16)
            vt = vt_scr[hd, c]
            q_w = (pr["qt"].astype(F32) * (jnp.exp(inter - m_t) * scale)).astype(BF16)
            pr["vt"] = vt
            pr["rhs"] = jnp.concatenate([pst, q_w], axis=0)
            pr["floor"] = jnp.exp(-m_t)
            g = total_logf(b_row, dr)
            m_new = jnp.maximum(g + m_prev, jnp.max(g - b_row + i_row, axis=-1, keepdims=True))
            pr["decay"] = jnp.exp(g + m_prev - m_new)
            vw = (vt.astype(F32) * jnp.exp(g - b_row + i_row - m_new)).astype(BF16)
            pr["x"] = _dot(vw, pr["k"])
        cmats = {}
        for pr in probs:
            key = 2 * pr["hd"] + pr["dr"]
            cmat = cmats[key] if key in cmats else c_scr[key]
            pr["num"] = _dot(jnp.concatenate([pr["vt"], cmat.astype(BF16)], axis=1), pr["rhs"])
            cmats[key] = pr["decay"] * cmat + pr["x"]
        for key, cmat in cmats.items():
            c_scr[key] = cmat
        for pr in probs:
            hd, dr, c = pr["hd"], pr["dr"], pr["c"]
            num_aug = pr["num"]
            den = num_aug[d:d + 1, :]
            h_scr[2 * hd + dr, c] = num_aug[:d, :] / jnp.maximum(jnp.abs(den), pr["floor"])
        return carry

    lax.fori_loop(0, nc // cu, body, 0)
    for hd in range(hb):
        for c in range(nc):
            ht = h_scr[2 * hd, c] + h_scr[2 * hd + 1, c]
            ht = ht * lax.rsqrt(jnp.mean(ht * ht, axis=0, keepdims=True) + EPS)
            og = og_ref[0, c * l:(c + 1) * l, hd * d:(hd + 1) * d].astype(F32)
            o_ref[0, c * l:(c + 1) * l, hd * d:(hd + 1) * d] = (
                ht.T * nw_ref[...] * _sigmoid(og)).astype(BF16)


def _mlstm(proj, gcols, grows, norm_w):
    b, s, _ = proj.shape
    l = M_CHUNK
    nc = s // l
    hb = M_HEADS_PER_STEP
    cu = min(M_CHUNKS_PER_ITER, nc)
    d = HEAD_DIM
    blk = lambda off: pl.BlockSpec((1, s, hb * d), lambda i, h: (i, 0, off * HEADS // hb + h))
    return pl.pallas_call(
        functools.partial(_mlstm_kernel, l=l, nc=nc, hb=hb, cu=cu),
        grid=(b, HEADS // hb),
        in_specs=[blk(0), blk(1), blk(2), blk(3),
                  pl.BlockSpec((1, 1, s, LANES), lambda i, h: (i, h, 0, 0)),
                  pl.BlockSpec((1, hb, nc, 8, l), lambda i, h: (i, h, 0, 0, 0)),
                  pl.BlockSpec((1, d), lambda i, h: (0, 0))],
        out_specs=pl.BlockSpec((1, s, hb * d), lambda i, h: (i, 0, h)),
        out_shape=jax.ShapeDtypeStruct((b, s, HEADS * d), BF16),
        scratch_shapes=[pltpu.VMEM((hb, nc, d, l), BF16),
                        pltpu.VMEM((hb, nc, d + BF16_SUBLANES, l), BF16),
                        pltpu.VMEM((2 * hb, nc, d, l), F32),
                        pltpu.VMEM((2 * hb, d + BF16_SUBLANES, d), F32),
                        pltpu.VMEM((2 * hb, nc, 8, LANES), F32)],
        compiler_params=_params("parallel", "parallel"),
        name="mlstm",
    )(proj, proj, proj, proj, gcols, grows, norm_w.reshape(1, d))


def _pair_rows(rows_t, first, l):
    b, _, _, s = rows_t.shape
    rows = rows_t[:, :, first:first + 4].reshape(b, HEADS, 2, 2, s // l, l)
    rows = rows.transpose(0, 1, 4, 3, 2, 5).reshape(b, HEADS, s // l, 2, 2 * l)
    return jnp.pad(rows, ((0, 0), (0, 0), (0, 0), (0, 6), (0, 0)))


def _gdn_kernel(q_ref, k_ref, v_ref, z_ref, cw_ref, gcol_ref, grow_ref, nw_ref, o_ref,
                pad_scr, q_scr, k_scr, v_scr, kt_scr, qm_scr, n_scr, oa_scr, od_scr, st_scr,
                *, l, nc, hb, cu):
    s = q_ref.shape[1]
    d = HEAD_DIM
    scale = d ** -0.5
    half = CONV_K // 2

    for which in range(3):
        pad_scr[which, 0:8, :] = jnp.zeros((8, d), F32)
        pad_scr[which, 8 + s:16 + s, :] = jnp.zeros((8, d), F32)

    rb = min(s, 512)
    srcs = (q_ref, k_ref, v_ref)
    dsts = (q_scr, k_scr, v_scr)

    for hd in range(hb):
        def fill(i, carry, hd=hd):
            r0 = pl.multiple_of(i * rb, rb)
            for which in range(3):
                pad_scr[which, pl.ds(8 + r0, rb), :] = (
                    srcs[which][0, pl.ds(r0, rb), hd * d:(hd + 1) * d].astype(F32))
            return carry

        lax.fori_loop(0, s // rb, fill, 0)

        def block(i, carry, hd=hd):
            r0 = pl.multiple_of(i * rb, rb)
            for which in range(3):
                acc = pad_scr[which, pl.ds(r0 + 8 - half, rb), :] * cw_ref[which, hd, 0:1, :]
                for j in range(1, CONV_K):
                    acc = acc + (pad_scr[which, pl.ds(r0 + 8 - half + j, rb), :]
                                 * cw_ref[which, hd, j:j + 1, :])
                y = acc * _sigmoid(acc)
                if which < 2:
                    y = y * lax.rsqrt(jnp.sum(y * y, axis=-1, keepdims=True) + EPS)
                dsts[which][hd, pl.ds(r0, rb), :] = y.astype(BF16)
                if which == 1:
                    for cc in range(rb // l):
                        yc = y[cc * l:(cc + 1) * l, :]
                        kt_scr[hd, i * (rb // l) + cc] = jnp.concatenate([yc, yc], axis=0).T.astype(BF16)
            return carry

        lax.fori_loop(0, s // rb, block, 0)

    t_idx = lax.broadcasted_iota(jnp.int32, (l, 2 * l), 0)
    lane = lax.broadcasted_iota(jnp.int32, (l, 2 * l), 1)
    is_f = lane < l
    s_idx = jnp.where(is_f, lane, lane - l)
    incl = jnp.where(is_f, t_idx - s_idx, s_idx - t_idx) >= 0
    diag = t_idx == s_idx
    eye2 = diag.astype(F32)
    zeros_b = jnp.zeros((l, d), BF16)

    def blockdiag(x2):
        return jnp.concatenate([jnp.where(is_f, x2, 0.0), jnp.where(is_f, 0.0, x2)], axis=0).astype(BF16)

    base = 8
    shift = base.bit_length() - 1
    same_base = (t_idx >> shift) == (s_idx >> shift)
    merge_masks = []
    size = base
    while size < l:
        sh = size.bit_length() - 1
        merge_masks.append(((t_idx >> (sh + 1)) == (s_idx >> (sh + 1))) & ((t_idx >> sh) != (s_idx >> sh)))
        size *= 2

    half_cu = cu // 2

    def b_step(j):
        chains = [(hd, dr, j if dr == 0 else nc - 1 - j) for hd in range(hb) for dr in range(2)]
        states = [st_scr[2 * hd + dr] for hd, dr, _ in chains]
        rs = [_dot(qm_scr[2 * hd + dr, c], st.astype(BF16)) for (hd, dr, c), st in zip(chains, states)]
        for (hd, dr, c), state, r in zip(chains, states, rs):
            idx = 2 * hd + dr
            gam_row = grow_ref[0, hd, c][0:1, :]
            g_last = gam_row[:, l - 1:l] if dr == 0 else gam_row[:, l:l + 1]
            st_scr[idx] = jnp.exp(g_last) * state + r[:d] + n_scr[idx, c]
            od_scr[idx, pl.ds(pl.multiple_of(c * l, l), l), :] = r[d:].astype(BF16)

    def phase_a(it, b_steps):
        chunk_ids = ([it * half_cu + cc for cc in range(half_cu)]
                     + [nc - half_cu * (it + 1) + cc for cc in range(half_cu)])
        probs = [dict(c=c, hd=hd) for c in chunk_ids for hd in range(hb)]
        pending = list(b_steps)

        def slot():
            if pending:
                b_step(pending.pop(0))

        for pr in probs:
            c, hd = pr["c"], pr["hd"]
            r0 = pl.multiple_of(c * l, l)
            qb = q_scr[hd, pl.ds(r0, l), :]
            kb = k_scr[hd, pl.ds(r0, l), :]
            pr["kq"] = _dot(jnp.concatenate([qb, kb], axis=0), kt_scr[hd, c])
        for pr in probs:
            c, hd = pr["c"], pr["hd"]
            cols = gcol_ref[0, 0, pl.ds(pl.multiple_of(c * l, l), l), :]
            col = lambda j: cols[:, 8 * hd + G_GATE0 + j:8 * hd + G_GATE0 + j + 1]
            gam_f, beta_f, gam_b, beta_b = col(G_C_F), col(G_X_F), col(G_C_B), col(G_X_B)
            gam_row = grow_ref[0, hd, c][0:1, :]
            e_incl = jnp.exp(jnp.where(incl, jnp.where(is_f, gam_f, gam_b) - gam_row, -jnp.inf))
            pr["attn2"] = pr["kq"][:l] * scale * e_incl
            a2 = jnp.where(is_f, beta_f, beta_b) * pr["kq"][l:] * jnp.where(diag, 0.0, e_incl)
            pr["a2"] = a2
            dblk = jnp.where(same_base, a2, 0.0)
            pr["p"] = eye2 - dblk
            pr["d2"] = _dot(dblk.astype(BF16), blockdiag(dblk))
        for pr in probs:
            res = _dot(jnp.concatenate([pr["p"], pr["d2"]], axis=0).astype(BF16), blockdiag(pr["d2"]))
            pr["p"] = pr["p"] + res[:l]
            pr["d4"] = res[l:]
        for pr in probs:
            pr["p"] = pr["p"] + _dot(pr["p"].astype(BF16), blockdiag(pr["d4"]))
        slot()
        for mask in merge_masks:
            for pr in probs:
                pr["x"] = _dot(jnp.where(mask, pr["a2"], 0.0).astype(BF16), blockdiag(pr["p"]))
            for pr in probs:
                pr["p"] = pr["p"] - _dot(pr["p"].astype(BF16), blockdiag(pr["x"]))
            slot()
        for pr in probs:
            c, hd = pr["c"], pr["hd"]
            r0 = pl.multiple_of(c * l, l)
            kb = k_scr[hd, pl.ds(r0, l), :]
            vb = v_scr[hd, pl.ds(r0, l), :]
            rows = grow_ref[0, hd, c]
            gam_row, beta_row = rows[0:1, :], rows[1:2, :]
            tk = (pr["p"] * (-beta_row * jnp.exp(gam_row))).astype(BF16)
            tv = (pr["p"] * beta_row).astype(BF16)
            pr["wk"] = _dot(tk, jnp.concatenate([jnp.concatenate([kb, zeros_b], axis=-1),
                                                 jnp.concatenate([zeros_b, kb], axis=-1)], axis=0))
            pr["wv"] = _dot(tv, jnp.concatenate([jnp.concatenate([vb, zeros_b], axis=-1),
                                                 jnp.concatenate([zeros_b, vb], axis=-1)], axis=0))
        for pr in probs:
            c, hd = pr["c"], pr["hd"]
            gam_row = grow_ref[0, hd, c][0:1, :]
            g_last = jnp.where(is_f[0:1, :], gam_row[:, l - 1:l], gam_row[:, l:l + 1])
            wk, wv = pr["wk"].astype(BF16), pr["wv"].astype(BF16)
            wu_bd = jnp.concatenate(
                [jnp.concatenate([wk[:, :d], wv[:, :d], zeros_b, zeros_b], axis=-1),
                 jnp.concatenate([zeros_b, zeros_b, wk[:, d:], wv[:, d:]], axis=-1)], axis=0)
            k_dec_t = (kt_scr[hd, c].astype(F32) * jnp.exp(g_last - gam_row)).astype(BF16)
            pr["res"] = _dot(jnp.concatenate([pr["attn2"].astype(BF16), k_dec_t], axis=0), wu_bd)
        for pr in probs:
            c, hd = pr["c"], pr["hd"]
            r0 = pl.multiple_of(c * l, l)
            q = q_scr[hd, pl.ds(r0, l), :].astype(F32)
            cols = gcol_ref[0, 0, pl.ds(r0, l), :]
            col = lambda j: cols[:, 8 * hd + G_GATE0 + j:8 * hd + G_GATE0 + j + 1]
            gam_f, gam_b = col(G_C_F), col(G_C_B)
            res = pr["res"]
            qd_f = q * (jnp.exp(gam_f) * scale) + res[:l, 0:d]
            qd_b = q * (jnp.exp(gam_b) * scale) + res[:l, 2 * d:3 * d]
            qm_scr[2 * hd, c] = jnp.concatenate([res[l:, 0:d], qd_f], axis=0).astype(BF16)
            qm_scr[2 * hd + 1, c] = jnp.concatenate([res[l:, 2 * d:3 * d], qd_b], axis=0).astype(BF16)
            n_scr[2 * hd, c] = res[l:, d:2 * d].astype(BF16)
            n_scr[2 * hd + 1, c] = res[l:, 3 * d:].astype(BF16)
            oa_scr[hd, pl.ds(r0, l), :] = res[:l, d:2 * d] + res[:l, 3 * d:]
        while pending:
            slot()

    st_scr[...] = jnp.zeros_like(st_scr)
    n_iter = nc // cu
    phase_a(0, [])

    def body_ab(it, carry):
        phase_a(it, [(it - 1) * half_cu + u for u in range(half_cu)])
        return carry

    lax.fori_loop(1, n_iter, body_ab, 0)

    def body_b(j, carry):
        b_step(j)
        return carry

    def finish_chunk(c):
        r0 = pl.multiple_of(c * l, l)
        for hd in range(hb):
            z = z_ref[0, pl.ds(r0, l), hd * d:(hd + 1) * d].astype(F32)
            o = (oa_scr[hd, pl.ds(r0, l), :] + od_scr[2 * hd, pl.ds(r0, l), :]
                 + od_scr[2 * hd + 1, pl.ds(r0, l), :])
            o_ref[0, pl.ds(r0, l), hd * d:(hd + 1) * d] = (
                _rms(o, nw_ref[...]) * (z * _sigmoid(z))).astype(BF16)

    def body_b_finish(j, carry):
        finish_chunk(j - 1)
        finish_chunk(nc - j)
        b_step(j)
        return carry

    first_finish = nc // 2 + 1
    lax.fori_loop((n_iter - 1) * half_cu, first_finish, body_b, 0)
    lax.fori_loop(first_finish, nc, body_b_finish, 0)
    finish_chunk(nc - 1)
    finish_chunk(0)


def _gdn(proj, conv_w, gcols, grows, norm_w):
    b, s, _ = proj.shape
    l = G_CHUNK
    assert 2 * l == LANES
    nc = s // l
    hb = G_HEADS_PER_STEP
    cu = min(G_CHUNKS_PER_ITER, nc)
    d = HEAD_DIM
    blk = lambda off: pl.BlockSpec((1, s, hb * d), lambda i, h: (i, 0, off * HEADS // hb + h))
    return pl.pallas_call(
        functools.partial(_gdn_kernel, l=l, nc=nc, hb=hb, cu=cu),
        grid=(b, HEADS // hb),
        in_specs=[blk(4), blk(5), blk(6), blk(7),
                  pl.BlockSpec((3, hb, 8, d), lambda i, h: (0, h, 0, 0)),
                  pl.BlockSpec((1, 1, s, LANES), lambda i, h: (i, h, 0, 0)),
                  pl.BlockSpec((1, hb, nc, 8, 2 * l), lambda i, h: (i, h, 0, 0, 0)),
                  pl.BlockSpec((1, d), lambda i, h: (0, 0))],
        out_specs=pl.BlockSpec((1, s, hb * d), lambda i, h: (i, 0, h)),
        out_shape=jax.ShapeDtypeStruct((b, s, HEADS * d), BF16),
        scratch_shapes=[pltpu.VMEM((3, s + 16, d), F32),
                        pltpu.VMEM((hb, s, d), BF16), pltpu.VMEM((hb, s, d), BF16),
                        pltpu.VMEM((hb, s, d), BF16),
                        pltpu.VMEM((hb, nc, d, 2 * l), BF16),
                        pltpu.VMEM((2 * hb, nc, l + d, d), BF16),
                        pltpu.VMEM((2 * hb, nc, d, d), BF16),
                        pltpu.VMEM((hb, s, d), F32),
                        pltpu.VMEM((2 * hb, s, d), BF16),
                        pltpu.VMEM((2 * hb, d, d), F32)],
        compiler_params=_params("parallel", "parallel"),
        name="gdn",
    )(proj, proj, proj, proj, conv_w, gcols, grows, norm_w.reshape(1, d))


def kernel(x, c, w_ada, b_ada, norm_ffn1, w_ffn1_in, w_ffn1_out, norm_mix, w_in, mlstm_gate_bias,
           gdn_a_log, gdn_dt_bias, gdn_conv_w, mlstm_out_norm, gdn_out_norm, w_branch_mlstm,
           w_branch_gdn, w_out, norm_ffn2, w_ffn2_in, w_ffn2_out, norm_final):
    b, s, d = x.shape
    depth = w_ada.shape[0]
    width = HEADS * HEAD_DIM
    ng = 4 * HEADS
    sizes = (width,) * 4 + (ng,) + (width,) * 4 + (ng,) + (d, d)
    offs = [0]
    for sz in sizes:
        offs.append(offs[-1] + sz)
    for layer in range(depth):
        mod = _ada(c, w_ada[layer], b_ada[layer]).reshape(b, 9, d)
        x = _ffn(x, mod, norm_ffn1[layer], w_ffn1_in[layer], w_ffn1_out[layer], norm_final,
                 sub=0, final=False)

        wl = w_in[layer]
        col = lambda i: wl[:, offs[i]:offs[i + 1]]
        w_big = jnp.concatenate([col(i).astype(BF16) for i in (0, 1, 2, 3, 5, 6, 7, 8, 10, 11)], axis=1)
        head_major = lambda t: t.reshape(t.shape[:-1] + (4, HEADS)).swapaxes(-1, -2)
        pad_lanes = lambda t: jnp.pad(t.reshape(t.shape[:-2] + (8 * HEADS,)),
                                      [(0, 0)] * (t.ndim - 2) + [(0, LANES - 8 * HEADS)])
        w_gates = pad_lanes(jnp.concatenate([head_major(col(4)), head_major(col(9))], axis=-1)).astype(BF16)
        proj, gates_raw = _proj(x, mod, norm_mix[layer], w_big, w_gates)

        zeros_h = jnp.zeros((HEADS,), F32)
        dt_rows = jnp.stack([gdn_dt_bias[layer][0], zeros_h, gdn_dt_bias[layer][1], zeros_h]).reshape(ng)
        alog_rows = jnp.stack([gdn_a_log[layer][0], zeros_h, gdn_a_log[layer][1], zeros_h]).reshape(ng)
        bias_row = pad_lanes(jnp.concatenate([head_major(mlstm_gate_bias[layer].reshape(ng)),
                                              head_major(dt_rows)], axis=-1))
        alog_row = pad_lanes(jnp.concatenate([jnp.zeros((HEADS, 4), F32), head_major(alog_rows)], axis=-1))
        gparams = jnp.zeros((8, LANES), F32).at[0].set(bias_row).at[1].set(alog_row)
        assert M_HEADS_PER_STEP == G_HEADS_PER_STEP
        gate_cols, gate_rows = _gate_prep(gates_raw, gparams, HEADS // M_HEADS_PER_STEP)
        gate_rows = gate_rows.reshape(b, HEADS, 8, s)
        m_rows = _chunk_rows(gate_rows, 0, M_CHUNK)
        g_rows = _pair_rows(gate_rows, 4, G_CHUNK)

        h_m = _mlstm(proj, gate_cols, m_rows, mlstm_out_norm[layer])
        conv_w = gdn_conv_w[layer].reshape(CONV_K, 3, HEADS, HEAD_DIM).transpose(1, 2, 0, 3)
        conv_w = jnp.pad(conv_w, ((0, 0), (0, 0), (0, 8 - CONV_K), (0, 0)))
        h_g = _gdn(proj, conv_w, gate_cols, g_rows, gdn_out_norm[layer])

        last = layer == depth - 1
        x = _ffn(x, mod, norm_ffn2[layer], w_ffn2_in[layer], w_ffn2_out[layer], norm_final,
                 sub=2, final=last,
                 merge=(h_m, h_g, proj, w_branch_mlstm[layer].astype(BF16),
                        w_branch_gdn[layer].astype(BF16), w_out[layer].astype(BF16)))
    return x
```

```python
import functools

import jax
import jax.numpy as jnp
from jax import lax
from jax.experimental import pallas as pl
from jax.experimental.pallas import tpu as pltpu

F32 = jnp.float32
BF16 = jnp.bfloat16

EPS = 1e-6
FFN_RES = 0.5
HEADS = 8
HEAD_DIM = 128
CONV_K = 5
LANES = 128
BF16_SUBLANES = 16
M_CHUNK = 128
M_HEADS_PER_STEP = 2
M_CHUNKS_PER_ITER = 4
G_CHUNK = 64
G_HEADS_PER_STEP = 2
G_CHUNKS_PER_ITER = 8
VMEM_LIMIT = 56 * 1024 * 1024

M_X_F, M_C_F, M_X_B, M_C_B = 0, 1, 2, 3
G_C_F, G_X_F, G_C_B, G_X_B = 0, 1, 2, 3
M_GATE0, G_GATE0 = 0, 4


def _dot(a, b):
    return jnp.dot(a, b, preferred_element_type=F32)


def _rms(x, w):
    return (x * lax.rsqrt(jnp.mean(x * x, axis=-1, keepdims=True) + EPS)) * w


def _sigmoid(x):
    return 1.0 / (1.0 + jnp.exp(-x))


def _params(*sem):
    return pltpu.CompilerParams(dimension_semantics=sem, vmem_limit_bytes=VMEM_LIMIT)


def _ada_kernel(c_ref, w_ref, b_ref, o_ref):
    c = c_ref[...]
    cs = (c * _sigmoid(c)).astype(BF16)
    o_ref[...] = _dot(cs, w_ref[...].astype(BF16)) + b_ref[...]


def _ada(c, w_ada, b_ada):
    b, d = c.shape
    n = w_ada.shape[1]
    tn = 1024
    return pl.pallas_call(
        _ada_kernel,
        grid=(n // tn,),
        in_specs=[pl.BlockSpec((b, d), lambda j: (0, 0)),
                  pl.BlockSpec((d, tn), lambda j: (0, j)),
                  pl.BlockSpec((1, tn), lambda j: (0, j))],
        out_specs=pl.BlockSpec((b, tn), lambda j: (0, j)),
        out_shape=jax.ShapeDtypeStruct((b, n), F32),
        compiler_params=_params("parallel"),
        name="ada",
    )(c, w_ada, b_ada.reshape(1, n))


def _ffn_rows(x, mod_ref, nw_ref, wg_ref, wu_ref, wo_ref, nf_ref, a_scr, *, sub, final, fc):
    shift = mod_ref[0, 3 * sub + 0:3 * sub + 1, :]
    scale = mod_ref[0, 3 * sub + 1:3 * sub + 2, :]
    gate = mod_ref[0, 3 * sub + 2:3 * sub + 3, :]
    h = (_rms(x, nw_ref[...]) * (1.0 + scale) + shift).astype(BF16)
    f = wg_ref.shape[1]
    for c0 in range(0, f, fc):
        c1 = min(c0 + fc, f)
        g = _dot(h, wg_ref[:, c0:c1])
        u = _dot(h, wu_ref[:, c0:c1])
        a_scr[:, c0:c1] = (g * _sigmoid(g) * u).astype(BF16)
    y = x + FFN_RES * gate * _dot(a_scr[...], wo_ref[...])
    if final:
        y = _rms(y, nf_ref[...])
    return y


def _ffn_kernel(x_ref, mod_ref, nw_ref, wg_ref, wu_ref, wo_ref, nf_ref, o_ref, a_scr, **static):
    o_ref[0] = _ffn_rows(x_ref[0], mod_ref, nw_ref, wg_ref, wu_ref, wo_ref, nf_ref, a_scr, **static)


def _merge_ffn_kernel(x_ref, mod_ref, hm_ref, hg_ref, mm_ref, mg_ref, wbm_ref, wbg_ref, wmo_ref,
                      nw_ref, wg_ref, wu_ref, wo_ref, nf_ref, o_ref, a_scr, **static):
    gate = mod_ref[0, 5:6, :]
    y = (_sigmoid(mm_ref[0].astype(F32)) * _dot(hm_ref[0], wbm_ref[...])
         + _sigmoid(mg_ref[0].astype(F32)) * _dot(hg_ref[0], wbg_ref[...]))
    x = x_ref[0] + gate * _dot(y.astype(BF16), wmo_ref[...])
    o_ref[0] = _ffn_rows(x, mod_ref, nw_ref, wg_ref, wu_ref, wo_ref, nf_ref, a_scr, **static)


def _ffn(x, mod, norm_w, w_in, w_out, norm_final, *, sub, final, merge=None):
    b, s, d = x.shape
    f = w_out.shape[0]
    tm = min(512, s)
    w_in = w_in.astype(BF16)
    wo = w_out.astype(BF16)
    const2 = lambda i, j: (0, 0)
    single = pl.Buffered(1)
    tok = lambda w, off: pl.BlockSpec((1, tm, w), lambda i, j: (i, j, off))
    x_specs = [tok(d, 0), pl.BlockSpec((1, 9, d), lambda i, j: (i, 0, 0))]
    ffn_specs = [pl.BlockSpec((1, d), const2),
                 pl.BlockSpec((d, f), const2, pipeline_mode=single),
                 pl.BlockSpec((d, f), lambda i, j: (0, 1), pipeline_mode=single),
                 pl.BlockSpec((f, d), const2, pipeline_mode=single),
                 pl.BlockSpec((1, d), const2)]
    ffn_args = (norm_w.reshape(1, d), w_in, w_in, wo, norm_final.reshape(1, d))
    static = dict(sub=sub, final=final, fc=512)
    if merge is None:
        body, in_specs, args = _ffn_kernel, x_specs + ffn_specs, (x, mod) + ffn_args
        name = "ffn"
    else:
        h_m, h_g, proj, w_m, w_g, w_o = merge
        width = h_m.shape[-1]
        merge_off = 8 * width // d
        wspec = lambda r: pl.BlockSpec((r, d), const2, pipeline_mode=single)
        merge_specs = [tok(width, 0), tok(width, 0), tok(d, merge_off), tok(d, merge_off + 1),
                       wspec(width), wspec(width), wspec(d)]
        body, in_specs = _merge_ffn_kernel, x_specs + merge_specs + ffn_specs
        args = (x, mod, h_m, h_g, proj, proj, w_m, w_g, w_o) + ffn_args
        name = "merge_ffn"
    return pl.pallas_call(
        functools.partial(body, **static),
        grid=(b, s // tm),
        in_specs=in_specs,
        out_specs=tok(d, 0),
        out_shape=jax.ShapeDtypeStruct((b, s, d), F32),
        scratch_shapes=[pltpu.VMEM((tm, f), BF16)],
        compiler_params=_params("parallel", "parallel"),
        name=name,
    )(*args)


def _proj_kernel(x_ref, mod_ref, nw_ref, w_ref, wgate_ref, o_ref, gate_ref, h_scr):
    @pl.when(pl.program_id(2) == 0)
    def _():
        shift = mod_ref[0, 3:4, :]
        scale = mod_ref[0, 4:5, :]
        h = (_rms(x_ref[0], nw_ref[...]) * (1.0 + scale) + shift).astype(BF16)
        h_scr[...] = h
        gate_ref[0] = _dot(h, wgate_ref[...])

    o_ref[0] = _dot(h_scr[...], w_ref[...]).astype(BF16)


def _proj(x, mod, norm_w, w_big, w_gates):
    b, s, d = x.shape
    n = w_big.shape[1]
    tm, tn = min(1024, s), 2048
    return pl.pallas_call(
        _proj_kernel,
        grid=(b, s // tm, n // tn),
        in_specs=[pl.BlockSpec((1, tm, d), lambda i, j, k: (i, j, 0)),
                  pl.BlockSpec((1, 9, d), lambda i, j, k: (i, 0, 0)),
                  pl.BlockSpec((1, d), lambda i, j, k: (0, 0)),
                  pl.BlockSpec((d, tn), lambda i, j, k: (0, k)),
                  pl.BlockSpec((d, LANES), lambda i, j, k: (0, 0))],
        out_specs=[pl.BlockSpec((1, tm, tn), lambda i, j, k: (i, j, k)),
                   pl.BlockSpec((1, tm, LANES), lambda i, j, k: (i, j, 0))],
        out_shape=[jax.ShapeDtypeStruct((b, s, n), BF16),
                   jax.ShapeDtypeStruct((b, s, LANES), F32)],
        scratch_shapes=[pltpu.VMEM((tm, d), BF16)],
        compiler_params=_params("parallel", "parallel", "arbitrary"),
        name="proj",
    )(x, mod, norm_w.reshape(1, d), w_big, w_gates)


def _split_dot(tri, y):
    hi = y.astype(BF16)
    r1 = y - hi.astype(F32)
    mid = r1.astype(BF16)
    lo = (r1 - mid.astype(F32)).astype(BF16)
    return _dot(tri, hi) + _dot(tri, mid) + _dot(tri, lo)


def _gate_prep_kernel(g_ref, p_ref, cols_ref, rows_ref, y_scr, o_scr, *, lm, lg, lanes_per_group):
    s = g_ref.shape[1]
    lane = lax.broadcasted_iota(jnp.int32, (1, LANES), 1)
    j = lane & 7
    pre = g_ref[0] + p_ref[0:1, :]
    a_coef = -jnp.exp(p_ref[1:2, :])
    tail = jnp.log(1.0 + jnp.exp(-jnp.abs(pre)))
    sp = jnp.maximum(pre, 0.0) + tail
    log_sig = jnp.minimum(pre, 0.0) - tail
    is_logf = (j == 1) | (j == 3)
    is_a = (j == 4) | (j == 6)
    is_beta = (j == 5) | (j == 7)
    y = jnp.where(is_logf, log_sig, pre)
    y = jnp.where(is_a, a_coef * sp, y)
    y = jnp.where(is_beta, _sigmoid(pre), y)
    y_scr[...] = y
    is_prefix = (j == 1) | (j == 4)
    is_suffix = (j == 3) | (j == 6)

    def cumsum_pass(l, lanes_sel):
        t = lax.broadcasted_iota(jnp.int32, (l, l), 0)
        u = lax.broadcasted_iota(jnp.int32, (l, l), 1)
        tri = (t >= u).astype(BF16)

        per_iter = min(4, s // l)

        def body(it, carry):
            starts = [pl.multiple_of((it * per_iter + u) * l, l) for u in range(per_iter)]
            ycs = [y_scr[pl.ds(r0, l), :] for r0 in starts]
            pres = [_split_dot(tri, yc) for yc in ycs]
            for r0, yc, pre_c in zip(starts, ycs, pres):
                tot = pre_c[l - 1:l, :]
                suf_c = tot - pre_c + yc
                out = jnp.where(is_prefix, pre_c, jnp.where(is_suffix, suf_c, yc))
                cur = o_scr[pl.ds(r0, l), :]
                o_scr[pl.ds(r0, l), :] = jnp.where(lanes_sel, out, cur)
            return carry

        lax.fori_loop(0, s // (l * per_iter), body, 0)

    o_scr[...] = y
    cumsum_pass(lm, j < 4)
    cumsum_pass(lg, j >= 4)

    n_rows = rows_ref.shape[1]
    blk = min(s, 256)

    def emit(i, carry):
        r0 = pl.multiple_of(i * blk, blk)
        o = o_scr[pl.ds(r0, blk), :]
        for grp in range(cols_ref.shape[1]):
            shift = (LANES - lanes_per_group * grp) % LANES
            cols_ref[0, grp, pl.ds(r0, blk), :] = o if shift == 0 else pltpu.roll(o, shift, axis=1)
        for r in range(0, blk, LANES):
            rows_ref[0, :, pl.ds(r0 + r, LANES)] = o[r:r + LANES, :].T[:n_rows, :]
        return carry

    lax.fori_loop(0, s // blk, emit, 0)


def _gate_prep(gates_raw, params, groups):
    b, s, _ = gates_raw.shape
    n_rows = 8 * HEADS
    return pl.pallas_call(
        functools.partial(_gate_prep_kernel, lm=M_CHUNK, lg=G_CHUNK, lanes_per_group=n_rows // groups),
        grid=(b,),
        in_specs=[pl.BlockSpec((1, s, LANES), lambda i: (i, 0, 0)),
                  pl.BlockSpec((8, LANES), lambda i: (0, 0))],
        out_specs=[pl.BlockSpec((1, groups, s, LANES), lambda i: (i, 0, 0, 0)),
                   pl.BlockSpec((1, n_rows, s), lambda i: (i, 0, 0))],
        out_shape=[jax.ShapeDtypeStruct((b, groups, s, LANES), F32),
                   jax.ShapeDtypeStruct((b, n_rows, s), F32)],
        scratch_shapes=[pltpu.VMEM((s, LANES), F32), pltpu.VMEM((s, LANES), F32)],
        compiler_params=_params("parallel"),
        name="gate_prep",
    )(gates_raw, params)


def _chunk_rows(rows_t, first, l):
    b, _, _, s = rows_t.shape
    rows = rows_t[:, :, first:first + 4].reshape(b, HEADS, 4, s // l, l).transpose(0, 1, 3, 2, 4)
    return jnp.pad(rows, ((0, 0), (0, 0), (0, 0), (0, 4), (0, 0)))


def _mlstm_kernel(q_ref, k_ref, v_ref, og_ref, gcol_ref, grow_ref, nw_ref, o_ref,
                  qt_scr, vt_scr, h_scr, c_scr, m_scr, *, l, nc, hb, cu):
    d = HEAD_DIM
    da = vt_scr.shape[2]
    scale = d ** -0.5
    s_idx = lax.broadcasted_iota(jnp.int32, (l, l), 0)
    t_idx = lax.broadcasted_iota(jnp.int32, (l, l), 1)
    ones_rows = (lax.broadcasted_iota(jnp.int32, (da - d, l), 0) == 0).astype(BF16)

    for hd in range(hb):
        for c in range(nc):
            qt_scr[hd, c] = q_ref[0, c * l:(c + 1) * l, hd * d:(hd + 1) * d].astype(F32).T.astype(BF16)
            vt = v_ref[0, c * l:(c + 1) * l, hd * d:(hd + 1) * d].astype(F32).T.astype(BF16)
            vt_scr[hd, c] = jnp.concatenate([vt, ones_rows], axis=0)

    def gate_idx(dr):
        return (M_X_F, M_C_F) if dr == 0 else (M_X_B, M_C_B)

    def chunk_of(dr, step):
        return step if dr == 0 else nc - 1 - step

    def total_logf(b_row, dr):
        return b_row[:, l - 1:l] if dr == 0 else b_row[:, 0:1]

    row_id = lax.broadcasted_iota(jnp.int32, (8 * nc, 1), 0) & 7
    for hd in range(hb):
        rows_all = grow_ref[0, hd].reshape(8 * nc, l)
        b_up = pltpu.roll(rows_all, 8 * nc - 1, axis=0)
        g_col = jnp.where(row_id == M_X_F, b_up[:, l - 1:l], b_up[:, 0:1])
        a_max = jnp.max(g_col - b_up + rows_all, axis=-1, keepdims=True)
        g_rep = jnp.broadcast_to(g_col, (8 * nc, l))
        a_rep = jnp.broadcast_to(a_max, (8 * nc, l))
        for dr in range(2):
            xi, _ = gate_idx(dr)
            m = jnp.zeros((1, l), F32)
            for step in range(nc):
                c = chunk_of(dr, step)
                r = 8 * c + xi
                m_scr[2 * hd + dr, c] = jnp.broadcast_to(m, (8, l))
                m = jnp.maximum(g_rep[r:r + 1, :] + m, a_rep[r:r + 1, :])

    c_scr[...] = jnp.zeros_like(c_scr)

    def body(it, carry):
        probs = []
        for k_step in range(cu):
            for hd in range(hb):
                for dr in range(2):
                    probs.append(dict(hd=hd, dr=dr, c=chunk_of(dr, it * cu + k_step)))
        for pr in probs:
            hd, c = pr["hd"], pr["c"]
            pr["k"] = k_ref[0, pl.ds(pl.multiple_of(c * l, l), l), hd * d:(hd + 1) * d]
            pr["qt"] = qt_scr[hd, c]
            pr["st"] = _dot(pr["k"], pr["qt"])
        for pr in probs:
            hd, dr, c = pr["hd"], pr["dr"], pr["c"]
            xi, ci = gate_idx(dr)
            cols = gcol_ref[0, 0, pl.ds(pl.multiple_of(c * l, l), l), :]
            rows = grow_ref[0, hd, c]
            lane0 = 8 * hd + M_GATE0
            c_col = cols[:, lane0 + ci:lane0 + ci + 1] - cols[:, lane0 + xi:lane0 + xi + 1]
            i_row, b_row = rows[xi:xi + 1, :], rows[ci:ci + 1, :]
            m_prev = m_scr[2 * hd + dr, c][0:1, :]
            mask = (s_idx <= t_idx) if dr == 0 else (s_idx >= t_idx)
            dmat = jnp.where(mask, b_row - c_col, -jnp.inf)
            inter = b_row + m_prev
            m_t = jnp.maximum(inter, jnp.max(dmat, axis=0, keepdims=True))
            pst = (pr["st"] * scale * jnp.exp(dmat - m_t)).astype(BF16)
            vt = vt_scr[hd, c]
            q_w = (pr["qt"].astype(F32) * (jnp.exp(inter - m_t) * scale)).astype(BF16)
            pr["vt"] = vt
            pr["rhs"] = jnp.concatenate([pst, q_w], axis=0)
            pr["floor"] = jnp.exp(-m_t)
            g = total_logf(b_row, dr)
            m_new = jnp.maximum(g + m_prev, jnp.max(g - b_row + i_row, axis=-1, keepdims=True))
            pr["decay"] = jnp.exp(g + m_prev - m_new)
            vw = (vt.astype(F32) * jnp.exp(g - b_row + i_row - m_new)).astype(BF16)
            pr["x"] = _dot(vw, pr["k"])
        cmats = {}
        for pr in probs:
            key = 2 * pr["hd"] + pr["dr"]
            cmat = cmats[key] if key in cmats else c_scr[key]
            pr["num"] = _dot(jnp.concatenate([pr["vt"], cmat.astype(BF16)], axis=1), pr["rhs"])
            cmats[key] = pr["decay"] * cmat + pr["x"]
        for key, cmat in cmats.items():
            c_scr[key] = cmat
        for pr in probs:
            hd, dr, c = pr["hd"], pr["dr"], pr["c"]
            num_aug = pr["num"]
            den = num_aug[d:d + 1, :]
            h_scr[2 * hd + dr, c] = num_aug[:d, :] / jnp.maximum(jnp.abs(den), pr["floor"])
        return carry

    lax.fori_loop(0, nc // cu, body, 0)
    for hd in range(hb):
        for c in range(nc):
            ht = h_scr[2 * hd, c] + h_scr[2 * hd + 1, c]
            ht = ht * lax.rsqrt(jnp.mean(ht * ht, axis=0, keepdims=True) + EPS)
            og = og_ref[0, c * l:(c + 1) * l, hd * d:(hd + 1) * d].astype(F32)
            o_ref[0, c * l:(c + 1) * l, hd * d:(hd + 1) * d] = (
                ht.T * nw_ref[...] * _sigmoid(og)).astype(BF16)


def _mlstm(proj, gcols, grows, norm_w):
    b, s, _ = proj.shape
    l = M_CHUNK
    nc = s // l
    hb = M_HEADS_PER_STEP
    cu = min(M_CHUNKS_PER_ITER, nc)
    d = HEAD_DIM
    blk = lambda off: pl.BlockSpec((1, s, hb * d), lambda i, h: (i, 0, off * HEADS // hb + h))
    return pl.pallas_call(
        functools.partial(_mlstm_kernel, l=l, nc=nc, hb=hb, cu=cu),
        grid=(b, HEADS // hb),
        in_specs=[blk(0), blk(1), blk(2), blk(3),
                  pl.BlockSpec((1, 1, s, LANES), lambda i, h: (i, h, 0, 0)),
                  pl.BlockSpec((1, hb, nc, 8, l), lambda i, h: (i, h, 0, 0, 0)),
                  pl.BlockSpec((1, d), lambda i, h: (0, 0))],
        out_specs=pl.BlockSpec((1, s, hb * d), lambda i, h: (i, 0, h)),
        out_shape=jax.ShapeDtypeStruct((b, s, HEADS * d), BF16),
        scratch_shapes=[pltpu.VMEM((hb, nc, d, l), BF16),
                        pltpu.VMEM((hb, nc, d + BF16_SUBLANES, l), BF16),
                        pltpu.VMEM((2 * hb, nc, d, l), F32),
                        pltpu.VMEM((2 * hb, d + BF16_SUBLANES, d), F32),
                        pltpu.VMEM((2 * hb, nc, 8, LANES), F32)],
        compiler_params=_params("parallel", "parallel"),
        name="mlstm",
    )(proj, proj, proj, proj, gcols, grows, norm_w.reshape(1, d))


def _pair_rows(rows_t, first, l):
    b, _, _, s = rows_t.shape
    rows = rows_t[:, :, first:first + 4].reshape(b, HEADS, 2, 2, s // l, l)
    rows = rows.transpose(0, 1, 4, 3, 2, 5).reshape(b, HEADS, s // l, 2, 2 * l)
    return jnp.pad(rows, ((0, 0), (0, 0), (0, 0), (0, 6), (0, 0)))


def _gdn_kernel(q_ref, k_ref, v_ref, z_ref, cw_ref, gcol_ref, grow_ref, nw_ref, o_ref,
                pad_scr, q_scr, k_scr, v_scr, kt_scr, qm_scr, n_scr, oa_scr, od_scr, st_scr,
                *, l, nc, hb, cu):
    s = q_ref.shape[1]
    d = HEAD_DIM
    scale = d ** -0.5
    half = CONV_K // 2

    for which in range(3):
        pad_scr[which, 0:8, :] = jnp.zeros((8, d), F32)
        pad_scr[which, 8 + s:16 + s, :] = jnp.zeros((8, d), F32)

    rb = min(s, 512)
    srcs = (q_ref, k_ref, v_ref)
    dsts = (q_scr, k_scr, v_scr)

    for hd in range(hb):
        def fill(i, carry, hd=hd):
            r0 = pl.multiple_of(i * rb, rb)
            for which in range(3):
                pad_scr[which, pl.ds(8 + r0, rb), :] = (
                    srcs[which][0, pl.ds(r0, rb), hd * d:(hd + 1) * d].astype(F32))
            return carry

        lax.fori_loop(0, s // rb, fill, 0)

        def block(i, carry, hd=hd):
            r0 = pl.multiple_of(i * rb, rb)
            for which in range(3):
                acc = pad_scr[which, pl.ds(r0 + 8 - half, rb), :] * cw_ref[which, hd, 0:1, :]
                for j in range(1, CONV_K):
                    acc = acc + (pad_scr[which, pl.ds(r0 + 8 - half + j, rb), :]
                                 * cw_ref[which, hd, j:j + 1, :])
                y = acc * _sigmoid(acc)
                if which < 2:
                    y = y * lax.rsqrt(jnp.sum(y * y, axis=-1, keepdims=True) + EPS)
                dsts[which][hd, pl.ds(r0, rb), :] = y.astype(BF16)
                if which == 1:
                    for cc in range(rb // l):
                        yc = y[cc * l:(cc + 1) * l, :]
                        kt_scr[hd, i * (rb // l) + cc] = jnp.concatenate([yc, yc], axis=0).T.astype(BF16)
            return carry

        lax.fori_loop(0, s // rb, block, 0)

    t_idx = lax.broadcasted_iota(jnp.int32, (l, 2 * l), 0)
    lane = lax.broadcasted_iota(jnp.int32, (l, 2 * l), 1)
    is_f = lane < l
    s_idx = jnp.where(is_f, lane, lane - l)
    incl = jnp.where(is_f, t_idx - s_idx, s_idx - t_idx) >= 0
    diag = t_idx == s_idx
    eye2 = diag.astype(F32)
    zeros_b = jnp.zeros((l, d), BF16)

    def blockdiag(x2):
        return jnp.concatenate([jnp.where(is_f, x2, 0.0), jnp.where(is_f, 0.0, x2)], axis=0).astype(BF16)

    base = 8
    shift = base.bit_length() - 1
    same_base = (t_idx >> shift) == (s_idx >> shift)
    merge_masks = []
    size = base
    while size < l:
        sh = size.bit_length() - 1
        merge_masks.append(((t_idx >> (sh + 1)) == (s_idx >> (sh + 1))) & ((t_idx >> sh) != (s_idx >> sh)))
        size *= 2

    half_cu = cu // 2

    def b_step(j):
        chains = [(hd, dr, j if dr == 0 else nc - 1 - j) for hd in range(hb) for dr in range(2)]
        states = [st_scr[2 * hd + dr] for hd, dr, _ in chains]
        rs = [_dot(qm_scr[2 * hd + dr, c], st.astype(BF16)) for (hd, dr, c), st in zip(chains, states)]
        for (hd, dr, c), state, r in zip(chains, states, rs):
            idx = 2 * hd + dr
            gam_row = grow_ref[0, hd, c][0:1, :]
            g_last = gam_row[:, l - 1:l] if dr == 0 else gam_row[:, l:l + 1]
            st_scr[idx] = jnp.exp(g_last) * state + r[:d] + n_scr[idx, c]
            od_scr[idx, pl.ds(pl.multiple_of(c * l, l), l), :] = r[d:].astype(BF16)

    def phase_a(it, b_steps):
        chunk_ids = ([it * half_cu + cc for cc in range(half_cu)]
                     + [nc - half_cu * (it + 1) + cc for cc in range(half_cu)])
        probs = [dict(c=c, hd=hd) for c in chunk_ids for hd in range(hb)]
        pending = list(b_steps)

        def slot():
            if pending:
                b_step(pending.pop(0))

        for pr in probs:
            c, hd = pr["c"], pr["hd"]
            r0 = pl.multiple_of(c * l, l)
            qb = q_scr[hd, pl.ds(r0, l), :]
            kb = k_scr[hd, pl.ds(r0, l), :]
            pr["kq"] = _dot(jnp.concatenate([qb, kb], axis=0), kt_scr[hd, c])
        for pr in probs:
            c, hd = pr["c"], pr["hd"]
            cols = gcol_ref[0, 0, pl.ds(pl.multiple_of(c * l, l), l), :]
            col = lambda j: cols[:, 8 * hd + G_GATE0 + j:8 * hd + G_GATE0 + j + 1]
            gam_f, beta_f, gam_b, beta_b = col(G_C_F), col(G_X_F), col(G_C_B), col(G_X_B)
            gam_row = grow_ref[0, hd, c][0:1, :]
            e_incl = jnp.exp(jnp.where(incl, jnp.where(is_f, gam_f, gam_b) - gam_row, -jnp.inf))
            pr["attn2"] = pr["kq"][:l] * scale * e_incl
            a2 = jnp.where(is_f, beta_f, beta_b) * pr["kq"][l:] * jnp.where(diag, 0.0, e_incl)
            pr["a2"] = a2
            dblk = jnp.where(same_base, a2, 0.0)
            pr["p"] = eye2 - dblk
            pr["d2"] = _dot(dblk.astype(BF16), blockdiag(dblk))
        for pr in probs:
            res = _dot(jnp.concatenate([pr["p"], pr["d2"]], axis=0).astype(BF16), blockdiag(pr["d2"]))
            pr["p"] = pr["p"] + res[:l]
            pr["d4"] = res[l:]
        for pr in probs:
            pr["p"] = pr["p"] + _dot(pr["p"].astype(BF16), blockdiag(pr["d4"]))
        slot()
        for mask in merge_masks:
            for pr in probs:
                pr["x"] = _dot(jnp.where(mask, pr["a2"], 0.0).astype(BF16), blockdiag(pr["p"]))
            for pr in probs:
                pr["p"] = pr["p"] - _dot(pr["p"].astype(BF16), blockdiag(pr["x"]))
            slot()
        for pr in probs:
            c, hd = pr["c"], pr["hd"]
            r0 = pl.multiple_of(c * l, l)
            kb = k_scr[hd, pl.ds(r0, l), :]
            vb = v_scr[hd, pl.ds(r0, l), :]
            rows = grow_ref[0, hd, c]
            gam_row, beta_row = rows[0:1, :], rows[1:2, :]
            tk = (pr["p"] * (-beta_row * jnp.exp(gam_row))).astype(BF16)
            tv = (pr["p"] * beta_row).astype(BF16)
            pr["wk"] = _dot(tk, jnp.concatenate([jnp.concatenate([kb, zeros_b], axis=-1),
                                                 jnp.concatenate([zeros_b, kb], axis=-1)], axis=0))
            pr["wv"] = _dot(tv, jnp.concatenate([jnp.concatenate([vb, zeros_b], axis=-1),
                                                 jnp.concatenate([zeros_b, vb], axis=-1)], axis=0))
        for pr in probs:
            c, hd = pr["c"], pr["hd"]
            gam_row = grow_ref[0, hd, c][0:1, :]
            g_last = jnp.where(is_f[0:1, :], gam_row[:, l - 1:l], gam_row[:, l:l + 1])
            wk, wv = pr["wk"].astype(BF16), pr["wv"].astype(BF16)
            wu_bd = jnp.concatenate(
                [jnp.concatenate([wk[:, :d], wv[:, :d], zeros_b, zeros_b], axis=-1),
                 jnp.concatenate([zeros_b, zeros_b, wk[:, d:], wv[:, d:]], axis=-1)], axis=0)
            k_dec_t = (kt_scr[hd, c].astype(F32) * jnp.exp(g_last - gam_row)).astype(BF16)
            pr["res"] = _dot(jnp.concatenate([pr["attn2"].astype(BF16), k_dec_t], axis=0), wu_bd)
        for pr in probs:
            c, hd = pr["c"], pr["hd"]
            r0 = pl.multiple_of(c * l, l)
            q = q_scr[hd, pl.ds(r0, l), :].astype(F32)
            cols = gcol_ref[0, 0, pl.ds(r0, l), :]
            col = lambda j: cols[:, 8 * hd + G_GATE0 + j:8 * hd + G_GATE0 + j + 1]
            gam_f, gam_b = col(G_C_F), col(G_C_B)
            res = pr["res"]
            qd_f = q * (jnp.exp(gam_f) * scale) + res[:l, 0:d]
            qd_b = q * (jnp.exp(gam_b) * scale) + res[:l, 2 * d:3 * d]
            qm_scr[2 * hd, c] = jnp.concatenate([res[l:, 0:d], qd_f], axis=0).astype(BF16)
            qm_scr[2 * hd + 1, c] = jnp.concatenate([res[l:, 2 * d:3 * d], qd_b], axis=0).astype(BF16)
            n_scr[2 * hd, c] = res[l:, d:2 * d].astype(BF16)
            n_scr[2 * hd + 1, c] = res[l:, 3 * d:].astype(BF16)
            oa_scr[hd, pl.ds(r0, l), :] = res[:l, d:2 * d] + res[:l, 3 * d:]
        while pending:
            slot()

    st_scr[...] = jnp.zeros_like(st_scr)
    n_iter = nc // cu
    phase_a(0, [])

    def body_ab(it, carry):
        phase_a(it, [(it - 1) * half_cu + u for u in range(half_cu)])
        return carry

    lax.fori_loop(1, n_iter, body_ab, 0)

    def body_b(j, carry):
        b_step(j)
        return carry

    def finish_chunk(c):
        r0 = pl.multiple_of(c * l, l)
        for hd in range(hb):
            z = z_ref[0, pl.ds(r0, l), hd * d:(hd + 1) * d].astype(F32)
            o = (oa_scr[hd, pl.ds(r0, l), :] + od_scr[2 * hd, pl.ds(r0, l), :]
                 + od_scr[2 * hd + 1, pl.ds(r0, l), :])
            o_ref[0, pl.ds(r0, l), hd * d:(hd + 1) * d] = (
                _rms(o, nw_ref[...]) * (z * _sigmoid(z))).astype(BF16)

    def body_b_finish(j, carry):
        finish_chunk(j - 1)
        finish_chunk(nc - j)
        b_step(j)
        return carry

    first_finish = nc // 2 + 1
    lax.fori_loop((n_iter - 1) * half_cu, first_finish, body_b, 0)
    lax.fori_loop(first_finish, nc, body_b_finish, 0)
    finish_chunk(nc - 1)
    finish_chunk(0)


def _gdn(proj, conv_w, gcols, grows, norm_w):
    b, s, _ = proj.shape
    l = G_CHUNK
    assert 2 * l == LANES
    nc = s // l
    hb = G_HEADS_PER_STEP
    cu = min(G_CHUNKS_PER_ITER, nc)
    d = HEAD_DIM
    blk = lambda off: pl.BlockSpec((1, s, hb * d), lambda i, h: (i, 0, off * HEADS // hb + h))
    return pl.pallas_call(
        functools.partial(_gdn_kernel, l=l, nc=nc, hb=hb, cu=cu),
        grid=(b, HEADS // hb),
        in_specs=[blk(4), blk(5), blk(6), blk(7),
                  pl.BlockSpec((3, hb, 8, d), lambda i, h: (0, h, 0, 0)),
                  pl.BlockSpec((1, 1, s, LANES), lambda i, h: (i, h, 0, 0)),
                  pl.BlockSpec((1, hb, nc, 8, 2 * l), lambda i, h: (i, h, 0, 0, 0)),
                  pl.BlockSpec((1, d), lambda i, h: (0, 0))],
        out_specs=pl.BlockSpec((1, s, hb * d), lambda i, h: (i, 0, h)),
        out_shape=jax.ShapeDtypeStruct((b, s, HEADS * d), BF16),
        scratch_shapes=[pltpu.VMEM((3, s + 16, d), F32),
                        pltpu.VMEM((hb, s, d), BF16), pltpu.VMEM((hb, s, d), BF16),
                        pltpu.VMEM((hb, s, d), BF16),
                        pltpu.VMEM((hb, nc, d, 2 * l), BF16),
                        pltpu.VMEM((2 * hb, nc, l + d, d), BF16),
                        pltpu.VMEM((2 * hb, nc, d, d), BF16),
                        pltpu.VMEM((hb, s, d), F32),
                        pltpu.VMEM((2 * hb, s, d), BF16),
                        pltpu.VMEM((2 * hb, d, d), F32)],
        compiler_params=_params("parallel", "parallel"),
        name="gdn",
    )(proj, proj, proj, proj, conv_w, gcols, grows, norm_w.reshape(1, d))


def kernel(x, c, w_ada, b_ada, norm_ffn1, w_ffn1_in, w_ffn1_out, norm_mix, w_in, mlstm_gate_bias,
           gdn_a_log, gdn_dt_bias, gdn_conv_w, mlstm_out_norm, gdn_out_norm, w_branch_mlstm,
           w_branch_gdn, w_out, norm_ffn2, w_ffn2_in, w_ffn2_out, norm_final):
    b, s, d = x.shape
    depth = w_ada.shape[0]
    width = HEADS * HEAD_DIM
    ng = 4 * HEADS
    sizes = (width,) * 4 + (ng,) + (width,) * 4 + (ng,) + (d, d)
    offs = [0]
    for sz in sizes:
        offs.append(offs[-1] + sz)
    for layer in range(depth):
        mod = _ada(c, w_ada[layer], b_ada[layer]).reshape(b, 9, d)
        x = _ffn(x, mod, norm_ffn1[layer], w_ffn1_in[layer], w_ffn1_out[layer], norm_final,
                 sub=0, final=False)

        wl = w_in[layer]
        col = lambda i: wl[:, offs[i]:offs[i + 1]]
        w_big = jnp.concatenate([col(i).astype(BF16) for i in (0, 1, 2, 3, 5, 6, 7, 8, 10, 11)], axis=1)
        head_major = lambda t: t.reshape(t.shape[:-1] + (4, HEADS)).swapaxes(-1, -2)
        pad_lanes = lambda t: jnp.pad(t.reshape(t.shape[:-2] + (8 * HEADS,)),
                                      [(0, 0)] * (t.ndim - 2) + [(0, LANES - 8 * HEADS)])
        w_gates = pad_lanes(jnp.concatenate([head_major(col(4)), head_major(col(9))], axis=-1)).astype(BF16)
        proj, gates_raw = _proj(x, mod, norm_mix[layer], w_big, w_gates)

        zeros_h = jnp.zeros((HEADS,), F32)
        dt_rows = jnp.stack([gdn_dt_bias[layer][0], zeros_h, gdn_dt_bias[layer][1], zeros_h]).reshape(ng)
        alog_rows = jnp.stack([gdn_a_log[layer][0], zeros_h, gdn_a_log[layer][1], zeros_h]).reshape(ng)
        bias_row = pad_lanes(jnp.concatenate([head_major(mlstm_gate_bias[layer].reshape(ng)),
                                              head_major(dt_rows)], axis=-1))
        alog_row = pad_lanes(jnp.concatenate([jnp.zeros((HEADS, 4), F32), head_major(alog_rows)], axis=-1))
        gparams = jnp.zeros((8, LANES), F32).at[0].set(bias_row).at[1].set(alog_row)
        assert M_HEADS_PER_STEP == G_HEADS_PER_STEP
        gate_cols, gate_rows = _gate_prep(gates_raw, gparams, HEADS // M_HEADS_PER_STEP)
        gate_rows = gate_rows.reshape(b, HEADS, 8, s)
        m_rows = _chunk_rows(gate_rows, 0, M_CHUNK)
        g_rows = _pair_rows(gate_rows, 4, G_CHUNK)

        h_m = _mlstm(proj, gate_cols, m_rows, mlstm_out_norm[layer])
        conv_w = gdn_conv_w[layer].reshape(CONV_K, 3, HEADS, HEAD_DIM).transpose(1, 2, 0, 3)
        conv_w = jnp.pad(conv_w, ((0, 0), (0, 0), (0, 8 - CONV_K), (0, 0)))
        h_g = _gdn(proj, conv_w, gate_cols, g_rows, gdn_out_norm[layer])

        last = layer == depth - 1
        x = _ffn(x, mod, norm_ffn2[layer], w_ffn2_in[layer], w_ffn2_out[layer], norm_final,
                 sub=2, final=last,
                 merge=(h_m, h_g, proj, w_branch_mlstm[layer].astype(BF16),
                        w_branch_gdn[layer].astype(BF16), w_out[layer].astype(BF16)))
    return x
```

```python
import functools

import jax
import jax.numpy as jnp
from jax import lax
from jax.experimental import pallas as pl
from jax.experimental.pallas import tpu as pltpu

F32 = jnp.float32
BF16 = jnp.bfloat16

EPS = 1e-6
FFN_RES = 0.5
HEADS = 8
HEAD_DIM = 128
CONV_K = 5
LANES = 128
BF16_SUBLANES = 16
M_CHUNK = 128
M_HEADS_PER_STEP = 2
M_CHUNKS_PER_ITER = 4
G_CHUNK = 64
G_HEADS_PER_STEP = 2
G_CHUNKS_PER_ITER = 8
VMEM_LIMIT = 56 * 1024 * 1024

M_X_F, M_C_F, M_X_B, M_C_B = 0, 1, 2, 3
G_C_F, G_X_F, G_C_B, G_X_B = 0, 1, 2, 3
M_GATE0, G_GATE0 = 0, 4


def _dot(a, b):
    return jnp.dot(a, b, preferred_element_type=F32)


def _rms(x, w):
    return (x * lax.rsqrt(jnp.mean(x * x, axis=-1, keepdims=True) + EPS)) * w


def _sigmoid(x):
    return 1.0 / (1.0 + jnp.exp(-x))


def _params(*sem):
    return pltpu.CompilerParams(dimension_semantics=sem, vmem_limit_bytes=VMEM_LIMIT)


def _ada_kernel(c_ref, w_ref, b_ref, o_ref):
    c = c_ref[...]
    cs = (c * _sigmoid(c)).astype(BF16)
    o_ref[...] = _dot(cs, w_ref[...].astype(BF16)) + b_ref[...]


def _ada(c, w_ada, b_ada):
    b, d = c.shape
    n = w_ada.shape[1]
    tn = 1024
    return pl.pallas_call(
        _ada_kernel,
        grid=(n // tn,),
        in_specs=[pl.BlockSpec((b, d), lambda j: (0, 0)),
                  pl.BlockSpec((d, tn), lambda j: (0, j)),
                  pl.BlockSpec((1, tn), lambda j: (0, j))],
        out_specs=pl.BlockSpec((b, tn), lambda j: (0, j)),
        out_shape=jax.ShapeDtypeStruct((b, n), F32),
        compiler_params=_params("parallel"),
        name="ada",
    )(c, w_ada, b_ada.reshape(1, n))


def _ffn_rows(x, mod_ref, nw_ref, wg_ref, wu_ref, wo_ref, nf_ref, a_scr, *, sub, final, fc):
    shift = mod_ref[0, 3 * sub + 0:3 * sub + 1, :]
    scale = mod_ref[0, 3 * sub + 1:3 * sub + 2, :]
    gate = mod_ref[0, 3 * sub + 2:3 * sub + 3, :]
    h = (_rms(x, nw_ref[...]) * (1.0 + scale) + shift).astype(BF16)
    f = wg_ref.shape[1]
    for c0 in range(0, f, fc):
        c1 = min(c0 + fc, f)
        g = _dot(h, wg_ref[:, c0:c1])
        u = _dot(h, wu_ref[:, c0:c1])
        a_scr[:, c0:c1] = (g * _sigmoid(g) * u).astype(BF16)
    y = x + FFN_RES * gate * _dot(a_scr[...], wo_ref[...])
    if final:
        y = _rms(y, nf_ref[...])
    return y


def _ffn_kernel(x_ref, mod_ref, nw_ref, wg_ref, wu_ref, wo_ref, nf_ref, o_ref, a_scr, **static):
    o_ref[0] = _ffn_rows(x_ref[0], mod_ref, nw_ref, wg_ref, wu_ref, wo_ref, nf_ref, a_scr, **static)


def _merge_ffn_kernel(x_ref, mod_ref, hm_ref, hg_ref, mm_ref, mg_ref, wbm_ref, wbg_ref, wmo_ref,
                      nw_ref, wg_ref, wu_ref, wo_ref, nf_ref, o_ref, a_scr, **static):
    gate = mod_ref[0, 5:6, :]
    y = (_sigmoid(mm_ref[0].astype(F32)) * _dot(hm_ref[0], wbm_ref[...])
         + _sigmoid(mg_ref[0].astype(F32)) * _dot(hg_ref[0], wbg_ref[...]))
    x = x_ref[0] + gate * _dot(y.astype(BF16), wmo_ref[...])
    o_ref[0] = _ffn_rows(x, mod_ref, nw_ref, wg_ref, wu_ref, wo_ref, nf_ref, a_scr, **static)


def _ffn(x, mod, norm_w, w_in, w_out, norm_final, *, sub, final, merge=None):
    b, s, d = x.shape
    f = w_out.shape[0]
    tm = min(512, s)
    w_in = w_in.astype(BF16)
    wo = w_out.astype(BF16)
    const2 = lambda i, j: (0, 0)
    single = pl.Buffered(1)
    tok = lambda w, off: pl.BlockSpec((1, tm, w), lambda i, j: (i, j, off))
    x_specs = [tok(d, 0), pl.BlockSpec((1, 9, d), lambda i, j: (i, 0, 0))]
    ffn_specs = [pl.BlockSpec((1, d), const2),
                 pl.BlockSpec((d, f), const2, pipeline_mode=single),
                 pl.BlockSpec((d, f), lambda i, j: (0, 1), pipeline_mode=single),
                 pl.BlockSpec((f, d), const2, pipeline_mode=single),
                 pl.BlockSpec((1, d), const2)]
    ffn_args = (norm_w.reshape(1, d), w_in, w_in, wo, norm_final.reshape(1, d))
    static = dict(sub=sub, final=final, fc=512)
    if merge is None:
        body, in_specs, args = _ffn_kernel, x_specs + ffn_specs, (x, mod) + ffn_args
        name = "ffn"
    else:
        h_m, h_g, proj, w_m, w_g, w_o = merge
        width = h_m.shape[-1]
        merge_off = 8 * width // d
        wspec = lambda r: pl.BlockSpec((r, d), const2, pipeline_mode=single)
        merge_specs = [tok(width, 0), tok(width, 0), tok(d, merge_off), tok(d, merge_off + 1),
                       wspec(width), wspec(width), wspec(d)]
        body, in_specs = _merge_ffn_kernel, x_specs + merge_specs + ffn_specs
        args = (x, mod, h_m, h_g, proj, proj, w_m, w_g, w_o) + ffn_args
        name = "merge_ffn"
    return pl.pallas_call(
        functools.partial(body, **static),
        grid=(b, s // tm),
        in_specs=in_specs,
        out_specs=tok(d, 0),
        out_shape=jax.ShapeDtypeStruct((b, s, d), F32),
        scratch_shapes=[pltpu.VMEM((tm, f), BF16)],
        compiler_params=_params("parallel", "parallel"),
        name=name,
    )(*args)


def _proj_kernel(x_ref, mod_ref, nw_ref, w_ref, wgate_ref, o_ref, gate_ref, h_scr):
    @pl.when(pl.program_id(2) == 0)
    def _():
        shift = mod_ref[0, 3:4, :]
        scale = mod_ref[0, 4:5, :]
        h = (_rms(x_ref[0], nw_ref[...]) * (1.0 + scale) + shift).astype(BF16)
        h_scr[...] = h
        gate_ref[0] = _dot(h, wgate_ref[...])

    o_ref[0] = _dot(h_scr[...], w_ref[...]).astype(BF16)


def _proj(x, mod, norm_w, w_big, w_gates):
    b, s, d = x.shape
    n = w_big.shape[1]
    tm, tn = min(2048, s), 1024
    return pl.pallas_call(
        _proj_kernel,
        grid=(b, s // tm, n // tn),
        in_specs=[pl.BlockSpec((1, tm, d), lambda i, j, k: (i, j, 0)),
                  pl.BlockSpec((1, 9, d), lambda i, j, k: (i, 0, 0)),
                  pl.BlockSpec((1, d), lambda i, j, k: (0, 0)),
                  pl.BlockSpec((d, tn), lambda i, j, k: (0, k)),
                  pl.BlockSpec((d, LANES), lambda i, j, k: (0, 0))],
        out_specs=[pl.BlockSpec((1, tm, tn), lambda i, j, k: (i, j, k)),
                   pl.BlockSpec((1, tm, LANES), lambda i, j, k: (i, j, 0))],
        out_shape=[jax.ShapeDtypeStruct((b, s, n), BF16),
                   jax.ShapeDtypeStruct((b, s, LANES), F32)],
        scratch_shapes=[pltpu.VMEM((tm, d), BF16)],
        compiler_params=_params("parallel", "parallel", "arbitrary"),
        name="proj",
    )(x, mod, norm_w.reshape(1, d), w_big, w_gates)


def _split_dot(tri, y):
    hi = y.astype(BF16)
    r1 = y - hi.astype(F32)
    mid = r1.astype(BF16)
    lo = (r1 - mid.astype(F32)).astype(BF16)
    return _dot(tri, hi) + _dot(tri, mid) + _dot(tri, lo)


def _gate_prep_kernel(g_ref, p_ref, cols_ref, rows_ref, y_scr, o_scr, *, lm, lg, lanes_per_group):
    s = g_ref.shape[1]
    lane = lax.broadcasted_iota(jnp.int32, (1, LANES), 1)
    j = lane & 7
    pre = g_ref[0] + p_ref[0:1, :]
    a_coef = -jnp.exp(p_ref[1:2, :])
    tail = jnp.log(1.0 + jnp.exp(-jnp.abs(pre)))
    sp = jnp.maximum(pre, 0.0) + tail
    log_sig = jnp.minimum(pre, 0.0) - tail
    is_logf = (j == 1) | (j == 3)
    is_a = (j == 4) | (j == 6)
    is_beta = (j == 5) | (j == 7)
    y = jnp.where(is_logf, log_sig, pre)
    y = jnp.where(is_a, a_coef * sp, y)
    y = jnp.where(is_beta, _sigmoid(pre), y)
    y_scr[...] = y
    is_prefix = (j == 1) | (j == 4)
    is_suffix = (j == 3) | (j == 6)

    def cumsum_pass(l, lanes_sel):
        t = lax.broadcasted_iota(jnp.int32, (l, l), 0)
        u = lax.broadcasted_iota(jnp.int32, (l, l), 1)
        tri = (t >= u).astype(BF16)

        per_iter = min(4, s // l)

        def body(it, carry):
            starts = [pl.multiple_of((it * per_iter + u) * l, l) for u in range(per_iter)]
            ycs = [y_scr[pl.ds(r0, l), :] for r0 in starts]
            pres = [_split_dot(tri, yc) for yc in ycs]
            for r0, yc, pre_c in zip(starts, ycs, pres):
                tot = pre_c[l - 1:l, :]
                suf_c = tot - pre_c + yc
                out = jnp.where(is_prefix, pre_c, jnp.where(is_suffix, suf_c, yc))
                cur = o_scr[pl.ds(r0, l), :]
                o_scr[pl.ds(r0, l), :] = jnp.where(lanes_sel, out, cur)
            return carry

        lax.fori_loop(0, s // (l * per_iter), body, 0)

    o_scr[...] = y
    cumsum_pass(lm, j < 4)
    cumsum_pass(lg, j >= 4)

    n_rows = rows_ref.shape[1]
    blk = min(s, 256)

    def emit(i, carry):
        r0 = pl.multiple_of(i * blk, blk)
        o = o_scr[pl.ds(r0, blk), :]
        for grp in range(cols_ref.shape[1]):
            shift = (LANES - lanes_per_group * grp) % LANES
            cols_ref[0, grp, pl.ds(r0, blk), :] = o if shift == 0 else pltpu.roll(o, shift, axis=1)
        for r in range(0, blk, LANES):
            rows_ref[0, :, pl.ds(r0 + r, LANES)] = o[r:r + LANES, :].T[:n_rows, :]
        return carry

    lax.fori_loop(0, s // blk, emit, 0)


def _gate_prep(gates_raw, params, groups):
    b, s, _ = gates_raw.shape
    n_rows = 8 * HEADS
    return pl.pallas_call(
        functools.partial(_gate_prep_kernel, lm=M_CHUNK, lg=G_CHUNK, lanes_per_group=n_rows // groups),
        grid=(b,),
        in_specs=[pl.BlockSpec((1, s, LANES), lambda i: (i, 0, 0)),
                  pl.BlockSpec((8, LANES), lambda i: (0, 0))],
        out_specs=[pl.BlockSpec((1, groups, s, LANES), lambda i: (i, 0, 0, 0)),
                   pl.BlockSpec((1, n_rows, s), lambda i: (i, 0, 0))],
        out_shape=[jax.ShapeDtypeStruct((b, groups, s, LANES), F32),
                   jax.ShapeDtypeStruct((b, n_rows, s), F32)],
        scratch_shapes=[pltpu.VMEM((s, LANES), F32), pltpu.VMEM((s, LANES), F32)],
        compiler_params=_params("parallel"),
        name="gate_prep",
    )(gates_raw, params)


def _chunk_rows(rows_t, first, l):
    b, _, _, s = rows_t.shape
    rows = rows_t[:, :, first:first + 4].reshape(b, HEADS, 4, s // l, l).transpose(0, 1, 3, 2, 4)
    return jnp.pad(rows, ((0, 0), (0, 0), (0, 0), (0, 4), (0, 0)))


def _mlstm_kernel(q_ref, k_ref, v_ref, og_ref, gcol_ref, grow_ref, nw_ref, o_ref,
                  qt_scr, vt_scr, h_scr, c_scr, m_scr, *, l, nc, hb, cu):
    d = HEAD_DIM
    da = vt_scr.shape[2]
    scale = d ** -0.5
    s_idx = lax.broadcasted_iota(jnp.int32, (l, l), 0)
    t_idx = lax.broadcasted_iota(jnp.int32, (l, l), 1)
    ones_rows = (lax.broadcasted_iota(jnp.int32, (da - d, l), 0) == 0).astype(BF16)

    for hd in range(hb):
        for c in range(nc):
            qt_scr[hd, c] = q_ref[0, c * l:(c + 1) * l, hd * d:(hd + 1) * d].astype(F32).T.astype(BF16)
            vt = v_ref[0, c * l:(c + 1) * l, hd * d:(hd + 1) * d].astype(F32).T.astype(BF16)
            vt_scr[hd, c] = jnp.concatenate([vt, ones_rows], axis=0)

    def gate_idx(dr):
        return (M_X_F, M_C_F) if dr == 0 else (M_X_B, M_C_B)

    def chunk_of(dr, step):
        return step if dr == 0 else nc - 1 - step

    def total_logf(b_row, dr):
        return b_row[:, l - 1:l] if dr == 0 else b_row[:, 0:1]

    row_id = lax.broadcasted_iota(jnp.int32, (8 * nc, 1), 0) & 7
    for hd in range(hb):
        rows_all = grow_ref[0, hd].reshape(8 * nc, l)
        b_up = pltpu.roll(rows_all, 8 * nc - 1, axis=0)
        g_col = jnp.where(row_id == M_X_F, b_up[:, l - 1:l], b_up[:, 0:1])
        a_max = jnp.max(g_col - b_up + rows_all, axis=-1, keepdims=True)
        g_rep = jnp.broadcast_to(g_col, (8 * nc, l))
        a_rep = jnp.broadcast_to(a_max, (8 * nc, l))
        for dr in range(2):
            xi, _ = gate_idx(dr)
            m = jnp.zeros((1, l), F32)
            for step in range(nc):
                c = chunk_of(dr, step)
                r = 8 * c + xi
                m_scr[2 * hd + dr, c] = jnp.broadcast_to(m, (8, l))
                m = jnp.maximum(g_rep[r:r + 1, :] + m, a_rep[r:r + 1, :])

    c_scr[...] = jnp.zeros_like(c_scr)

    def body(it, carry):
        probs = []
        for k_step in range(cu):
            for hd in range(hb):
                for dr in range(2):
                    probs.append(dict(hd=hd, dr=dr, c=chunk_of(dr, it * cu + k_step)))
        for pr in probs:
            hd, c = pr["hd"], pr["c"]
            pr["k"] = k_ref[0, pl.ds(pl.multiple_of(c * l, l), l), hd * d:(hd + 1) * d]
            pr["qt"] = qt_scr[hd, c]
            pr["st"] = _dot(pr["k"], pr["qt"])
        for pr in probs:
            hd, dr, c = pr["hd"], pr["dr"], pr["c"]
            xi, ci = gate_idx(dr)
            cols = gcol_ref[0, 0, pl.ds(pl.multiple_of(c * l, l), l), :]
            rows = grow_ref[0, hd, c]
            lane0 = 8 * hd + M_GATE0
            c_col = cols[:, lane0 + ci:lane0 + ci + 1] - cols[:, lane0 + xi:lane0 + xi + 1]
            i_row, b_row = rows[xi:xi + 1, :], rows[ci:ci + 1, :]
            m_prev = m_scr[2 * hd + dr, c][0:1, :]
            mask = (s_idx <= t_idx) if dr == 0 else (s_idx >= t_idx)
            dmat = jnp.where(mask, b_row - c_col, -jnp.inf)
            inter = b_row + m_prev
            m_t = jnp.maximum(inter, jnp.max(dmat, axis=0, keepdims=True))
            pst = (pr["st"] * scale * jnp.exp(dmat - m_t)).astype(BF16)
            vt = vt_scr[hd, c]
            q_w = (pr["qt"].astype(F32) * (jnp.exp(inter - m_t) * scale)).astype(BF16)
            pr["vt"] = vt
            pr["rhs"] = jnp.concatenate([pst, q_w], axis=0)
            pr["floor"] = jnp.exp(-m_t)
            g = total_logf(b_row, dr)
            m_new = jnp.maximum(g + m_prev, jnp.max(g - b_row + i_row, axis=-1, keepdims=True))
            pr["decay"] = jnp.exp(g + m_prev - m_new)
            vw = (vt.astype(F32) * jnp.exp(g - b_row + i_row - m_new)).astype(BF16)
            pr["x"] = _dot(vw, pr["k"])
        cmats = {}
        for pr in probs:
            key = 2 * pr["hd"] + pr["dr"]
            cmat = cmats[key] if key in cmats else c_scr[key]
            pr["num"] = _dot(jnp.concatenate([pr["vt"], cmat.astype(BF16)], axis=1), pr["rhs"])
            cmats[key] = pr["decay"] * cmat + pr["x"]
        for key, cmat in cmats.items():
            c_scr[key] = cmat
        for pr in probs:
            hd, dr, c = pr["hd"], pr["dr"], pr["c"]
            num_aug = pr["num"]
            den = num_aug[d:d + 1, :]
            h_scr[2 * hd + dr, c] = num_aug[:d, :] / jnp.maximum(jnp.abs(den), pr["floor"])
        return carry

    lax.fori_loop(0, nc // cu, body, 0)
    for hd in range(hb):
        for c in range(nc):
            ht = h_scr[2 * hd, c] + h_scr[2 * hd + 1, c]
            ht = ht * lax.rsqrt(jnp.mean(ht * ht, axis=0, keepdims=True) + EPS)
            og = og_ref[0, c * l:(c + 1) * l, hd * d:(hd + 1) * d].astype(F32)
            o_ref[0, c * l:(c + 1) * l, hd * d:(hd + 1) * d] = (
                ht.T * nw_ref[...] * _sigmoid(og)).astype(BF16)


def _mlstm(proj, gcols, grows, norm_w):
    b, s, _ = proj.shape
    l = M_CHUNK
    nc = s // l
    hb = M_HEADS_PER_STEP
    cu = min(M_CHUNKS_PER_ITER, nc)
    d = HEAD_DIM
    blk = lambda off: pl.BlockSpec((1, s, hb * d), lambda i, h: (i, 0, off * HEADS // hb + h))
    return pl.pallas_call(
        functools.partial(_mlstm_kernel, l=l, nc=nc, hb=hb, cu=cu),
        grid=(b, HEADS // hb),
        in_specs=[blk(0), blk(1), blk(2), blk(3),
                  pl.BlockSpec((1, 1, s, LANES), lambda i, h: (i, h, 0, 0)),
                  pl.BlockSpec((1, hb, nc, 8, l), lambda i, h: (i, h, 0, 0, 0)),
                  pl.BlockSpec((1, d), lambda i, h: (0, 0))],
        out_specs=pl.BlockSpec((1, s, hb * d), lambda i, h: (i, 0, h)),
        out_shape=jax.ShapeDtypeStruct((b, s, HEADS * d), BF16),
        scratch_shapes=[pltpu.VMEM((hb, nc, d, l), BF16),
                        pltpu.VMEM((hb, nc, d + BF16_SUBLANES, l), BF16),
                        pltpu.VMEM((2 * hb, nc, d, l), F32),
                        pltpu.VMEM((2 * hb, d + BF16_SUBLANES, d), F32),
                        pltpu.VMEM((2 * hb, nc, 8, LANES), F32)],
        compiler_params=_params("parallel", "parallel"),
        name="mlstm",
    )(proj, proj, proj, proj, gcols, grows, norm_w.reshape(1, d))


def _pair_rows(rows_t, first, l):
    b, _, _, s = rows_t.shape
    rows = rows_t[:, :, first:first + 4].reshape(b, HEADS, 2, 2, s // l, l)
    rows = rows.transpose(0, 1, 4, 3, 2, 5).reshape(b, HEADS, s // l, 2, 2 * l)
    return jnp.pad(rows, ((0, 0), (0, 0), (0, 0), (0, 6), (0, 0)))


def _gdn_kernel(q_ref, k_ref, v_ref, z_ref, cw_ref, gcol_ref, grow_ref, nw_ref, o_ref,
                pad_scr, q_scr, k_scr, v_scr, kt_scr, qm_scr, n_scr, oa_scr, od_scr, st_scr,
                *, l, nc, hb, cu):
    s = q_ref.shape[1]
    d = HEAD_DIM
    scale = d ** -0.5
    half = CONV_K // 2

    for which in range(3):
        pad_scr[which, 0:8, :] = jnp.zeros((8, d), F32)
        pad_scr[which, 8 + s:16 + s, :] = jnp.zeros((8, d), F32)

    rb = min(s, 512)
    srcs = (q_ref, k_ref, v_ref)
    dsts = (q_scr, k_scr, v_scr)

    for hd in range(hb):
        def fill(i, carry, hd=hd):
            r0 = pl.multiple_of(i * rb, rb)
            for which in range(3):
                pad_scr[which, pl.ds(8 + r0, rb), :] = (
                    srcs[which][0, pl.ds(r0, rb), hd * d:(hd + 1) * d].astype(F32))
            return carry

        lax.fori_loop(0, s // rb, fill, 0)

        def block(i, carry, hd=hd):
            r0 = pl.multiple_of(i * rb, rb)
            for which in range(3):
                acc = pad_scr[which, pl.ds(r0 + 8 - half, rb), :] * cw_ref[which, hd, 0:1, :]
                for j in range(1, CONV_K):
                    acc = acc + (pad_scr[which, pl.ds(r0 + 8 - half + j, rb), :]
                                 * cw_ref[which, hd, j:j + 1, :])
                y = acc * _sigmoid(acc)
                if which < 2:
                    y = y * lax.rsqrt(jnp.sum(y * y, axis=-1, keepdims=True) + EPS)
                dsts[which][hd, pl.ds(r0, rb), :] = y.astype(BF16)
                if which == 1:
                    for cc in range(rb // l):
                        yc = y[cc * l:(cc + 1) * l, :]
                        kt_scr[hd, i * (rb // l) + cc] = jnp.concatenate([yc, yc], axis=0).T.astype(BF16)
            return carry

        lax.fori_loop(0, s // rb, block, 0)

    t_idx = lax.broadcasted_iota(jnp.int32, (l, 2 * l), 0)
    lane = lax.broadcasted_iota(jnp.int32, (l, 2 * l), 1)
    is_f = lane < l
    s_idx = jnp.where(is_f, lane, lane - l)
    incl = jnp.where(is_f, t_idx - s_idx, s_idx - t_idx) >= 0
    diag = t_idx == s_idx
    eye2 = diag.astype(F32)
    zeros_b = jnp.zeros((l, d), BF16)

    def blockdiag(x2):
        return jnp.concatenate([jnp.where(is_f, x2, 0.0), jnp.where(is_f, 0.0, x2)], axis=0).astype(BF16)

    base = 16
    shift = base.bit_length() - 1
    same_base = (t_idx >> shift) == (s_idx >> shift)
    merge_masks = []
    size = base
    while size < l:
        sh = size.bit_length() - 1
        merge_masks.append(((t_idx >> (sh + 1)) == (s_idx >> (sh + 1))) & ((t_idx >> sh) != (s_idx >> sh)))
        size *= 2

    half_cu = cu // 2

    def b_step(j):
        chains = [(hd, dr, j if dr == 0 else nc - 1 - j) for hd in range(hb) for dr in range(2)]
        states = [st_scr[2 * hd + dr] for hd, dr, _ in chains]
        rs = [_dot(qm_scr[2 * hd + dr, c], st.astype(BF16)) for (hd, dr, c), st in zip(chains, states)]
        for (hd, dr, c), state, r in zip(chains, states, rs):
            idx = 2 * hd + dr
            gam_row = grow_ref[0, hd, c][0:1, :]
            g_last = gam_row[:, l - 1:l] if dr == 0 else gam_row[:, l:l + 1]
            st_scr[idx] = jnp.exp(g_last) * state + r[:d] + n_scr[idx, c]
            od_scr[idx, pl.ds(pl.multiple_of(c * l, l), l), :] = r[d:].astype(BF16)

    def phase_a(it, b_steps):
        chunk_ids = ([it * half_cu + cc for cc in range(half_cu)]
                     + [nc - half_cu * (it + 1) + cc for cc in range(half_cu)])
        probs = [dict(c=c, hd=hd) for c in chunk_ids for hd in range(hb)]
        pending = list(b_steps)

        def slot():
            if pending:
                b_step(pending.pop(0))

        for pr in probs:
            c, hd = pr["c"], pr["hd"]
            r0 = pl.multiple_of(c * l, l)
            qb = q_scr[hd, pl.ds(r0, l), :]
            kb = k_scr[hd, pl.ds(r0, l), :]
            pr["kq"] = _dot(jnp.concatenate([qb, kb], axis=0), kt_scr[hd, c])
        for pr in probs:
            c, hd = pr["c"], pr["hd"]
            cols = gcol_ref[0, 0, pl.ds(pl.multiple_of(c * l, l), l), :]
            col = lambda j: cols[:, 8 * hd + G_GATE0 + j:8 * hd + G_GATE0 + j + 1]
            gam_f, beta_f, gam_b, beta_b = col(G_C_F), col(G_X_F), col(G_C_B), col(G_X_B)
            gam_row = grow_ref[0, hd, c][0:1, :]
            e_incl = jnp.exp(jnp.where(incl, jnp.where(is_f, gam_f, gam_b) - gam_row, -jnp.inf))
            pr["attn2"] = pr["kq"][:l] * scale * e_incl
            a2 = jnp.where(is_f, beta_f, beta_b) * pr["kq"][l:] * jnp.where(diag, 0.0, e_incl)
            pr["a2"] = a2
            dblk = jnp.where(same_base, a2, 0.0)
            pr["p"] = eye2 - dblk
            pr["dpow"] = _dot(dblk.astype(BF16), blockdiag(dblk))
        power = 2
        while 2 * power < base:
            for pr in probs:
                res = _dot(jnp.concatenate([pr["p"], pr["dpow"]], axis=0).astype(BF16),
                           blockdiag(pr["dpow"]))
                pr["p"] = pr["p"] + res[:l]
                pr["dpow"] = res[l:]
            power *= 2
        for pr in probs:
            pr["p"] = pr["p"] + _dot(pr["p"].astype(BF16), blockdiag(pr["dpow"]))
        slot()
        for mask in merge_masks:
            for pr in probs:
                pr["x"] = _dot(jnp.where(mask, pr["a2"], 0.0).astype(BF16), blockdiag(pr["p"]))
            for pr in probs:
                pr["p"] = pr["p"] - _dot(pr["p"].astype(BF16), blockdiag(pr["x"]))
            slot()
        for pr in probs:
            c, hd = pr["c"], pr["hd"]
            r0 = pl.multiple_of(c * l, l)
            kb = k_scr[hd, pl.ds(r0, l), :]
            vb = v_scr[hd, pl.ds(r0, l), :]
            rows = grow_ref[0, hd, c]
            gam_row, beta_row = rows[0:1, :], rows[1:2, :]
            tk = (pr["p"] * (-beta_row * jnp.exp(gam_row))).astype(BF16)
            tv = (pr["p"] * beta_row).astype(BF16)
            pr["wk"] = _dot(tk, jnp.concatenate([jnp.concatenate([kb, zeros_b], axis=-1),
                                                 jnp.concatenate([zeros_b, kb], axis=-1)], axis=0))
            pr["wv"] = _dot(tv, jnp.concatenate([jnp.concatenate([vb, zeros_b], axis=-1),
                                                 jnp.concatenate([zeros_b, vb], axis=-1)], axis=0))
        slot()
        for pr in probs:
            c, hd = pr["c"], pr["hd"]
            gam_row = grow_ref[0, hd, c][0:1, :]
            g_last = jnp.where(is_f[0:1, :], gam_row[:, l - 1:l], gam_row[:, l:l + 1])
            wk, wv = pr["wk"].astype(BF16), pr["wv"].astype(BF16)
            wu_bd = jnp.concatenate(
                [jnp.concatenate([wk[:, :d], wv[:, :d], zeros_b, zeros_b], axis=-1),
                 jnp.concatenate([zeros_b, zeros_b, wk[:, d:], wv[:, d:]], axis=-1)], axis=0)
            k_dec_t = (kt_scr[hd, c].astype(F32) * jnp.exp(g_last - gam_row)).astype(BF16)
            pr["res"] = _dot(jnp.concatenate([pr["attn2"].astype(BF16), k_dec_t], axis=0), wu_bd)
        for pr in probs:
            c, hd = pr["c"], pr["hd"]
            r0 = pl.multiple_of(c * l, l)
            q = q_scr[hd, pl.ds(r0, l), :].astype(F32)
            cols = gcol_ref[0, 0, pl.ds(r0, l), :]
            col = lambda j: cols[:, 8 * hd + G_GATE0 + j:8 * hd + G_GATE0 + j + 1]
            gam_f, gam_b = col(G_C_F), col(G_C_B)
            res = pr["res"]
            qd_f = q * (jnp.exp(gam_f) * scale) + res[:l, 0:d]
            qd_b = q * (jnp.exp(gam_b) * scale) + res[:l, 2 * d:3 * d]
            qm_scr[2 * hd, c] = jnp.concatenate([res[l:, 0:d], qd_f], axis=0).astype(BF16)
            qm_scr[2 * hd + 1, c] = jnp.concatenate([res[l:, 2 * d:3 * d], qd_b], axis=0).astype(BF16)
            n_scr[2 * hd, c] = res[l:, d:2 * d].astype(BF16)
            n_scr[2 * hd + 1, c] = res[l:, 3 * d:].astype(BF16)
            oa_scr[hd, pl.ds(r0, l), :] = res[:l, d:2 * d] + res[:l, 3 * d:]
        while pending:
            slot()

    st_scr[...] = jnp.zeros_like(st_scr)
    n_iter = nc // cu
    phase_a(0, [])

    def body_ab(it, carry):
        phase_a(it, [(it - 1) * half_cu + u for u in range(half_cu)])
        return carry

    lax.fori_loop(1, n_iter, body_ab, 0)

    def body_b(j, carry):
        b_step(j)
        return carry

    def finish_chunk(c):
        r0 = pl.multiple_of(c * l, l)
        for hd in range(hb):
            z = z_ref[0, pl.ds(r0, l), hd * d:(hd + 1) * d].astype(F32)
            o = (oa_scr[hd, pl.ds(r0, l), :] + od_scr[2 * hd, pl.ds(r0, l), :]
                 + od_scr[2 * hd + 1, pl.ds(r0, l), :])
            o_ref[0, pl.ds(r0, l), hd * d:(hd + 1) * d] = (
                _rms(o, nw_ref[...]) * (z * _sigmoid(z))).astype(BF16)

    def body_b_finish(j, carry):
        finish_chunk(j - 1)
        finish_chunk(nc - j)
        b_step(j)
        return carry

    first_finish = nc // 2 + 1
    lax.fori_loop((n_iter - 1) * half_cu, first_finish, body_b, 0)
    lax.fori_loop(first_finish, nc, body_b_finish, 0)
    finish_chunk(nc - 1)
    finish_chunk(0)


def _gdn(proj, conv_w, gcols, grows, norm_w):
    b, s, _ = proj.shape
    l = G_CHUNK
    assert 2 * l == LANES
    nc = s // l
    hb = G_HEADS_PER_STEP
    cu = min(G_CHUNKS_PER_ITER, nc)
    d = HEAD_DIM
    blk = lambda off: pl.BlockSpec((1, s, hb * d), lambda i, h: (i, 0, off * HEADS // hb + h))
    return pl.pallas_call(
        functools.partial(_gdn_kernel, l=l, nc=nc, hb=hb, cu=cu),
        grid=(b, HEADS // hb),
        in_specs=[blk(4), blk(5), blk(6), blk(7),
                  pl.BlockSpec((3, hb, 8, d), lambda i, h: (0, h, 0, 0)),
                  pl.BlockSpec((1, 1, s, LANES), lambda i, h: (i, h, 0, 0)),
                  pl.BlockSpec((1, hb, nc, 8, 2 * l), lambda i, h: (i, h, 0, 0, 0)),
                  pl.BlockSpec((1, d), lambda i, h: (0, 0))],
        out_specs=pl.BlockSpec((1, s, hb * d), lambda i, h: (i, 0, h)),
        out_shape=jax.ShapeDtypeStruct((b, s, HEADS * d), BF16),
        scratch_shapes=[pltpu.VMEM((3, s + 16, d), F32),
                        pltpu.VMEM((hb, s, d), BF16), pltpu.VMEM((hb, s, d), BF16),
                        pltpu.VMEM((hb, s, d), BF16),
                        pltpu.VMEM((hb, nc, d, 2 * l), BF16),
                        pltpu.VMEM((2 * hb, nc, l + d, d), BF16),
                        pltpu.VMEM((2 * hb, nc, d, d), BF16),
                        pltpu.VMEM((hb, s, d), F32),
                        pltpu.VMEM((2 * hb, s, d), BF16),
                        pltpu.VMEM((2 * hb, d, d), F32)],
        compiler_params=_params("parallel", "parallel"),
        name="gdn",
    )(proj, proj, proj, proj, conv_w, gcols, grows, norm_w.reshape(1, d))


def kernel(x, c, w_ada, b_ada, norm_ffn1, w_ffn1_in, w_ffn1_out, norm_mix, w_in, mlstm_gate_bias,
           gdn_a_log, gdn_dt_bias, gdn_conv_w, mlstm_out_norm, gdn_out_norm, w_branch_mlstm,
           w_branch_gdn, w_out, norm_ffn2, w_ffn2_in, w_ffn2_out, norm_final):
    b, s, d = x.shape
    depth = w_ada.shape[0]
    width = HEADS * HEAD_DIM
    ng = 4 * HEADS
    sizes = (width,) * 4 + (ng,) + (width,) * 4 + (ng,) + (d, d)
    offs = [0]
    for sz in sizes:
        offs.append(offs[-1] + sz)
    for layer in range(depth):
        mod = _ada(c, w_ada[layer], b_ada[layer]).reshape(b, 9, d)
        x = _ffn(x, mod, norm_ffn1[layer], w_ffn1_in[layer], w_ffn1_out[layer], norm_final,
                 sub=0, final=False)

        wl = w_in[layer]
        col = lambda i: wl[:, offs[i]:offs[i + 1]]
        w_big = jnp.concatenate([col(i).astype(BF16) for i in (0, 1, 2, 3, 5, 6, 7, 8, 10, 11)], axis=1)
        head_major = lambda t: t.reshape(t.shape[:-1] + (4, HEADS)).swapaxes(-1, -2)
        pad_lanes = lambda t: jnp.pad(t.reshape(t.shape[:-2] + (8 * HEADS,)),
                                      [(0, 0)] * (t.ndim - 2) + [(0, LANES - 8 * HEADS)])
        w_gates = pad_lanes(jnp.concatenate([head_major(col(4)), head_major(col(9))], axis=-1)).astype(BF16)
        proj, gates_raw = _proj(x, mod, norm_mix[layer], w_big, w_gates)

        zeros_h = jnp.zeros((HEADS,), F32)
        dt_rows = jnp.stack([gdn_dt_bias[layer][0], zeros_h, gdn_dt_bias[layer][1], zeros_h]).reshape(ng)
        alog_rows = jnp.stack([gdn_a_log[layer][0], zeros_h, gdn_a_log[layer][1], zeros_h]).reshape(ng)
        bias_row = pad_lanes(jnp.concatenate([head_major(mlstm_gate_bias[layer].reshape(ng)),
                                              head_major(dt_rows)], axis=-1))
        alog_row = pad_lanes(jnp.concatenate([jnp.zeros((HEADS, 4), F32), head_major(alog_rows)], axis=-1))
        gparams = jnp.zeros((8, LANES), F32).at[0].set(bias_row).at[1].set(alog_row)
        assert M_HEADS_PER_STEP == G_HEADS_PER_STEP
        gate_cols, gate_rows = _gate_prep(gates_raw, gparams, HEADS // M_HEADS_PER_STEP)
        gate_rows = gate_rows.reshape(b, HEADS, 8, s)
        m_rows = _chunk_rows(gate_rows, 0, M_CHUNK)
        g_rows = _pair_rows(gate_rows, 4, G_CHUNK)

        h_m = _mlstm(proj, gate_cols, m_rows, mlstm_out_norm[layer])
        conv_w = gdn_conv_w[layer].reshape(CONV_K, 3, HEADS, HEAD_DIM).transpose(1, 2, 0, 3)
        conv_w = jnp.pad(conv_w, ((0, 0), (0, 0), (0, 8 - CONV_K), (0, 0)))
        h_g = _gdn(proj, conv_w, gate_cols, g_rows, gdn_out_norm[layer])

        last = layer == depth - 1
        x = _ffn(x, mod, norm_ffn2[layer], w_ffn2_in[layer], w_ffn2_out[layer], norm_final,
                 sub=2, final=last,
                 merge=(h_m, h_g, proj, w_branch_mlstm[layer].astype(BF16),
                        w_branch_gdn[layer].astype(BF16), w_out[layer].astype(BF16)))
    return x
```
